```python
import jax, jax.numpy as jnp
from jax import lax
import numpy as np

D_MODEL = 2048
BATCH = 1
SEQ = 8192
DEPTH = 1

HEAD_DIM = 128
ATTN_WIDTH = D_MODEL // 2
ATTN_HEADS = ATTN_WIDTH // HEAD_DIM
POOL_WIDTH = D_MODEL // 2
POOL_WINDOWS = (2, 4, 8, 16)
POOL_GROUPS = len(POOL_WINDOWS)
POOL_GROUP_WIDTH = POOL_WIDTH // POOL_GROUPS
DILATED_PATTERNS = ((128, 1), (512, 4), (2048, 16))
SUB_BLOCK = 128
D_FF = 4 * D_MODEL
ROPE_THETA = 10000.0
LN_EPS = 1e-5
DEEPNORM_ALPHA = (2.0 * DEPTH) ** 0.25
DEEPNORM_BETA = (8.0 * DEPTH) ** -0.25
IN_SPLITS = (ATTN_WIDTH, 2 * ATTN_WIDTH, 3 * ATTN_WIDTH, 3 * ATTN_WIDTH + POOL_WIDTH,
             3 * ATTN_WIDTH + POOL_WIDTH + D_MODEL)
IN_WIDTH = 3 * ATTN_WIDTH + POOL_WIDTH + 2 * D_MODEL

kernel_name = "dilated_attn_pool_gated_hybrid_deepnorm"


def layer_norm(x, g, b):
    xf = x.astype(jnp.float32)
    mu = jnp.mean(xf, axis=-1, keepdims=True)
    var = jnp.mean(jnp.square(xf - mu), axis=-1, keepdims=True)
    return ((xf - mu) * lax.rsqrt(var + LN_EPS) * g.astype(jnp.float32) + b.astype(jnp.float32)).astype(x.dtype)


def rope(t, positions):
    half = HEAD_DIM // 2
    inv_freq = ROPE_THETA ** (-jnp.arange(half, dtype=jnp.float32) / half)
    ang = positions.astype(jnp.float32)[..., None] * inv_freq
    cos = jnp.cos(ang)[:, :, None, :]
    sin = jnp.sin(ang)[:, :, None, :]
    t1 = t[..., :half].astype(jnp.float32)
    t2 = t[..., half:].astype(jnp.float32)
    return jnp.concatenate([t1 * cos - t2 * sin, t2 * cos + t1 * sin], axis=-1).astype(t.dtype)


def dilated_window_attention(q, k, v, window, dilation):
    B, S, H, Dh = q.shape
    span = window // dilation
    blk = SUB_BLOCK
    assert span <= blk
    unit = dilation * blk
    s_pad = -(-S // unit) * unit
    m_len = s_pad // dilation
    nb = m_len // blk

    def to_blocks(t):
        t = jnp.pad(t, ((0, 0), (0, s_pad - S), (0, 0), (0, 0)))
        t = t.reshape(B, m_len, dilation, H, Dh).transpose(0, 2, 1, 3, 4)
        return t.reshape(B, dilation, nb, blk, H, Dh)

    def with_prev(t):
        prev = jnp.pad(t, ((0, 0), (0, 0), (1, 0), (0, 0), (0, 0), (0, 0)))[:, :, :-1]
        return jnp.concatenate([prev, t], axis=3)

    qb = to_blocks(q)
    kw = with_prev(to_blocks(k))
    vw = with_prev(to_blocks(v))
    s = jnp.einsum('brnqhd,brnkhd->brnhqk', qb, kw,
                   preferred_element_type=jnp.float32) * (HEAD_DIM ** -0.5)
    qi = jnp.arange(blk)[:, None]
    kj = jnp.arange(2 * blk)[None, :]
    dist = qi + blk - kj
    band = (dist >= 0) & (dist <= span)
    valid = band[None] & ((jnp.arange(nb)[:, None, None] > 0) | (kj >= blk)[None])
    s = jnp.where(valid[None, None, :, None], s, -jnp.inf)
    mx = jnp.max(s, axis=-1, keepdims=True)
    p = jnp.exp(s - mx)
    l = jnp.sum(p, axis=-1)
    o = jnp.einsum('brnhqk,brnkhd->brnqhd', p, vw.astype(jnp.float32))
    o = o / jnp.swapaxes(l, 3, 4)[..., None]
    lse = jnp.swapaxes(mx[..., 0] + jnp.log(l), 3, 4)
    o = o.reshape(B, dilation, m_len, H, Dh).transpose(0, 2, 1, 3, 4).reshape(B, s_pad, H, Dh)[:, :S]
    lse = lse.reshape(B, dilation, m_len, H).transpose(0, 2, 1, 3).reshape(B, s_pad, H)[:, :S]
    return o, lse


def pool_mixer(u, w_pool, pool_scale):
    B, S, _ = u.shape
    ug = u.reshape(B, S, POOL_GROUPS, POOL_GROUP_WIDTH)
    pooled = []
    for g, w in enumerate(POOL_WINDOWS):
        xg = ug[:, :, g].astype(jnp.float32)
        c = jnp.cumsum(xg, axis=1)
        c_lag = jnp.pad(c, ((0, 0), (w, 0), (0, 0)))[:, :S]
        count = jnp.minimum(jnp.arange(1, S + 1), w).astype(jnp.float32)[None, :, None]
        pooled.append((c - c_lag) / count - xg)
    p = jnp.stack(pooled, axis=2).astype(u.dtype)
    y = jnp.einsum('bsgc,gcd->bsgd', p, w_pool).reshape(B, S, POOL_WIDTH)
    return y * pool_scale


def setup_inputs(seed: int = 0) -> dict:
    key = jax.random.key(seed)
    ks = jax.random.split(key, 20)
    f32 = jnp.float32

    def nrm(k, shape, fan_in, gain=1.0):
        return jax.random.normal(k, shape, f32) * (gain * fan_in ** -0.5)

    x = jax.random.normal(ks[0], (BATCH, SEQ, D_MODEL), f32)
    positions = (jnp.arange(SEQ, dtype=jnp.int32)[None, :]
                 + jax.random.randint(ks[1], (BATCH, 1), 0, 1024, dtype=jnp.int32))
    w_in = jnp.concatenate([
        nrm(ks[2], (DEPTH, D_MODEL, 2 * ATTN_WIDTH), D_MODEL),
        nrm(ks[3], (DEPTH, D_MODEL, ATTN_WIDTH), D_MODEL, DEEPNORM_BETA),
        nrm(ks[4], (DEPTH, D_MODEL, POOL_WIDTH), D_MODEL),
        nrm(ks[5], (DEPTH, D_MODEL, 2 * D_MODEL), D_MODEL),
    ], axis=-1)
    w_pool = nrm(ks[6], (DEPTH, POOL_GROUPS, POOL_GROUP_WIDTH, POOL_GROUP_WIDTH), POOL_GROUP_WIDTH)
    pool_scale = 1.0 + 0.1 * jax.random.normal(ks[7], (DEPTH, POOL_WIDTH), f32)
    w_branch_attn = nrm(ks[8], (DEPTH, ATTN_WIDTH, D_MODEL), ATTN_WIDTH, DEEPNORM_BETA)
    w_branch_pool = nrm(ks[9], (DEPTH, POOL_WIDTH, D_MODEL), POOL_WIDTH, DEEPNORM_BETA)
    w_out = nrm(ks[10], (DEPTH, D_MODEL, D_MODEL), D_MODEL, DEEPNORM_BETA)
    ln_mix_g = 1.0 + 0.05 * jax.random.normal(ks[11], (DEPTH, D_MODEL), f32)
    ln_mix_b = 0.02 * jax.random.normal(ks[12], (DEPTH, D_MODEL), f32)
    w_ff1 = nrm(ks[13], (DEPTH, D_MODEL, D_FF), D_MODEL, DEEPNORM_BETA)
    w_ff2 = nrm(ks[14], (DEPTH, D_FF, D_MODEL), D_FF, DEEPNORM_BETA)
    ln_ff_g = 1.0 + 0.05 * jax.random.normal(ks[15], (DEPTH, D_MODEL), f32)
    ln_ff_b = 0.02 * jax.random.normal(ks[16], (DEPTH, D_MODEL), f32)
    return {"x": x, "positions": positions, "w_in": w_in, "w_pool": w_pool,
            "pool_scale": pool_scale, "w_branch_attn": w_branch_attn,
            "w_branch_pool": w_branch_pool, "w_out": w_out, "ln_mix_g": ln_mix_g,
            "ln_mix_b": ln_mix_b, "w_ff1": w_ff1, "w_ff2": w_ff2,
            "ln_ff_g": ln_ff_g, "ln_ff_b": ln_ff_b}


def reference(x, positions, w_in, w_pool, pool_scale, w_branch_attn, w_branch_pool, w_out,
              ln_mix_g, ln_mix_b, w_ff1, w_ff2, ln_ff_g, ln_ff_b):
    B, S, _ = x.shape
    for layer in range(DEPTH):
        h = x @ w_in[layer]
        q, k, v, u, gate_attn, gate_pool = jnp.split(h, IN_SPLITS, axis=-1)
        q = rope(q.reshape(B, S, ATTN_HEADS, HEAD_DIM), positions)
        k = rope(k.reshape(B, S, ATTN_HEADS, HEAD_DIM), positions)
        v = v.reshape(B, S, ATTN_HEADS, HEAD_DIM)
        outs, lses = [], []
        for window, dilation in DILATED_PATTERNS:
            o_g, lse_g = dilated_window_attention(q, k, v, window, dilation)
            outs.append(o_g)
            lses.append(lse_g)
        mix_w = jax.nn.softmax(jnp.stack(lses, axis=0), axis=0)
        o_attn = jnp.einsum('pbsh,pbshd->bshd', mix_w, jnp.stack(outs, axis=0))
        y_attn = o_attn.reshape(B, S, ATTN_WIDTH).astype(x.dtype) @ w_branch_attn[layer]
        y_pool = pool_mixer(u, w_pool[layer], pool_scale[layer]) @ w_branch_pool[layer]
        merged = jax.nn.sigmoid(gate_attn) * y_attn + jax.nn.sigmoid(gate_pool) * y_pool
        mix = merged @ w_out[layer]
        x = layer_norm(DEEPNORM_ALPHA * x + mix, ln_mix_g[layer], ln_mix_b[layer])
        f = jnp.square(jax.nn.relu(x @ w_ff1[layer])) @ w_ff2[layer]
        x = layer_norm(DEEPNORM_ALPHA * x + f, ln_ff_g[layer], ln_ff_b[layer])
    return x
```

```python
import functools

import jax
import jax.numpy as jnp
from jax import lax
from jax.experimental import pallas as pl
from jax.experimental.pallas import tpu as pltpu

F32 = jnp.float32
BF16 = jnp.bfloat16

D_MODEL = 2048
SEQ = 8192
HEAD_DIM = 128
ATTN_WIDTH = D_MODEL // 2
ATTN_HEADS = ATTN_WIDTH // HEAD_DIM
POOL_WIDTH = D_MODEL // 2
POOL_WINDOWS = (2, 4, 8, 16)
POOL_GROUP_WIDTH = POOL_WIDTH // len(POOL_WINDOWS)
MAX_POOL_WINDOW = max(POOL_WINDOWS)
DILATIONS = (1, 4, 16)
SUB_BLOCK = 128
D_FF = 4 * D_MODEL
IN_WIDTH = 3 * ATTN_WIDTH + POOL_WIDTH + 2 * D_MODEL
ROPE_THETA = 10000.0
LN_EPS = 1e-5
DEPTH = 1
DEEPNORM_ALPHA = (2.0 * DEPTH) ** 0.25
SM_SCALE = HEAD_DIM ** -0.5
MASK_VALUE = -1e30

VMEM_LIMIT_BYTES = 56 * 1024 * 1024

ATTN_CHUNK = max(DILATIONS) * SUB_BLOCK
PROJ_TM, PROJ_TN = 1024, 1024
MIX_TM = 256
FFN_TM, FFN_TF = 512, 512


def _layer_norm(y, g, b):
    mu = jnp.mean(y, axis=-1, keepdims=True)
    yc = y - mu
    var = jnp.mean(yc * yc, axis=-1, keepdims=True)
    return yc * lax.rsqrt(var + LN_EPS) * g + b


def _proj_kernel(x_ref, pos_ref, invf_ref, w_ref, o_ref, xb_ref, cos_ref, sin_ref, *, n_rope_blocks):
    j = pl.program_id(1)

    @pl.when(j == 0)
    def _():
        xb_ref[...] = x_ref[...].astype(BF16)
        ang = pos_ref[...].astype(F32) * invf_ref[...]
        lane = lax.broadcasted_iota(jnp.int32, (1, HEAD_DIM), 1)
        sign = jnp.where(lane < HEAD_DIM // 2, -1.0, 1.0).astype(F32)
        cos_ref[...] = jnp.cos(ang)
        sin_ref[...] = jnp.sin(ang) * sign

    acc = jnp.dot(xb_ref[...], w_ref[...], preferred_element_type=F32)

    @pl.when(j < n_rope_blocks)
    def _():
        cos = cos_ref[...]
        sin = sin_ref[...]
        for hh in range(acc.shape[1] // HEAD_DIM):
            t = acc[:, hh * HEAD_DIM:(hh + 1) * HEAD_DIM]
            o_ref[:, hh * HEAD_DIM:(hh + 1) * HEAD_DIM] = t * cos + pltpu.roll(t, HEAD_DIM // 2, 1) * sin

    @pl.when(j >= n_rope_blocks)
    def _():
        o_ref[...] = acc


def _proj(x2, pos2, invf2, w_in_b):
    s, d = x2.shape
    tm, tn = PROJ_TM, PROJ_TN
    n_rope_blocks = (2 * ATTN_WIDTH) // tn
    return pl.pallas_call(
        functools.partial(_proj_kernel, n_rope_blocks=n_rope_blocks),
        grid=(s // tm, IN_WIDTH // tn),
        in_specs=[
            pl.BlockSpec((tm, d), lambda i, j: (i, 0)),
            pl.BlockSpec((tm, 1), lambda i, j: (i, 0)),
            pl.BlockSpec((1, HEAD_DIM), lambda i, j: (0, 0)),
            pl.BlockSpec((d, tn), lambda i, j: (0, j)),
        ],
        out_specs=pl.BlockSpec((tm, tn), lambda i, j: (i, j)),
        out_shape=jax.ShapeDtypeStruct((s, IN_WIDTH), F32),
        scratch_shapes=[
            pltpu.VMEM((tm, d), BF16),
            pltpu.VMEM((tm, HEAD_DIM), F32),
            pltpu.VMEM((tm, HEAD_DIM), F32),
        ],
        compiler_params=pltpu.CompilerParams(
            dimension_semantics=("arbitrary", "arbitrary"), vmem_limit_bytes=VMEM_LIMIT_BYTES),
        name="proj_rope",
    )(x2, pos2, invf2, w_in_b)


def _attn_kernel(q_ref, kc_ref, kp_ref, vc_ref, vp_ref, o_ref,
                 qd_ref, kd_ref, vd_ref, bias_ref, acc_refs, m_refs, l_refs):
    n = pl.program_id(0)
    blk = SUB_BLOCK

    qi = lax.broadcasted_iota(jnp.int32, (blk, 2 * blk), 0)
    kj = lax.broadcasted_iota(jnp.int32, (blk, 2 * blk), 1)
    band = (kj >= qi) & (kj <= qi + blk)
    bias_ref[0] = jnp.where(band, 0.0, MASK_VALUE).astype(F32)
    bias_ref[1] = jnp.where(band & (kj >= blk), 0.0, MASK_VALUE).astype(F32)

    for p, dil in enumerate(DILATIONS):
        m_len = ATTN_CHUNK // dil
        nb = m_len // blk
        krows = blk + m_len

        for r in range(dil):
            qd_ref[pl.ds(r * m_len, m_len), :] = q_ref[pl.ds(r, m_len, stride=dil), :].astype(BF16)
            kd_ref[pl.ds(r * krows, blk), :] = (
                kp_ref[pl.ds((m_len - blk) * dil + r, blk, stride=dil), :].astype(BF16))
            kd_ref[pl.ds(r * krows + blk, m_len), :] = kc_ref[pl.ds(r, m_len, stride=dil), :].astype(BF16)
            vd_ref[pl.ds(r * krows, blk), :] = (
                vp_ref[pl.ds((m_len - blk) * dil + r, blk, stride=dil), :].astype(BF16))
            vd_ref[pl.ds(r * krows + blk, m_len), :] = vc_ref[pl.ds(r, m_len, stride=dil), :].astype(BF16)

        acc_ref, m_ref, l_ref = acc_refs[p], m_refs[p], l_refs[p]

        def unit(t, carry, dil=dil, m_len=m_len, nb=nb, krows=krows,
                 acc_ref=acc_ref, m_ref=m_ref, l_ref=l_ref):
            r = t // nb
            b = t % nb
            q0 = pl.multiple_of(r * m_len + b * blk, blk)
            k0 = pl.multiple_of(r * krows + b * blk, blk)
            qb = qd_ref[pl.ds(q0, blk), :]
            kb = kd_ref[pl.ds(k0, 2 * blk), :]
            vb = vd_ref[pl.ds(k0, 2 * blk), :]
            s = lax.dot_general(qb, kb, (((1,), (1,)), ((), ())), preferred_element_type=F32)
            first = jnp.logical_and(n == 0, b == 0).astype(jnp.int32)
            s = s + bias_ref[first]
            m_raw = jnp.max(s, axis=-1, keepdims=True)
            pr = jnp.exp((s - m_raw) * SM_SCALE)
            l = jnp.sum(pr, axis=-1, keepdims=True)
            acc = jnp.dot(pr.astype(BF16), vb, preferred_element_type=F32)
            row0 = b * (blk * dil) + r
            acc_ref[pl.ds(row0, blk, stride=dil), :] = acc
            m_ref[pl.ds(row0, blk, stride=dil), :] = jnp.broadcast_to(m_raw * SM_SCALE, (blk, HEAD_DIM))
            l_ref[pl.ds(row0, blk, stride=dil), :] = jnp.broadcast_to(l, (blk, HEAD_DIM))
            return carry

        lax.fori_loop(0, dil * nb, unit, 0)

    def combine(c, carry):
        rows = pl.ds(pl.multiple_of(c * blk, blk), blk)
        m0, m1, m2 = m_refs[0][rows, :], m_refs[1][rows, :], m_refs[2][rows, :]
        mm = jnp.maximum(jnp.maximum(m0, m1), m2)
        w0, w1, w2 = jnp.exp(m0 - mm), jnp.exp(m1 - mm), jnp.exp(m2 - mm)
        num = w0 * acc_refs[0][rows, :] + w1 * acc_refs[1][rows, :] + w2 * acc_refs[2][rows, :]
        den = w0 * l_refs[0][rows, :] + w1 * l_refs[1][rows, :] + w2 * l_refs[2][rows, :]
        o_ref[rows, :] = (num / den).astype(o_ref.dtype)
        return carry

    lax.fori_loop(0, ATTN_CHUNK // blk, combine, 0)


def _attn(h):
    s = h.shape[0]
    c = ATTN_CHUNK
    blk = SUB_BLOCK
    kv_rows = max(DILATIONS) * blk + c
    chunk = lambda col0: pl.BlockSpec((c, HEAD_DIM), lambda n, hh: (n, col0 + hh))
    prev = lambda col0: pl.BlockSpec((c, HEAD_DIM), lambda n, hh: (jnp.maximum(n - 1, 0), col0 + hh))

    def body(q_ref, kc_ref, kp_ref, vc_ref, vp_ref, o_ref, qd, kd, vd, bias,
             a0, a1, a2, m0, m1, m2, l0, l1, l2):
        _attn_kernel(q_ref, kc_ref, kp_ref, vc_ref, vp_ref, o_ref, qd, kd, vd, bias,
                     (a0, a1, a2), (m0, m1, m2), (l0, l1, l2))

    return pl.pallas_call(
        body,
        grid=(s // c, ATTN_HEADS),
        in_specs=[chunk(0), chunk(ATTN_HEADS), prev(ATTN_HEADS), chunk(2 * ATTN_HEADS), prev(2 * ATTN_HEADS)],
        out_specs=pl.BlockSpec((c, HEAD_DIM), lambda n, hh: (n, hh)),
        out_shape=jax.ShapeDtypeStruct((s, ATTN_WIDTH), BF16),
        scratch_shapes=[
            pltpu.VMEM((c, HEAD_DIM), BF16),
            pltpu.VMEM((kv_rows, HEAD_DIM), BF16),
            pltpu.VMEM((kv_rows, HEAD_DIM), BF16),
            pltpu.VMEM((2, blk, 2 * blk), F32),
        ] + [pltpu.VMEM((c, HEAD_DIM), F32)] * 9,
        compiler_params=pltpu.CompilerParams(
            dimension_semantics=("arbitrary", "arbitrary"), vmem_limit_bytes=VMEM_LIMIT_BYTES),
        name="dilated_attn",
    )(h, h, h, h, h)


def _mix_kernel(oat_ref, u_ref, uh_ref, ga_ref, gp_ref, x_ref, wpool_ref, pscale_ref,
                wba_ref, wbb_ref, wout_ref, g_ref, b_ref, out_ref, ubuf_ref, pm_ref):
    i = pl.program_id(0)
    tm = u_ref.shape[0]
    halo = MAX_POOL_WINDOW

    @pl.when(i == 0)
    def _():
        ubuf_ref[0:halo, :] = jnp.zeros((halo, POOL_WIDTH), F32)

    @pl.when(i > 0)
    def _():
        ubuf_ref[0:halo, :] = uh_ref[...]

    ubuf_ref[halo:halo + tm, :] = u_ref[...]

    t_glob = i * tm + lax.broadcasted_iota(jnp.int32, (tm, 1), 0)
    for g, w in enumerate(POOL_WINDOWS):
        cols = slice(g * POOL_GROUP_WIDTH, (g + 1) * POOL_GROUP_WIDTH)
        ug = u_ref[:, cols]
        wsum = ug
        for jj in range(1, w):
            wsum = wsum + ubuf_ref[halo - jj:halo - jj + tm, cols]
        count = jnp.minimum(t_glob + 1, w).astype(F32)
        pooled = wsum / count - ug
        y = jnp.dot(pooled.astype(BF16), wpool_ref[g], preferred_element_type=F32)
        pm_ref[:, cols] = (y * pscale_ref[:, cols]).astype(BF16)

    y_attn = jnp.dot(oat_ref[...], wba_ref[...], preferred_element_type=F32)
    y_pool = jnp.dot(pm_ref[...], wbb_ref[...], preferred_element_type=F32)
    merged = jax.nn.sigmoid(ga_ref[...]) * y_attn + jax.nn.sigmoid(gp_ref[...]) * y_pool
    mix = jnp.dot(merged.astype(BF16), wout_ref[...], preferred_element_type=F32)
    out_ref[...] = _layer_norm(DEEPNORM_ALPHA * x_ref[...] + mix, g_ref[...], b_ref[...])


def _mix(o_attn, h, x2, w_pool_b, pool_scale, w_ba_b, w_bb_b, w_out_b, ln_g, ln_b):
    s, d = x2.shape
    tm = MIX_TM
    halo = MAX_POOL_WINDOW
    u_col = (3 * ATTN_WIDTH) // POOL_WIDTH
    ga_col = (3 * ATTN_WIDTH + POOL_WIDTH) // d
    resident = lambda shape: pl.BlockSpec(shape, lambda i: (0,) * len(shape), pipeline_mode=pl.Buffered(1))
    return pl.pallas_call(
        _mix_kernel,
        grid=(s // tm,),
        in_specs=[
            pl.BlockSpec((tm, ATTN_WIDTH), lambda i: (i, 0)),
            pl.BlockSpec((tm, POOL_WIDTH), lambda i: (i, u_col)),
            pl.BlockSpec((halo, POOL_WIDTH), lambda i: (jnp.maximum(i * (tm // halo) - 1, 0), u_col)),
            pl.BlockSpec((tm, d), lambda i: (i, ga_col)),
            pl.BlockSpec((tm, d), lambda i: (i, ga_col + 1)),
            pl.BlockSpec((tm, d), lambda i: (i, 0)),
            resident(w_pool_b.shape),
            resident((1, POOL_WIDTH)),
            resident(w_ba_b.shape),
            resident(w_bb_b.shape),
            resident(w_out_b.shape),
            resident((1, d)),
            resident((1, d)),
        ],
        out_specs=pl.BlockSpec((tm, d), lambda i: (i, 0)),
        out_shape=jax.ShapeDtypeStruct((s, d), F32),
        scratch_shapes=[
            pltpu.VMEM((halo + tm, POOL_WIDTH), F32),
            pltpu.VMEM((tm, POOL_WIDTH), BF16),
        ],
        compiler_params=pltpu.CompilerParams(
            dimension_semantics=("arbitrary",), vmem_limit_bytes=VMEM_LIMIT_BYTES),
        name="mix_ln",
    )(o_attn, h, h, h, h, x2, w_pool_b, pool_scale, w_ba_b, w_bb_b, w_out_b, ln_g, ln_b)


def _ffn_kernel(x_ref, w1_ref, w2_ref, g_ref, b_ref, out_ref, xb_ref, acc_ref):
    j = pl.program_id(1)

    @pl.when(j == 0)
    def _():
        xb_ref[...] = x_ref[...].astype(BF16)

    hid = jnp.dot(xb_ref[...], w1_ref[...], preferred_element_type=F32)
    hid = jnp.square(jnp.maximum(hid, 0.0)).astype(BF16)
    part = jnp.dot(hid, w2_ref[...], preferred_element_type=F32)

    @pl.when(j == 0)
    def _():
        acc_ref[...] = part

    @pl.when(j > 0)
    def _():
        acc_ref[...] += part

    @pl.when(j == pl.num_programs(1) - 1)
    def _():
        out_ref[...] = _layer_norm(DEEPNORM_ALPHA * x_ref[...] + acc_ref[...], g_ref[...], b_ref[...])


def _ffn(x1, w1_b, w2_b, ln_g, ln_b):
    s, d = x1.shape
    tm, tf = FFN_TM, FFN_TF
    return pl.pallas_call(
        _ffn_kernel,
        grid=(s // tm, D_FF // tf),
        in_specs=[
            pl.BlockSpec((tm, d), lambda i, j: (i, 0)),
            pl.BlockSpec((d, tf), lambda i, j: (0, j)),
            pl.BlockSpec((tf, d), lambda i, j: (j, 0)),
            pl.BlockSpec((1, d), lambda i, j: (0, 0)),
            pl.BlockSpec((1, d), lambda i, j: (0, 0)),
        ],
        out_specs=pl.BlockSpec((tm, d), lambda i, j: (i, 0)),
        out_shape=jax.ShapeDtypeStruct((s, d), F32),
        scratch_shapes=[
            pltpu.VMEM((tm, d), BF16),
            pltpu.VMEM((tm, d), F32),
        ],
        compiler_params=pltpu.CompilerParams(
            dimension_semantics=("arbitrary", "arbitrary"), vmem_limit_bytes=VMEM_LIMIT_BYTES),
        name="ffn_ln",
    )(x1, w1_b, w2_b, ln_g, ln_b)


def kernel(x, positions, w_in, w_pool, pool_scale, w_branch_attn, w_branch_pool, w_out,
           ln_mix_g, ln_mix_b, w_ff1, w_ff2, ln_ff_g, ln_ff_b):
    b, s, d = x.shape
    assert (b, s, d) == (1, SEQ, D_MODEL) and w_in.shape[0] == DEPTH
    half = HEAD_DIM // 2
    inv_freq = ROPE_THETA ** (-jnp.arange(half, dtype=F32) / half)
    invf2 = jnp.concatenate([inv_freq, inv_freq]).reshape(1, HEAD_DIM)
    x2 = x.reshape(s, d)
    pos2 = positions.reshape(s, 1)
    for layer in range(DEPTH):
        h = _proj(x2, pos2, invf2, w_in[layer].astype(BF16))
        o_attn = _attn(h)
        x2 = _mix(o_attn, h, x2, w_pool[layer].astype(BF16), pool_scale[layer].reshape(1, POOL_WIDTH),
                  w_branch_attn[layer].astype(BF16), w_branch_pool[layer].astype(BF16),
                  w_out[layer].astype(BF16), ln_mix_g[layer].reshape(1, d), ln_mix_b[layer].reshape(1, d))
        x2 = _ffn(x2, w_ff1[layer].astype(BF16), w_ff2[layer].astype(BF16),
                  ln_ff_g[layer].reshape(1, d), ln_ff_b[layer].reshape(1, d))
    return x2.reshape(b, s, d)
```

```python
import functools

import jax
import jax.numpy as jnp
from jax import lax
from jax.experimental import pallas as pl
from jax.experimental.pallas import tpu as pltpu

F32 = jnp.float32
BF16 = jnp.bfloat16

D_MODEL = 2048
SEQ = 8192
HEAD_DIM = 128
ATTN_WIDTH = D_MODEL // 2
ATTN_HEADS = ATTN_WIDTH // HEAD_DIM
POOL_WIDTH = D_MODEL // 2
POOL_WINDOWS = (2, 4, 8, 16)
POOL_GROUP_WIDTH = POOL_WIDTH // len(POOL_WINDOWS)
MAX_POOL_WINDOW = max(POOL_WINDOWS)
DILATIONS = (1, 4, 16)
SUB_BLOCK = 128
D_FF = 4 * D_MODEL
IN_WIDTH = 3 * ATTN_WIDTH + POOL_WIDTH + 2 * D_MODEL
ROPE_THETA = 10000.0
LN_EPS = 1e-5
DEPTH = 1
DEEPNORM_ALPHA = (2.0 * DEPTH) ** 0.25
SM_SCALE = HEAD_DIM ** -0.5
LOG2_E = 1.4426950408889634
MASK_VALUE = -1e30

VMEM_LIMIT_BYTES = 56 * 1024 * 1024

ATTN_CHUNK = max(DILATIONS) * SUB_BLOCK
PROJ_TM, PROJ_TN = 1024, 1024
MIX_TM = 256
FFN_TM, FFN_TF = 512, 512


def _layer_norm(y, g, b):
    mu = jnp.mean(y, axis=-1, keepdims=True)
    yc = y - mu
    var = jnp.mean(yc * yc, axis=-1, keepdims=True)
    return yc * lax.rsqrt(var + LN_EPS) * g + b


def _proj_kernel(x_ref, pos_ref, invf_ref, w_ref, o_ref, xb_ref, cos_ref, sin_ref, *, n_rope_blocks):
    j = pl.program_id(1)

    @pl.when(j == 0)
    def _():
        xb_ref[...] = x_ref[...].astype(BF16)
        ang = pos_ref[...].astype(F32) * invf_ref[...]
        lane = lax.broadcasted_iota(jnp.int32, (1, HEAD_DIM), 1)
        sign = jnp.where(lane < HEAD_DIM // 2, -1.0, 1.0).astype(F32)
        cos_ref[...] = jnp.cos(ang)
        sin_ref[...] = jnp.sin(ang) * sign

    acc = jnp.dot(xb_ref[...], w_ref[...], preferred_element_type=F32)

    @pl.when(j < n_rope_blocks)
    def _():
        cos = cos_ref[...]
        sin = sin_ref[...]
        for hh in range(acc.shape[1] // HEAD_DIM):
            t = acc[:, hh * HEAD_DIM:(hh + 1) * HEAD_DIM]
            o_ref[:, hh * HEAD_DIM:(hh + 1) * HEAD_DIM] = t * cos + pltpu.roll(t, HEAD_DIM // 2, 1) * sin

    @pl.when(j >= n_rope_blocks)
    def _():
        o_ref[...] = acc


def _proj(x2, pos2, invf2, w_in_b):
    s, d = x2.shape
    tm, tn = PROJ_TM, PROJ_TN
    n_rope_blocks = (2 * ATTN_WIDTH) // tn
    return pl.pallas_call(
        functools.partial(_proj_kernel, n_rope_blocks=n_rope_blocks),
        grid=(s // tm, IN_WIDTH // tn),
        in_specs=[
            pl.BlockSpec((tm, d), lambda i, j: (i, 0)),
            pl.BlockSpec((tm, 1), lambda i, j: (i, 0)),
            pl.BlockSpec((1, HEAD_DIM), lambda i, j: (0, 0)),
            pl.BlockSpec((d, tn), lambda i, j: (0, j)),
        ],
        out_specs=pl.BlockSpec((tm, tn), lambda i, j: (i, j)),
        out_shape=jax.ShapeDtypeStruct((s, IN_WIDTH), F32),
        scratch_shapes=[
            pltpu.VMEM((tm, d), BF16),
            pltpu.VMEM((tm, HEAD_DIM), F32),
            pltpu.VMEM((tm, HEAD_DIM), F32),
        ],
        compiler_params=pltpu.CompilerParams(
            dimension_semantics=("arbitrary", "arbitrary"), vmem_limit_bytes=VMEM_LIMIT_BYTES),
        name="proj_rope",
    )(x2, pos2, invf2, w_in_b)


def _attn_kernel(q_ref, kc_ref, kp_ref, vc_ref, vp_ref, o_ref, bias_ref,
                 qd_refs, kd_refs, vd_refs, acc_refs, m_refs, l_refs):
    n = pl.program_id(0)
    blk = SUB_BLOCK

    qi = lax.broadcasted_iota(jnp.int32, (blk, 2 * blk), 0)
    kj = lax.broadcasted_iota(jnp.int32, (blk, 2 * blk), 1)
    band = (kj >= qi) & (kj <= qi + blk)
    bias_ref[0] = jnp.where(band, 0.0, MASK_VALUE).astype(F32)
    bias_ref[1] = jnp.where(band & (kj >= blk), 0.0, MASK_VALUE).astype(F32)
    first_chunk = (n == 0).astype(jnp.int32)

    for p, dil in enumerate(DILATIONS):
        m_len = ATTN_CHUNK // dil
        nb = m_len // blk
        krows = blk + m_len
        qd_ref, kd_ref, vd_ref = qd_refs[p], kd_refs[p], vd_refs[p]
        acc_ref, m_ref, l_ref = acc_refs[p], m_refs[p], l_refs[p]

        for r in range(dil):
            qd_ref[pl.ds(r * m_len, m_len), :] = q_ref[pl.ds(r, m_len, stride=dil), :].astype(BF16)
            kd_ref[pl.ds(r * krows, blk), :] = (
                kp_ref[pl.ds((m_len - blk) * dil + r, blk, stride=dil), :].astype(BF16))
            kd_ref[pl.ds(r * krows + blk, m_len), :] = kc_ref[pl.ds(r, m_len, stride=dil), :].astype(BF16)
            vd_ref[pl.ds(r * krows, blk), :] = (
                vp_ref[pl.ds((m_len - blk) * dil + r, blk, stride=dil), :].astype(BF16))
            vd_ref[pl.ds(r * krows + blk, m_len), :] = vc_ref[pl.ds(r, m_len, stride=dil), :].astype(BF16)

        for r in range(dil):
            for b in range(nb):
                q0 = r * m_len + b * blk
                k0 = r * krows + b * blk
                qb = qd_ref[q0:q0 + blk, :]
                kb = kd_ref[k0:k0 + 2 * blk, :]
                vb = vd_ref[k0:k0 + 2 * blk, :]
                s = lax.dot_general(qb, kb, (((1,), (1,)), ((), ())), preferred_element_type=F32)
                s = s + (bias_ref[first_chunk] if b == 0 else bias_ref[0])
                m_raw = jnp.max(s, axis=-1, keepdims=True)
                pr = jnp.exp2((s - m_raw) * (SM_SCALE * LOG2_E))
                l = jnp.sum(pr, axis=-1, keepdims=True)
                acc = jnp.dot(pr.astype(BF16), vb, preferred_element_type=F32)
                rows = pl.ds(b * (blk * dil) + r, blk, stride=dil)
                acc_ref[rows, :] = acc
                m_ref[rows, :] = jnp.broadcast_to(m_raw * SM_SCALE, (blk, HEAD_DIM))
                l_ref[rows, :] = jnp.broadcast_to(l, (blk, HEAD_DIM))

    for c in range(ATTN_CHUNK // blk):
        rows = slice(c * blk, (c + 1) * blk)
        m0, m1, m2 = m_refs[0][rows, :], m_refs[1][rows, :], m_refs[2][rows, :]
        mm = jnp.maximum(jnp.maximum(m0, m1), m2)
        w0, w1, w2 = jnp.exp(m0 - mm), jnp.exp(m1 - mm), jnp.exp(m2 - mm)
        num = w0 * acc_refs[0][rows, :] + w1 * acc_refs[1][rows, :] + w2 * acc_refs[2][rows, :]
        den = w0 * l_refs[0][rows, :] + w1 * l_refs[1][rows, :] + w2 * l_refs[2][rows, :]
        o_ref[rows, :] = (num / den).astype(o_ref.dtype)


def _attn(h):
    s = h.shape[0]
    c = ATTN_CHUNK
    blk = SUB_BLOCK
    npat = len(DILATIONS)
    chunk = lambda col0: pl.BlockSpec((c, HEAD_DIM), lambda n, hh: (n, col0 + hh))
    prev = lambda col0: pl.BlockSpec((c, HEAD_DIM), lambda n, hh: (jnp.maximum(n - 1, 0), col0 + hh))

    def body(q_ref, kc_ref, kp_ref, vc_ref, vp_ref, o_ref, bias, *scr):
        groups = [scr[i * npat:(i + 1) * npat] for i in range(6)]
        _attn_kernel(q_ref, kc_ref, kp_ref, vc_ref, vp_ref, o_ref, bias, *groups)

    kv_scratch = [pltpu.VMEM((dil * blk + c, HEAD_DIM), BF16) for dil in DILATIONS]
    return pl.pallas_call(
        body,
        grid=(s // c, ATTN_HEADS),
        in_specs=[chunk(0), chunk(ATTN_HEADS), prev(ATTN_HEADS), chunk(2 * ATTN_HEADS), prev(2 * ATTN_HEADS)],
        out_specs=pl.BlockSpec((c, HEAD_DIM), lambda n, hh: (n, hh)),
        out_shape=jax.ShapeDtypeStruct((s, ATTN_WIDTH), BF16),
        scratch_shapes=(
            [pltpu.VMEM((2, blk, 2 * blk), F32)]
            + [pltpu.VMEM((c, HEAD_DIM), BF16)] * npat
            + kv_scratch + kv_scratch
            + [pltpu.VMEM((c, HEAD_DIM), F32)] * (3 * npat)),
        compiler_params=pltpu.CompilerParams(
            dimension_semantics=("arbitrary", "arbitrary"), vmem_limit_bytes=VMEM_LIMIT_BYTES),
        name="dilated_attn",
    )(h, h, h, h, h)


def _mix_kernel(oat_ref, u_ref, uh_ref, ga_ref, gp_ref, x_ref, wpool_ref, pscale_ref,
                wba_ref, wbb_ref, wout_ref, g_ref, b_ref, out_ref, ubuf_ref, pm_ref):
    i = pl.program_id(0)
    tm = u_ref.shape[0]
    halo = MAX_POOL_WINDOW

    @pl.when(i == 0)
    def _():
        ubuf_ref[0:halo, :] = jnp.zeros((halo, POOL_WIDTH), F32)

    @pl.when(i > 0)
    def _():
        ubuf_ref[0:halo, :] = uh_ref[...]

    ubuf_ref[halo:halo + tm, :] = u_ref[...]

    t_glob = i * tm + lax.broadcasted_iota(jnp.int32, (tm, 1), 0)
    for g, w in enumerate(POOL_WINDOWS):
        cols = slice(g * POOL_GROUP_WIDTH, (g + 1) * POOL_GROUP_WIDTH)
        ug = u_ref[:, cols]
        wsum = ug
        for jj in range(1, w):
            wsum = wsum + ubuf_ref[halo - jj:halo - jj + tm, cols]
        count = jnp.minimum(t_glob + 1, w).astype(F32)
        pooled = wsum / count - ug
        y = jnp.dot(pooled.astype(BF16), wpool_ref[g], preferred_element_type=F32)
        pm_ref[:, cols] = (y * pscale_ref[:, cols]).astype(BF16)

    y_attn = jnp.dot(oat_ref[...], wba_ref[...], preferred_element_type=F32)
    y_pool = jnp.dot(pm_ref[...], wbb_ref[...], preferred_element_type=F32)
    merged = jax.nn.sigmoid(ga_ref[...]) * y_attn + jax.nn.sigmoid(gp_ref[...]) * y_pool
    mix = jnp.dot(merged.astype(BF16), wout_ref[...], preferred_element_type=F32)
    out_ref[...] = _layer_norm(DEEPNORM_ALPHA * x_ref[...] + mix, g_ref[...], b_ref[...])


def _mix(o_attn, h, x2, w_pool_b, pool_scale, w_ba_b, w_bb_b, w_out_b, ln_g, ln_b):
    s, d = x2.shape
    tm = MIX_TM
    halo = MAX_POOL_WINDOW
    u_col = (3 * ATTN_WIDTH) // POOL_WIDTH
    ga_col = (3 * ATTN_WIDTH + POOL_WIDTH) // d
    resident = lambda shape: pl.BlockSpec(shape, lambda i: (0,) * len(shape), pipeline_mode=pl.Buffered(1))
    return pl.pallas_call(
        _mix_kernel,
        grid=(s // tm,),
        in_specs=[
            pl.BlockSpec((tm, ATTN_WIDTH), lambda i: (i, 0)),
            pl.BlockSpec((tm, POOL_WIDTH), lambda i: (i, u_col)),
            pl.BlockSpec((halo, POOL_WIDTH), lambda i: (jnp.maximum(i * (tm // halo) - 1, 0), u_col)),
            pl.BlockSpec((tm, d), lambda i: (i, ga_col)),
            pl.BlockSpec((tm, d), lambda i: (i, ga_col + 1)),
            pl.BlockSpec((tm, d), lambda i: (i, 0)),
            resident(w_pool_b.shape),
            resident((1, POOL_WIDTH)),
            resident(w_ba_b.shape),
            resident(w_bb_b.shape),
            resident(w_out_b.shape),
            resident((1, d)),
            resident((1, d)),
        ],
        out_specs=pl.BlockSpec((tm, d), lambda i: (i, 0)),
        out_shape=jax.ShapeDtypeStruct((s, d), F32),
        scratch_shapes=[
            pltpu.VMEM((halo + tm, POOL_WIDTH), F32),
            pltpu.VMEM((tm, POOL_WIDTH), BF16),
        ],
        compiler_params=pltpu.CompilerParams(
            dimension_semantics=("arbitrary",), vmem_limit_bytes=VMEM_LIMIT_BYTES),
        name="mix_ln",
    )(o_attn, h, h, h, h, x2, w_pool_b, pool_scale, w_ba_b, w_bb_b, w_out_b, ln_g, ln_b)


def _ffn_kernel(x_ref, w1_ref, w2_ref, g_ref, b_ref, out_ref, xb_ref, acc_ref):
    j = pl.program_id(1)

    @pl.when(j == 0)
    def _():
        xb_ref[...] = x_ref[...].astype(BF16)

    hid = jnp.dot(xb_ref[...], w1_ref[...], preferred_element_type=F32)
    hid = jnp.square(jnp.maximum(hid, 0.0)).astype(BF16)
    part = jnp.dot(hid, w2_ref[...], preferred_element_type=F32)

    @pl.when(j == 0)
    def _():
        acc_ref[...] = part

    @pl.when(j > 0)
    def _():
        acc_ref[...] += part

    @pl.when(j == pl.num_programs(1) - 1)
    def _():
        out_ref[...] = _layer_norm(DEEPNORM_ALPHA * x_ref[...] + acc_ref[...], g_ref[...], b_ref[...])


def _ffn(x1, w1_b, w2_b, ln_g, ln_b):
    s, d = x1.shape
    tm, tf = FFN_TM, FFN_TF
    return pl.pallas_call(
        _ffn_kernel,
        grid=(s // tm, D_FF // tf),
        in_specs=[
            pl.BlockSpec((tm, d), lambda i, j: (i, 0)),
            pl.BlockSpec((d, tf), lambda i, j: (0, j)),
            pl.BlockSpec((tf, d), lambda i, j: (j, 0)),
            pl.BlockSpec((1, d), lambda i, j: (0, 0)),
            pl.BlockSpec((1, d), lambda i, j: (0, 0)),
        ],
        out_specs=pl.BlockSpec((tm, d), lambda i, j: (i, 0)),
        out_shape=jax.ShapeDtypeStruct((s, d), F32),
        scratch_shapes=[
            pltpu.VMEM((tm, d), BF16),
            pltpu.VMEM((tm, d), F32),
        ],
        compiler_params=pltpu.CompilerParams(
            dimension_semantics=("arbitrary", "arbitrary"), vmem_limit_bytes=VMEM_LIMIT_BYTES),
        name="ffn_ln",
    )(x1, w1_b, w2_b, ln_g, ln_b)


def kernel(x, positions, w_in, w_pool, pool_scale, w_branch_attn, w_branch_pool, w_out,
           ln_mix_g, ln_mix_b, w_ff1, w_ff2, ln_ff_g, ln_ff_b):
    b, s, d = x.shape
    assert (b, s, d) == (1, SEQ, D_MODEL) and w_in.shape[0] == DEPTH
    half = HEAD_DIM // 2
    inv_freq = ROPE_THETA ** (-jnp.arange(half, dtype=F32) / half)
    invf2 = jnp.concatenate([inv_freq, inv_freq]).reshape(1, HEAD_DIM)
    x2 = x.reshape(s, d)
    pos2 = positions.reshape(s, 1)
    for layer in range(DEPTH):
        h = _proj(x2, pos2, invf2, w_in[layer].astype(BF16))
        o_attn = _attn(h)
        x2 = _mix(o_attn, h, x2, w_pool[layer].astype(BF16), pool_scale[layer].reshape(1, POOL_WIDTH),
                  w_branch_attn[layer].astype(BF16), w_branch_pool[layer].astype(BF16),
                  w_out[layer].astype(BF16), ln_mix_g[layer].reshape(1, d), ln_mix_b[layer].reshape(1, d))
        x2 = _ffn(x2, w_ff1[layer].astype(BF16), w_ff2[layer].astype(BF16),
                  ln_ff_g[layer].reshape(1, d), ln_ff_b[layer].reshape(1, d))
    return x2.reshape(b, s, d)
```

```python
import functools

import jax
import jax.numpy as jnp
from jax import lax
from jax.experimental import pallas as pl
from jax.experimental.pallas import tpu as pltpu

F32 = jnp.float32
BF16 = jnp.bfloat16

D_MODEL = 2048
SEQ = 8192
HEAD_DIM = 128
ATTN_WIDTH = D_MODEL // 2
ATTN_HEADS = ATTN_WIDTH // HEAD_DIM
POOL_WIDTH = D_MODEL // 2
POOL_WINDOWS = (2, 4, 8, 16)
POOL_GROUP_WIDTH = POOL_WIDTH // len(POOL_WINDOWS)
MAX_POOL_WINDOW = max(POOL_WINDOWS)
DILATIONS = (1, 4, 16)
SUB_BLOCK = 128
D_FF = 4 * D_MODEL
IN_WIDTH = 3 * ATTN_WIDTH + POOL_WIDTH + 2 * D_MODEL
ROPE_THETA = 10000.0
LN_EPS = 1e-5
DEPTH = 1
DEEPNORM_ALPHA = (2.0 * DEPTH) ** 0.25
SM_SCALE = HEAD_DIM ** -0.5
LOG2_E = 1.4426950408889634
MASK_VALUE = -1e30

VMEM_LIMIT_BYTES = 56 * 1024 * 1024

ATTN_CHUNK = max(DILATIONS) * SUB_BLOCK
PROJ_TM, PROJ_TN = 1024, 1024
MIX_TM = 256
FFN_TM, FFN_TF = 512, 512


def _layer_norm(y, g, b):
    mu = jnp.mean(y, axis=-1, keepdims=True)
    yc = y - mu
    var = jnp.mean(yc * yc, axis=-1, keepdims=True)
    return yc * lax.rsqrt(var + LN_EPS) * g + b


def _proj_kernel(x_ref, pos_ref, invf_ref, w_ref, *rest, n_rope_blocks, n_cast):
    cast_in, o_ref, cast_out = rest[:n_cast], rest[n_cast], rest[n_cast + 1:2 * n_cast + 1]
    xb_ref, cos_ref, sin_ref = rest[2 * n_cast + 1:]
    j = pl.program_id(1)

    @pl.when(j == 0)
    def _():
        xb_ref[...] = x_ref[...].astype(BF16)
        ang = pos_ref[...].astype(F32) * invf_ref[...]
        lane = lax.broadcasted_iota(jnp.int32, (1, HEAD_DIM), 1)
        sign = jnp.where(lane < HEAD_DIM // 2, -1.0, 1.0).astype(F32)
        cos_ref[...] = jnp.cos(ang)
        sin_ref[...] = jnp.sin(ang) * sign

    for src, dst in zip(cast_in, cast_out):
        dst[...] = src[...].astype(BF16)

    o_ref[...] = jnp.dot(xb_ref[...], w_ref[...], preferred_element_type=F32)

    @pl.when(j < n_rope_blocks)
    def _():
        cos = cos_ref[...]
        sin = sin_ref[...]
        for hh in range(o_ref.shape[1] // HEAD_DIM):
            cols = slice(hh * HEAD_DIM, (hh + 1) * HEAD_DIM)
            t = o_ref[:, cols]
            o_ref[:, cols] = t * cos + pltpu.roll(t, HEAD_DIM // 2, 1) * sin


def _slab_specs(weights, n_steps, step_index):
    specs, shapes = [], []
    for w in weights:
        rows, cols = w.shape
        specs.append(pl.BlockSpec((rows // n_steps, cols), lambda *g: (step_index(*g), 0)))
        shapes.append(jax.ShapeDtypeStruct(w.shape, BF16))
    return specs, shapes


def _proj(x2, pos2, invf2, w_in_b, cast_weights):
    s, d = x2.shape
    tm, tn = PROJ_TM, PROJ_TN
    n_i, n_j = s // tm, IN_WIDTH // tn
    n_rope_blocks = (2 * ATTN_WIDTH) // tn
    cast_specs, cast_shapes = _slab_specs(cast_weights, n_i * n_j, lambda i, j: i * n_j + j)
    outs = pl.pallas_call(
        functools.partial(_proj_kernel, n_rope_blocks=n_rope_blocks, n_cast=len(cast_weights)),
        grid=(n_i, n_j),
        in_specs=[
            pl.BlockSpec((tm, d), lambda i, j: (i, 0)),
            pl.BlockSpec((tm, 1), lambda i, j: (i, 0)),
            pl.BlockSpec((1, HEAD_DIM), lambda i, j: (0, 0)),
            pl.BlockSpec((d, tn), lambda i, j: (0, j)),
        ] + cast_specs,
        out_specs=[pl.BlockSpec((tm, tn), lambda i, j: (i, j))] + cast_specs,
        out_shape=[jax.ShapeDtypeStruct((s, IN_WIDTH), F32)] + cast_shapes,
        scratch_shapes=[
            pltpu.VMEM((tm, d), BF16),
            pltpu.VMEM((tm, HEAD_DIM), F32),
            pltpu.VMEM((tm, HEAD_DIM), F32),
        ],
        compiler_params=pltpu.CompilerParams(
            dimension_semantics=("arbitrary", "arbitrary"), vmem_limit_bytes=VMEM_LIMIT_BYTES),
        name="proj_rope",
    )(x2, pos2, invf2, w_in_b, *cast_weights)
    return outs[0], outs[1:]


def _attn_kernel(q_ref, kc_ref, kp_ref, vc_ref, vp_ref, o_ref, bias_ref,
                 qd_refs, kd_refs, vd_refs, acc_refs, m_refs, l_refs):
    n = pl.program_id(0)
    blk = SUB_BLOCK

    qi = lax.broadcasted_iota(jnp.int32, (blk, 2 * blk), 0)
    kj = lax.broadcasted_iota(jnp.int32, (blk, 2 * blk), 1)
    band = (kj >= qi) & (kj <= qi + blk)
    bias_ref[0] = jnp.where(band, 0.0, MASK_VALUE).astype(F32)
    bias_ref[1] = jnp.where(band & (kj >= blk), 0.0, MASK_VALUE).astype(F32)
    first_chunk = (n == 0).astype(jnp.int32)

    for p, dil in enumerate(DILATIONS):
        m_len = ATTN_CHUNK // dil
        nb = m_len // blk
        krows = blk + m_len
        qd_ref, kd_ref, vd_ref = qd_refs[p], kd_refs[p], vd_refs[p]
        acc_ref, m_ref, l_ref = acc_refs[p], m_refs[p], l_refs[p]

        for r in range(dil):
            qd_ref[pl.ds(r * m_len, m_len), :] = q_ref[pl.ds(r, m_len, stride=dil), :].astype(BF16)
            kd_ref[pl.ds(r * krows, blk), :] = (
                kp_ref[pl.ds((m_len - blk) * dil + r, blk, stride=dil), :].astype(BF16))
            kd_ref[pl.ds(r * krows + blk, m_len), :] = kc_ref[pl.ds(r, m_len, stride=dil), :].astype(BF16)
            vd_ref[pl.ds(r * krows, blk), :] = (
                vp_ref[pl.ds((m_len - blk) * dil + r, blk, stride=dil), :].astype(BF16))
            vd_ref[pl.ds(r * krows + blk, m_len), :] = vc_ref[pl.ds(r, m_len, stride=dil), :].astype(BF16)

        for r in range(dil):
            for b in range(nb):
                q0 = r * m_len + b * blk
                k0 = r * krows + b * blk
                qb = qd_ref[q0:q0 + blk, :]
                kb = kd_ref[k0:k0 + 2 * blk, :]
                vb = vd_ref[k0:k0 + 2 * blk, :]
                s = lax.dot_general(qb, kb, (((1,), (1,)), ((), ())), preferred_element_type=F32)
                s = s + (bias_ref[first_chunk] if b == 0 else bias_ref[0])
                m_raw = jnp.max(s, axis=-1, keepdims=True)
                pr = jnp.exp2((s - m_raw) * (SM_SCALE * LOG2_E))
                l = jnp.sum(pr, axis=-1, keepdims=True)
                acc = jnp.dot(pr.astype(BF16), vb, preferred_element_type=F32)
                rows = pl.ds(b * (blk * dil) + r, blk, stride=dil)
                acc_ref[rows, :] = acc
                m_ref[rows, :] = jnp.broadcast_to(m_raw * SM_SCALE, (blk, HEAD_DIM))
                l_ref[rows, :] = jnp.broadcast_to(l, (blk, HEAD_DIM))

    for c in range(ATTN_CHUNK // blk):
        rows = slice(c * blk, (c + 1) * blk)
        m0, m1, m2 = m_refs[0][rows, :], m_refs[1][rows, :], m_refs[2][rows, :]
        mm = jnp.maximum(jnp.maximum(m0, m1), m2)
        w0, w1, w2 = jnp.exp(m0 - mm), jnp.exp(m1 - mm), jnp.exp(m2 - mm)
        num = w0 * acc_refs[0][rows, :] + w1 * acc_refs[1][rows, :] + w2 * acc_refs[2][rows, :]
        den = w0 * l_refs[0][rows, :] + w1 * l_refs[1][rows, :] + w2 * l_refs[2][rows, :]
        o_ref[rows, :] = (num / den).astype(o_ref.dtype)


def _attn(h):
    s = h.shape[0]
    c = ATTN_CHUNK
    blk = SUB_BLOCK
    npat = len(DILATIONS)
    chunk = lambda col0: pl.BlockSpec((c, HEAD_DIM), lambda n, hh: (n, col0 + hh))
    prev = lambda col0: pl.BlockSpec((c, HEAD_DIM), lambda n, hh: (jnp.maximum(n - 1, 0), col0 + hh))

    def body(q_ref, kc_ref, kp_ref, vc_ref, vp_ref, o_ref, bias, *scr):
        groups = [scr[i * npat:(i + 1) * npat] for i in range(6)]
        _attn_kernel(q_ref, kc_ref, kp_ref, vc_ref, vp_ref, o_ref, bias, *groups)

    kv_scratch = [pltpu.VMEM((dil * blk + c, HEAD_DIM), BF16) for dil in DILATIONS]
    return pl.pallas_call(
        body,
        grid=(s // c, ATTN_HEADS),
        in_specs=[chunk(0), chunk(ATTN_HEADS), prev(ATTN_HEADS), chunk(2 * ATTN_HEADS), prev(2 * ATTN_HEADS)],
        out_specs=pl.BlockSpec((c, HEAD_DIM), lambda n, hh: (n, hh)),
        out_shape=jax.ShapeDtypeStruct((s, ATTN_WIDTH), BF16),
        scratch_shapes=(
            [pltpu.VMEM((2, blk, 2 * blk), F32)]
            + [pltpu.VMEM((c, HEAD_DIM), BF16)] * npat
            + kv_scratch + kv_scratch
            + [pltpu.VMEM((c, HEAD_DIM), F32)] * (3 * npat)),
        compiler_params=pltpu.CompilerParams(
            dimension_semantics=("arbitrary", "arbitrary"), vmem_limit_bytes=VMEM_LIMIT_BYTES),
        name="dilated_attn",
    )(h, h, h, h, h)


def _mix_kernel(oat_ref, u_ref, uh_ref, ga_ref, gp_ref, x_ref, wpool_ref, pscale_ref,
                wba_ref, wbb_ref, wout_ref, g_ref, b_ref, w1_ref, w2_ref,
                out_ref, w1b_ref, w2b_ref, ubuf_ref, pm_ref):
    i = pl.program_id(0)
    tm = u_ref.shape[0]
    halo = MAX_POOL_WINDOW

    @pl.when(i == 0)
    def _():
        ubuf_ref[0:halo, :] = jnp.zeros((halo, POOL_WIDTH), F32)

    @pl.when(i > 0)
    def _():
        ubuf_ref[0:halo, :] = uh_ref[...]

    ubuf_ref[halo:halo + tm, :] = u_ref[...]

    t_glob = i * tm + lax.broadcasted_iota(jnp.int32, (tm, 1), 0)
    for g, w in enumerate(POOL_WINDOWS):
        cols = slice(g * POOL_GROUP_WIDTH, (g + 1) * POOL_GROUP_WIDTH)
        ug = u_ref[:, cols]
        wsum = ug
        for jj in range(1, w):
            wsum = wsum + ubuf_ref[halo - jj:halo - jj + tm, cols]
        count = jnp.minimum(t_glob + 1, w).astype(F32)
        pooled = wsum / count - ug
        y = jnp.dot(pooled.astype(BF16), wpool_ref[g], preferred_element_type=F32)
        pm_ref[:, cols] = (y * pscale_ref[:, cols]).astype(BF16)

    y_attn = jnp.dot(oat_ref[...], wba_ref[...], preferred_element_type=F32)
    y_pool = jnp.dot(pm_ref[...], wbb_ref[...], preferred_element_type=F32)
    merged = jax.nn.sigmoid(ga_ref[...]) * y_attn + jax.nn.sigmoid(gp_ref[...]) * y_pool
    mix = jnp.dot(merged.astype(BF16), wout_ref[...], preferred_element_type=F32)
    out_ref[...] = _layer_norm(DEEPNORM_ALPHA * x_ref[...] + mix, g_ref[...], b_ref[...])

    w1b_ref[...] = w1_ref[...].astype(BF16)
    w2b_ref[...] = w2_ref[...].astype(BF16)


def _mix(o_attn, h, x2, w_pool_b, pool_scale, w_ba_b, w_bb_b, w_out_b, ln_g, ln_b, w_ff1, w_ff2):
    s, d = x2.shape
    tm = MIX_TM
    n_i = s // tm
    halo = MAX_POOL_WINDOW
    u_col = (3 * ATTN_WIDTH) // POOL_WIDTH
    ga_col = (3 * ATTN_WIDTH + POOL_WIDTH) // d
    resident = lambda shape: pl.BlockSpec(shape, lambda i: (0,) * len(shape), pipeline_mode=pl.Buffered(1))
    w1_spec = pl.BlockSpec((d, D_FF // n_i), lambda i: (0, i))
    w2_spec = pl.BlockSpec((D_FF // n_i, d), lambda i: (i, 0))
    return pl.pallas_call(
        _mix_kernel,
        grid=(n_i,),
        in_specs=[
            pl.BlockSpec((tm, ATTN_WIDTH), lambda i: (i, 0)),
            pl.BlockSpec((tm, POOL_WIDTH), lambda i: (i, u_col)),
            pl.BlockSpec((halo, POOL_WIDTH), lambda i: (jnp.maximum(i * (tm // halo) - 1, 0), u_col)),
            pl.BlockSpec((tm, d), lambda i: (i, ga_col)),
            pl.BlockSpec((tm, d), lambda i: (i, ga_col + 1)),
            pl.BlockSpec((tm, d), lambda i: (i, 0)),
            resident(w_pool_b.shape),
            resident((1, POOL_WIDTH)),
            resident(w_ba_b.shape),
            resident(w_bb_b.shape),
            resident(w_out_b.shape),
            resident((1, d)),
            resident((1, d)),
            w1_spec,
            w2_spec,
        ],
        out_specs=[pl.BlockSpec((tm, d), lambda i: (i, 0)), w1_spec, w2_spec],
        out_shape=[jax.ShapeDtypeStruct((s, d), F32),
                   jax.ShapeDtypeStruct(w_ff1.shape, BF16), jax.ShapeDtypeStruct(w_ff2.shape, BF16)],
        scratch_shapes=[
            pltpu.VMEM((halo + tm, POOL_WIDTH), F32),
            pltpu.VMEM((tm, POOL_WIDTH), BF16),
        ],
        compiler_params=pltpu.CompilerParams(
            dimension_semantics=("arbitrary",), vmem_limit_bytes=VMEM_LIMIT_BYTES),
        name="mix_ln",
    )(o_attn, h, h, h, h, x2, w_pool_b, pool_scale, w_ba_b, w_bb_b, w_out_b, ln_g, ln_b, w_ff1, w_ff2)


def _ffn_kernel(x_ref, w1_ref, w2_ref, g_ref, b_ref, out_ref, xb_ref, acc_ref):
    j = pl.program_id(1)

    @pl.when(j == 0)
    def _():
        xb_ref[...] = x_ref[...].astype(BF16)
        acc_ref[...] = jnp.zeros(acc_ref.shape, F32)

    hid = jnp.dot(xb_ref[...], w1_ref[...], preferred_element_type=F32)
    hid = jnp.square(jnp.maximum(hid, 0.0)).astype(BF16)
    acc_ref[...] += jnp.dot(hid, w2_ref[...], preferred_element_type=F32)

    @pl.when(j == pl.num_programs(1) - 1)
    def _():
        out_ref[...] = _layer_norm(DEEPNORM_ALPHA * x_ref[...] + acc_ref[...], g_ref[...], b_ref[...])


def _ffn(x1, w1_b, w2_b, ln_g, ln_b):
    s, d = x1.shape
    tm, tf = FFN_TM, FFN_TF
    return pl.pallas_call(
        _ffn_kernel,
        grid=(s // tm, D_FF // tf),
        in_specs=[
            pl.BlockSpec((tm, d), lambda i, j: (i, 0)),
            pl.BlockSpec((d, tf), lambda i, j: (0, j)),
            pl.BlockSpec((tf, d), lambda i, j: (j, 0)),
            pl.BlockSpec((1, d), lambda i, j: (0, 0)),
            pl.BlockSpec((1, d), lambda i, j: (0, 0)),
        ],
        out_specs=pl.BlockSpec((tm, d), lambda i, j: (i, 0)),
        out_shape=jax.ShapeDtypeStruct((s, d), F32),
        scratch_shapes=[
            pltpu.VMEM((tm, d), BF16),
            pltpu.VMEM((tm, d), F32),
        ],
        compiler_params=pltpu.CompilerParams(
            dimension_semantics=("arbitrary", "arbitrary"), vmem_limit_bytes=VMEM_LIMIT_BYTES),
        name="ffn_ln",
    )(x1, w1_b, w2_b, ln_g, ln_b)


def kernel(x, positions, w_in, w_pool, pool_scale, w_branch_attn, w_branch_pool, w_out,
           ln_mix_g, ln_mix_b, w_ff1, w_ff2, ln_ff_g, ln_ff_b):
    b, s, d = x.shape
    assert (b, s, d) == (1, SEQ, D_MODEL) and w_in.shape[0] == DEPTH
    half = HEAD_DIM // 2
    inv_freq = ROPE_THETA ** (-jnp.arange(half, dtype=F32) / half)
    invf2 = jnp.concatenate([inv_freq, inv_freq]).reshape(1, HEAD_DIM)
    x2 = x.reshape(s, d)
    pos2 = positions.reshape(s, 1)
    for layer in range(DEPTH):
        w_pool2 = w_pool[layer].reshape(POOL_WIDTH, POOL_GROUP_WIDTH)
        h, (w_ba_b, w_bb_b, w_out_b, w_pool_b) = _proj(
            x2, pos2, invf2, w_in[layer].astype(BF16),
            [w_branch_attn[layer], w_branch_pool[layer], w_out[layer], w_pool2])
        o_attn = _attn(h)
        x2, w1_b, w2_b = _mix(
            o_attn, h, x2, w_pool_b.reshape(w_pool[layer].shape), pool_scale[layer].reshape(1, POOL_WIDTH),
            w_ba_b, w_bb_b, w_out_b, ln_mix_g[layer].reshape(1, d), ln_mix_b[layer].reshape(1, d),
            w_ff1[layer], w_ff2[layer])
        x2 = _ffn(x2, w1_b, w2_b, ln_ff_g[layer].reshape(1, d), ln_ff_b[layer].reshape(1, d))
    return x2.reshape(b, s, d)
```

```python
import functools

import jax
import jax.numpy as jnp
from jax import lax
from jax.experimental import pallas as pl
from jax.experimental.pallas import tpu as pltpu

F32 = jnp.float32
BF16 = jnp.bfloat16

D_MODEL = 2048
SEQ = 8192
HEAD_DIM = 128
ATTN_WIDTH = D_MODEL // 2
ATTN_HEADS = ATTN_WIDTH // HEAD_DIM
POOL_WIDTH = D_MODEL // 2
POOL_WINDOWS = (2, 4, 8, 16)
POOL_GROUP_WIDTH = POOL_WIDTH // len(POOL_WINDOWS)
MAX_POOL_WINDOW = max(POOL_WINDOWS)
DILATIONS = (1, 4, 16)
SUB_BLOCK = 128
D_FF = 4 * D_MODEL
IN_WIDTH = 3 * ATTN_WIDTH + POOL_WIDTH + 2 * D_MODEL
ROPE_THETA = 10000.0
LN_EPS = 1e-5
DEPTH = 1
DEEPNORM_ALPHA = (2.0 * DEPTH) ** 0.25
SM_SCALE = HEAD_DIM ** -0.5
LOG2_E = 1.4426950408889634
MASK_VALUE = -1e30

VMEM_LIMIT_BYTES = 56 * 1024 * 1024

ATTN_CHUNK = max(DILATIONS) * SUB_BLOCK
PROJ_TM, PROJ_TN = 1024, 1024
MIX_TM = 256
FFN_TM, FFN_TF = 1024, 512


def _layer_norm(y, g, b):
    mu = jnp.mean(y, axis=-1, keepdims=True)
    yc = y - mu
    var = jnp.mean(yc * yc, axis=-1, keepdims=True)
    return yc * lax.rsqrt(var + LN_EPS) * g + b


def _proj_kernel(x_ref, pos_ref, invf_ref, w_ref, *rest, n_rope_blocks, n_cast):
    cast_in, o_ref, cast_out = rest[:n_cast], rest[n_cast], rest[n_cast + 1:2 * n_cast + 1]
    xb_ref, cos_ref, sin_ref = rest[2 * n_cast + 1:]
    j = pl.program_id(1)

    @pl.when(j == 0)
    def _():
        xb_ref[...] = x_ref[...].astype(BF16)
        ang = pos_ref[...].astype(F32) * invf_ref[...]
        lane = lax.broadcasted_iota(jnp.int32, (1, HEAD_DIM), 1)
        sign = jnp.where(lane < HEAD_DIM // 2, -1.0, 1.0).astype(F32)
        cos_ref[...] = jnp.cos(ang)
        sin_ref[...] = jnp.sin(ang) * sign

    for src, dst in zip(cast_in, cast_out):
        dst[...] = src[...].astype(BF16)

    o_ref[...] = jnp.dot(xb_ref[...], w_ref[...], preferred_element_type=F32)

    @pl.when(j < n_rope_blocks)
    def _():
        cos = cos_ref[...]
        sin = sin_ref[...]
        for hh in range(o_ref.shape[1] // HEAD_DIM):
            cols = slice(hh * HEAD_DIM, (hh + 1) * HEAD_DIM)
            t = o_ref[:, cols]
            o_ref[:, cols] = t * cos + pltpu.roll(t, HEAD_DIM // 2, 1) * sin


def _slab_specs(weights, n_steps, step_index):
    specs, shapes = [], []
    for w in weights:
        rows, cols = w.shape
        specs.append(pl.BlockSpec((rows // n_steps, cols), lambda *g: (step_index(*g), 0)))
        shapes.append(jax.ShapeDtypeStruct(w.shape, BF16))
    return specs, shapes


def _proj(x2, pos2, invf2, w_in_b, cast_weights):
    s, d = x2.shape
    tm, tn = PROJ_TM, PROJ_TN
    n_i, n_j = s // tm, IN_WIDTH // tn
    n_rope_blocks = (2 * ATTN_WIDTH) // tn
    cast_specs, cast_shapes = _slab_specs(cast_weights, n_i * n_j, lambda i, j: i * n_j + j)
    outs = pl.pallas_call(
        functools.partial(_proj_kernel, n_rope_blocks=n_rope_blocks, n_cast=len(cast_weights)),
        grid=(n_i, n_j),
        in_specs=[
            pl.BlockSpec((tm, d), lambda i, j: (i, 0)),
            pl.BlockSpec((tm, 1), lambda i, j: (i, 0)),
            pl.BlockSpec((1, HEAD_DIM), lambda i, j: (0, 0)),
            pl.BlockSpec((d, tn), lambda i, j: (0, j)),
        ] + cast_specs,
        out_specs=[pl.BlockSpec((tm, tn), lambda i, j: (i, j))] + cast_specs,
        out_shape=[jax.ShapeDtypeStruct((s, IN_WIDTH), F32)] + cast_shapes,
        scratch_shapes=[
            pltpu.VMEM((tm, d), BF16),
            pltpu.VMEM((tm, HEAD_DIM), F32),
            pltpu.VMEM((tm, HEAD_DIM), F32),
        ],
        compiler_params=pltpu.CompilerParams(
            dimension_semantics=("arbitrary", "arbitrary"), vmem_limit_bytes=VMEM_LIMIT_BYTES),
        name="proj_rope",
    )(x2, pos2, invf2, w_in_b, *cast_weights)
    return outs[0], outs[1:]


def _attn_kernel(q_ref, kc_ref, kp_ref, vc_ref, vp_ref, o_ref, bias_ref,
                 qd_refs, kd_refs, vd_refs, acc_refs, m_refs, l_refs, stage_refs):
    n = pl.program_id(0)
    blk = SUB_BLOCK
    names = ("q", "kc", "kp", "vc", "vp")
    src = dict(zip(names, (q_ref, kc_ref, kp_ref, vc_ref, vp_ref)))
    src_dil = 1
    stage = dict(zip(names, stage_refs))

    qi = lax.broadcasted_iota(jnp.int32, (blk, 2 * blk), 0)
    kj = lax.broadcasted_iota(jnp.int32, (blk, 2 * blk), 1)
    band = (kj >= qi) & (kj <= qi + blk)
    bias_ref[0] = jnp.where(band, 0.0, MASK_VALUE).astype(F32)
    bias_ref[1] = jnp.where(band & (kj >= blk), 0.0, MASK_VALUE).astype(F32)
    first_chunk = (n == 0).astype(jnp.int32)

    for p, dil in enumerate(DILATIONS):
        m_len = ATTN_CHUNK // dil
        nb = m_len // blk
        krows = blk + m_len
        qd_ref, kd_ref, vd_ref = qd_refs[p], kd_refs[p], vd_refs[p]
        acc_ref, m_ref, l_ref = acc_refs[p], m_refs[p], l_refs[p]

        f = dil // src_dil
        src_len = ATTN_CHUNK // src_dil
        keep_f32 = f > 1 and p + 1 < len(DILATIONS)

        def seg(name, r, m0, rows, f=f, src=src, src_dil=src_dil, src_len=src_len, m_len=m_len):
            if f == 1:
                return src[name][pl.ds(r * m_len + m0, rows), :]
            row0 = (r % src_dil) * src_len + r // src_dil + f * m0
            return src[name][pl.ds(row0, rows, stride=f), :]

        for r in range(dil):
            for name, dst, dst_rows in (("q", qd_ref, m_len), ("k", kd_ref, krows), ("v", vd_ref, krows)):
                if name == "q":
                    cur = seg("q", r, 0, m_len)
                    if keep_f32:
                        stage["q"][r * m_len:(r + 1) * m_len, :] = cur
                    dst[r * m_len:(r + 1) * m_len, :] = cur.astype(BF16)
                    continue
                if keep_f32:
                    prev_full = seg(name + "p", r, 0, m_len)
                    stage[name + "p"][r * m_len:(r + 1) * m_len, :] = prev_full
                    prev_tail = prev_full[m_len - blk:, :]
                else:
                    prev_tail = seg(name + "p", r, m_len - blk, blk)
                cur = seg(name + "c", r, 0, m_len)
                if keep_f32:
                    stage[name + "c"][r * m_len:(r + 1) * m_len, :] = cur
                dst[r * dst_rows:r * dst_rows + blk, :] = prev_tail.astype(BF16)
                dst[r * dst_rows + blk:(r + 1) * dst_rows, :] = cur.astype(BF16)
        if keep_f32:
            src, src_dil = stage, dil

        for r in range(dil):
            for b in range(nb):
                q0 = r * m_len + b * blk
                k0 = r * krows + b * blk
                qb = qd_ref[q0:q0 + blk, :]
                kb = kd_ref[k0:k0 + 2 * blk, :]
                vb = vd_ref[k0:k0 + 2 * blk, :]
                s = lax.dot_general(qb, kb, (((1,), (1,)), ((), ())), preferred_element_type=F32)
                s = s + (bias_ref[first_chunk] if b == 0 else bias_ref[0])
                m_raw = jnp.max(s, axis=-1, keepdims=True)
                pr = jnp.exp2((s - m_raw) * (SM_SCALE * LOG2_E))
                l = jnp.sum(pr, axis=-1, keepdims=True)
                acc = jnp.dot(pr.astype(BF16), vb, preferred_element_type=F32)
                rows = pl.ds(b * (blk * dil) + r, blk, stride=dil)
                acc_ref[rows, :] = acc
                m_ref[rows, :] = jnp.broadcast_to(m_raw * SM_SCALE, (blk, HEAD_DIM))
                l_ref[rows, :] = jnp.broadcast_to(l, (blk, HEAD_DIM))

    for c in range(ATTN_CHUNK // blk):
        rows = slice(c * blk, (c + 1) * blk)
        m0, m1, m2 = m_refs[0][rows, :], m_refs[1][rows, :], m_refs[2][rows, :]
        mm = jnp.maximum(jnp.maximum(m0, m1), m2)
        w0, w1, w2 = jnp.exp(m0 - mm), jnp.exp(m1 - mm), jnp.exp(m2 - mm)
        num = w0 * acc_refs[0][rows, :] + w1 * acc_refs[1][rows, :] + w2 * acc_refs[2][rows, :]
        den = w0 * l_refs[0][rows, :] + w1 * l_refs[1][rows, :] + w2 * l_refs[2][rows, :]
        o_ref[rows, :] = (num / den).astype(o_ref.dtype)


def _attn(h):
    s = h.shape[0]
    c = ATTN_CHUNK
    blk = SUB_BLOCK
    npat = len(DILATIONS)
    chunk = lambda col0: pl.BlockSpec((c, HEAD_DIM), lambda n, hh: (n, col0 + hh))
    prev = lambda col0: pl.BlockSpec((c, HEAD_DIM), lambda n, hh: (jnp.maximum(n - 1, 0), col0 + hh))

    def body(q_ref, kc_ref, kp_ref, vc_ref, vp_ref, o_ref, bias, *scr):
        groups = [scr[i * npat:(i + 1) * npat] for i in range(6)]
        _attn_kernel(q_ref, kc_ref, kp_ref, vc_ref, vp_ref, o_ref, bias, *groups, scr[6 * npat:])

    kv_scratch = [pltpu.VMEM((dil * blk + c, HEAD_DIM), BF16) for dil in DILATIONS]
    return pl.pallas_call(
        body,
        grid=(s // c, ATTN_HEADS),
        in_specs=[chunk(0), chunk(ATTN_HEADS), prev(ATTN_HEADS), chunk(2 * ATTN_HEADS), prev(2 * ATTN_HEADS)],
        out_specs=pl.BlockSpec((c, HEAD_DIM), lambda n, hh: (n, hh)),
        out_shape=jax.ShapeDtypeStruct((s, ATTN_WIDTH), BF16),
        scratch_shapes=(
            [pltpu.VMEM((2, blk, 2 * blk), F32)]
            + [pltpu.VMEM((c, HEAD_DIM), BF16)] * npat
            + kv_scratch + kv_scratch
            + [pltpu.VMEM((c, HEAD_DIM), F32)] * (3 * npat)
            + [pltpu.VMEM((c, HEAD_DIM), F32)] * 5),
        compiler_params=pltpu.CompilerParams(
            dimension_semantics=("arbitrary", "arbitrary"), vmem_limit_bytes=VMEM_LIMIT_BYTES),
        name="dilated_attn",
    )(h, h, h, h, h)


def _mix_kernel(oat_ref, u_ref, uh_ref, ga_ref, gp_ref, x_ref, wpool_ref, pscale_ref,
                wba_ref, wbb_ref, wout_ref, g_ref, b_ref, w1_ref, w2_ref,
                out_ref, w1b_ref, w2b_ref, ubuf_ref, pm_ref):
    i = pl.program_id(0)
    tm = u_ref.shape[0]
    halo = MAX_POOL_WINDOW

    @pl.when(i == 0)
    def _():
        ubuf_ref[0:halo, :] = jnp.zeros((halo, POOL_WIDTH), F32)

    @pl.when(i > 0)
    def _():
        ubuf_ref[0:halo, :] = uh_ref[...]

    ubuf_ref[halo:halo + tm, :] = u_ref[...]

    t_glob = i * tm + lax.broadcasted_iota(jnp.int32, (tm, 1), 0)
    for g, w in enumerate(POOL_WINDOWS):
        cols = slice(g * POOL_GROUP_WIDTH, (g + 1) * POOL_GROUP_WIDTH)
        ug = u_ref[:, cols]
        wsum = ug
        for jj in range(1, w):
            wsum = wsum + ubuf_ref[halo - jj:halo - jj + tm, cols]
        count = jnp.minimum(t_glob + 1, w).astype(F32)
        pooled = wsum / count - ug
        y = jnp.dot(pooled.astype(BF16), wpool_ref[g], preferred_element_type=F32)
        pm_ref[:, cols] = (y * pscale_ref[:, cols]).astype(BF16)

    y_attn = jnp.dot(oat_ref[...], wba_ref[...], preferred_element_type=F32)
    y_pool = jnp.dot(pm_ref[...], wbb_ref[...], preferred_element_type=F32)
    merged = jax.nn.sigmoid(ga_ref[...]) * y_attn + jax.nn.sigmoid(gp_ref[...]) * y_pool
    mix = jnp.dot(merged.astype(BF16), wout_ref[...], preferred_element_type=F32)
    out_ref[...] = _layer_norm(DEEPNORM_ALPHA * x_ref[...] + mix, g_ref[...], b_ref[...])

    w1b_ref[...] = w1_ref[...].astype(BF16)
    w2b_ref[...] = w2_ref[...].astype(BF16)


def _mix(o_attn, h, x2, w_pool_b, pool_scale, w_ba_b, w_bb_b, w_out_b, ln_g, ln_b, w_ff1, w_ff2):
    s, d = x2.shape
    tm = MIX_TM
    n_i = s // tm
    halo = MAX_POOL_WINDOW
    u_col = (3 * ATTN_WIDTH) // POOL_WIDTH
    ga_col = (3 * ATTN_WIDTH + POOL_WIDTH) // d
    resident = lambda shape: pl.BlockSpec(shape, lambda i: (0,) * len(shape), pipeline_mode=pl.Buffered(1))
    w1_spec = pl.BlockSpec((d, D_FF // n_i), lambda i: (0, i))
    w2_spec = pl.BlockSpec((D_FF // n_i, d), lambda i: (i, 0))
    return pl.pallas_call(
        _mix_kernel,
        grid=(n_i,),
        in_specs=[
            pl.BlockSpec((tm, ATTN_WIDTH), lambda i: (i, 0)),
            pl.BlockSpec((tm, POOL_WIDTH), lambda i: (i, u_col)),
            pl.BlockSpec((halo, POOL_WIDTH), lambda i: (jnp.maximum(i * (tm // halo) - 1, 0), u_col)),
            pl.BlockSpec((tm, d), lambda i: (i, ga_col)),
            pl.BlockSpec((tm, d), lambda i: (i, ga_col + 1)),
            pl.BlockSpec((tm, d), lambda i: (i, 0)),
            resident(w_pool_b.shape),
            resident((1, POOL_WIDTH)),
            resident(w_ba_b.shape),
            resident(w_bb_b.shape),
            resident(w_out_b.shape),
            resident((1, d)),
            resident((1, d)),
            w1_spec,
            w2_spec,
        ],
        out_specs=[pl.BlockSpec((tm, d), lambda i: (i, 0)), w1_spec, w2_spec],
        out_shape=[jax.ShapeDtypeStruct((s, d), F32),
                   jax.ShapeDtypeStruct(w_ff1.shape, BF16), jax.ShapeDtypeStruct(w_ff2.shape, BF16)],
        scratch_shapes=[
            pltpu.VMEM((halo + tm, POOL_WIDTH), F32),
            pltpu.VMEM((tm, POOL_WIDTH), BF16),
        ],
        compiler_params=pltpu.CompilerParams(
            dimension_semantics=("arbitrary",), vmem_limit_bytes=VMEM_LIMIT_BYTES),
        name="mix_ln",
    )(o_attn, h, h, h, h, x2, w_pool_b, pool_scale, w_ba_b, w_bb_b, w_out_b, ln_g, ln_b, w_ff1, w_ff2)


def _ffn_kernel(x_ref, w1_ref, w2_ref, g_ref, b_ref, out_ref, xb_ref):
    j = pl.program_id(1)

    @pl.when(j == 0)
    def _():
        xb_ref[...] = x_ref[...].astype(BF16)
        out_ref[...] = jnp.zeros(out_ref.shape, F32)

    hid = jnp.dot(xb_ref[...], w1_ref[...], preferred_element_type=F32)
    hid = jnp.square(jnp.maximum(hid, 0.0)).astype(BF16)
    out_ref[...] += jnp.dot(hid, w2_ref[...], preferred_element_type=F32)

    @pl.when(j == pl.num_programs(1) - 1)
    def _():
        out_ref[...] = _layer_norm(DEEPNORM_ALPHA * x_ref[...] + out_ref[...], g_ref[...], b_ref[...])


def _ffn(x1, w1_b, w2_b, ln_g, ln_b):
    s, d = x1.shape
    tm, tf = FFN_TM, FFN_TF
    return pl.pallas_call(
        _ffn_kernel,
        grid=(s // tm, D_FF // tf),
        in_specs=[
            pl.BlockSpec((tm, d), lambda i, j: (i, 0)),
            pl.BlockSpec((d, tf), lambda i, j: (0, j)),
            pl.BlockSpec((tf, d), lambda i, j: (j, 0)),
            pl.BlockSpec((1, d), lambda i, j: (0, 0)),
            pl.BlockSpec((1, d), lambda i, j: (0, 0)),
        ],
        out_specs=pl.BlockSpec((tm, d), lambda i, j: (i, 0)),
        out_shape=jax.ShapeDtypeStruct((s, d), F32),
        scratch_shapes=[pltpu.VMEM((tm, d), BF16)],
        compiler_params=pltpu.CompilerParams(
            dimension_semantics=("arbitrary", "arbitrary"), vmem_limit_bytes=VMEM_LIMIT_BYTES),
        name="ffn_ln",
    )(x1, w1_b, w2_b, ln_g, ln_b)


def kernel(x, positions, w_in, w_pool, pool_scale, w_branch_attn, w_branch_pool, w_out,
           ln_mix_g, ln_mix_b, w_ff1, w_ff2, ln_ff_g, ln_ff_b):
    b, s, d = x.shape
    assert (b, s, d) == (1, SEQ, D_MODEL) and w_in.shape[0] == DEPTH
    half = HEAD_DIM // 2
    inv_freq = ROPE_THETA ** (-jnp.arange(half, dtype=F32) / half)
    invf2 = jnp.concatenate([inv_freq, inv_freq]).reshape(1, HEAD_DIM)
    x2 = x.reshape(s, d)
    pos2 = positions.reshape(s, 1)
    for layer in range(DEPTH):
        w_pool2 = w_pool[layer].reshape(POOL_WIDTH, POOL_GROUP_WIDTH)
        h, (w_ba_b, w_bb_b, w_out_b, w_pool_b) = _proj(
            x2, pos2, invf2, w_in[layer].astype(BF16),
            [w_branch_attn[layer], w_branch_pool[layer], w_out[layer], w_pool2])
        o_attn = _attn(h)
        x2, w1_b, w2_b = _mix(
            o_attn, h, x2, w_pool_b.reshape(w_pool[layer].shape), pool_scale[layer].reshape(1, POOL_WIDTH),
            w_ba_b, w_bb_b, w_out_b, ln_mix_g[layer].reshape(1, d), ln_mix_b[layer].reshape(1, d),
            w_ff1[layer], w_ff2[layer])
        x2 = _ffn(x2, w1_b, w2_b, ln_ff_g[layer].reshape(1, d), ln_ff_b[layer].reshape(1, d))
    return x2.reshape(b, s, d)
```

```python
import functools

import jax
import jax.numpy as jnp
from jax import lax
from jax.experimental import pallas as pl
from jax.experimental.pallas import tpu as pltpu

F32 = jnp.float32
BF16 = jnp.bfloat16

D_MODEL = 2048
SEQ = 8192
HEAD_DIM = 128
ATTN_WIDTH = D_MODEL // 2
ATTN_HEADS = ATTN_WIDTH // HEAD_DIM
POOL_WIDTH = D_MODEL // 2
POOL_WINDOWS = (2, 4, 8, 16)
POOL_GROUP_WIDTH = POOL_WIDTH // len(POOL_WINDOWS)
MAX_POOL_WINDOW = max(POOL_WINDOWS)
DILATIONS = (1, 4, 16)
SUB_BLOCK = 128
D_FF = 4 * D_MODEL
IN_WIDTH = 3 * ATTN_WIDTH + POOL_WIDTH + 2 * D_MODEL
MAIN_WIDTH = 3 * ATTN_WIDTH + POOL_WIDTH
ROPE_THETA = 10000.0
LN_EPS = 1e-5
DEPTH = 1
DEEPNORM_ALPHA = (2.0 * DEPTH) ** 0.25
SM_SCALE = HEAD_DIM ** -0.5
LOG2_E = 1.4426950408889634
MASK_VALUE = -1e30

VMEM_LIMIT_BYTES = 56 * 1024 * 1024

ATTN_CHUNK = max(DILATIONS) * SUB_BLOCK
PROJ_TM, PROJ_TN = 1024, 1024
MIX_TM, MIX_TN = 512, 512
FFN_TM, FFN_TF = 1024, 512


def _layer_norm(y, g, b):
    mu = jnp.mean(y, axis=-1, keepdims=True)
    yc = y - mu
    var = jnp.mean(yc * yc, axis=-1, keepdims=True)
    return yc * lax.rsqrt(var + LN_EPS) * g + b


def _proj_kernel(x_ref, pos_ref, invf_ref, w_ref, *rest, n_rope_blocks, n_f32_blocks, n_cast):
    cast_in, (main_ref, gate_ref) = rest[:n_cast], rest[n_cast:n_cast + 2]
    cast_out = rest[n_cast + 2:2 * n_cast + 2]
    xb_ref, cos_ref, sin_ref = rest[2 * n_cast + 2:]
    j = pl.program_id(1)
    tm = x_ref.shape[0]
    half = HEAD_DIM // 2

    @pl.when(j == 0)
    def _():
        xb_ref[...] = x_ref[...].astype(BF16)
        lane = lax.broadcasted_iota(jnp.int32, (1, HEAD_DIM), 1)
        low = lane < half
        pos = jnp.where(low, pos_ref[0:tm // 2, :], pos_ref[tm // 2:tm, :])
        ang = pos.astype(F32) * invf_ref[...]
        sign = jnp.where(low, -1.0, 1.0).astype(F32)
        for table_ref, tab in ((cos_ref, jnp.cos(ang)), (sin_ref, jnp.sin(ang))):
            swapped = pltpu.roll(tab, half, 1)
            scale = sign if table_ref is sin_ref else 1.0
            table_ref[0:tm // 2, :] = jnp.where(low, tab, swapped) * scale
            table_ref[tm // 2:tm, :] = jnp.where(low, swapped, tab) * scale

    for src, dst in zip(cast_in, cast_out):
        dst[...] = src[...].astype(BF16)

    def project():
        return jnp.dot(xb_ref[...], w_ref[...], preferred_element_type=F32)

    @pl.when(j < n_rope_blocks)
    def _():
        acc = project()
        cos = cos_ref[...]
        sin = sin_ref[...]
        for hh in range(acc.shape[1] // HEAD_DIM):
            cols = slice(hh * HEAD_DIM, (hh + 1) * HEAD_DIM)
            t = acc[:, cols]
            main_ref[:, cols] = t * cos + pltpu.roll(t, half, 1) * sin

    @pl.when(jnp.logical_and(j >= n_rope_blocks, j < n_f32_blocks))
    def _():
        main_ref[...] = project()

    @pl.when(j >= n_f32_blocks)
    def _():
        gate_ref[...] = project().astype(BF16)


def _slab_specs(weights, n_steps, step_index):
    specs, shapes = [], []
    for w in weights:
        rows, cols = w.shape
        specs.append(pl.BlockSpec((rows // n_steps, cols), lambda *g: (step_index(*g), 0)))
        shapes.append(jax.ShapeDtypeStruct(w.shape, BF16))
    return specs, shapes


def _proj(x2, pos2, invf2, w_in_b, cast_weights):
    s, d = x2.shape
    tm, tn = PROJ_TM, PROJ_TN
    n_i, n_j = s // tm, IN_WIDTH // tn
    n_rope_blocks = (2 * ATTN_WIDTH) // tn
    n_f32_blocks = MAIN_WIDTH // tn
    cast_specs, cast_shapes = _slab_specs(cast_weights, n_i * n_j, lambda i, j: i * n_j + j)
    main_spec = pl.BlockSpec((tm, tn), lambda i, j: (i, jnp.minimum(j, n_f32_blocks - 1)))
    gate_spec = pl.BlockSpec((tm, tn), lambda i, j: (i, jnp.maximum(j - n_f32_blocks, 0)))
    outs = pl.pallas_call(
        functools.partial(_proj_kernel, n_rope_blocks=n_rope_blocks, n_f32_blocks=n_f32_blocks,
                          n_cast=len(cast_weights)),
        grid=(n_i, n_j),
        in_specs=[
            pl.BlockSpec((tm, d), lambda i, j: (i, 0)),
            pl.BlockSpec((tm, 1), lambda i, j: (i, 0)),
            pl.BlockSpec((1, HEAD_DIM), lambda i, j: (0, 0)),
            pl.BlockSpec((d, tn), lambda i, j: (0, j)),
        ] + cast_specs,
        out_specs=[main_spec, gate_spec] + cast_specs,
        out_shape=[jax.ShapeDtypeStruct((s, MAIN_WIDTH), F32),
                   jax.ShapeDtypeStruct((s, IN_WIDTH - MAIN_WIDTH), BF16)] + cast_shapes,
        scratch_shapes=[
            pltpu.VMEM((tm, d), BF16),
            pltpu.VMEM((tm, HEAD_DIM), F32),
            pltpu.VMEM((tm, HEAD_DIM), F32),
        ],
        compiler_params=pltpu.CompilerParams(
            dimension_semantics=("arbitrary", "arbitrary"), vmem_limit_bytes=VMEM_LIMIT_BYTES),
        name="proj_rope",
    )(x2, pos2, invf2, w_in_b, *cast_weights)
    return outs[0], outs[1], outs[2:]


def _attn_kernel(q_ref, kc_ref, kp_ref, vc_ref, vp_ref, o_ref, bias_ref,
                 qd_refs, kd_refs, vd_refs, acc_refs, m_refs, l_refs, stage_refs):
    n = pl.program_id(0)
    blk = SUB_BLOCK
    names = ("q", "kc", "kp", "vc", "vp")
    src = dict(zip(names, (q_ref, kc_ref, kp_ref, vc_ref, vp_ref)))
    src_dil = 1
    stage = dict(zip(names, stage_refs))

    qi = lax.broadcasted_iota(jnp.int32, (blk, 2 * blk), 0)
    kj = lax.broadcasted_iota(jnp.int32, (blk, 2 * blk), 1)
    band = (kj >= qi) & (kj <= qi + blk)
    bias_ref[0] = jnp.where(band, 0.0, MASK_VALUE).astype(F32)
    bias_ref[1] = jnp.where(band & (kj >= blk), 0.0, MASK_VALUE).astype(F32)
    first_chunk = (n == 0).astype(jnp.int32)

    for p, dil in enumerate(DILATIONS):
        m_len = ATTN_CHUNK // dil
        nb = m_len // blk
        krows = blk + m_len
        qd_ref, kd_ref, vd_ref = qd_refs[p], kd_refs[p], vd_refs[p]
        acc_ref, m_ref, l_ref = acc_refs[p], m_refs[p], l_refs[p]

        f = dil // src_dil
        src_len = ATTN_CHUNK // src_dil
        keep_f32 = f > 1 and p + 1 < len(DILATIONS)

        def seg(name, r, m0, rows, f=f, src=src, src_dil=src_dil, src_len=src_len, m_len=m_len):
            if f == 1:
                return src[name][pl.ds(r * m_len + m0, rows), :]
            row0 = (r % src_dil) * src_len + r // src_dil + f * m0
            return src[name][pl.ds(row0, rows, stride=f), :]

        for r in range(dil):
            for name, dst, dst_rows in (("q", qd_ref, m_len), ("k", kd_ref, krows), ("v", vd_ref, krows)):
                if name == "q":
                    cur = seg("q", r, 0, m_len)
                    if keep_f32:
                        stage["q"][r * m_len:(r + 1) * m_len, :] = cur
                    dst[r * m_len:(r + 1) * m_len, :] = cur.astype(BF16)
                    continue
                if keep_f32:
                    prev_full = seg(name + "p", r, 0, m_len)
                    stage[name + "p"][r * m_len:(r + 1) * m_len, :] = prev_full
                    prev_tail = prev_full[m_len - blk:, :]
                else:
                    prev_tail = seg(name + "p", r, m_len - blk, blk)
                cur = seg(name + "c", r, 0, m_len)
                if keep_f32:
                    stage[name + "c"][r * m_len:(r + 1) * m_len, :] = cur
                dst[r * dst_rows:r * dst_rows + blk, :] = prev_tail.astype(BF16)
                dst[r * dst_rows + blk:(r + 1) * dst_rows, :] = cur.astype(BF16)
        if keep_f32:
            src, src_dil = stage, dil

        for r in range(dil):
            for b in range(nb):
                q0 = r * m_len + b * blk
                k0 = r * krows + b * blk
                qb = qd_ref[q0:q0 + blk, :]
                kb = kd_ref[k0:k0 + 2 * blk, :]
                vb = vd_ref[k0:k0 + 2 * blk, :]
                s = lax.dot_general(qb, kb, (((1,), (1,)), ((), ())), preferred_element_type=F32)
                s = s + (bias_ref[first_chunk] if b == 0 else bias_ref[0])
                m_raw = jnp.max(s, axis=-1, keepdims=True)
                pr = jnp.exp2((s - m_raw) * (SM_SCALE * LOG2_E))
                l = jnp.sum(pr, axis=-1, keepdims=True)
                acc = jnp.dot(pr.astype(BF16), vb, preferred_element_type=F32)
                rows = pl.ds(b * (blk * dil) + r, blk, stride=dil)
                acc_ref[rows, :] = acc
                m_ref[rows, :] = jnp.broadcast_to(m_raw * SM_SCALE, (blk, HEAD_DIM))
                l_ref[rows, :] = jnp.broadcast_to(l, (blk, HEAD_DIM))

    for c in range(ATTN_CHUNK // blk):
        rows = slice(c * blk, (c + 1) * blk)
        m0, m1, m2 = m_refs[0][rows, :], m_refs[1][rows, :], m_refs[2][rows, :]
        mm = jnp.maximum(jnp.maximum(m0, m1), m2)
        w0, w1, w2 = jnp.exp(m0 - mm), jnp.exp(m1 - mm), jnp.exp(m2 - mm)
        num = w0 * acc_refs[0][rows, :] + w1 * acc_refs[1][rows, :] + w2 * acc_refs[2][rows, :]
        den = w0 * l_refs[0][rows, :] + w1 * l_refs[1][rows, :] + w2 * l_refs[2][rows, :]
        o_ref[rows, :] = (num / den).astype(o_ref.dtype)


def _attn(h, cast_weights):
    s = h.shape[0]
    c = ATTN_CHUNK
    blk = SUB_BLOCK
    npat = len(DILATIONS)
    n_cast = len(cast_weights)
    n_chunks = s // c
    chunk = lambda col0: pl.BlockSpec((c, HEAD_DIM), lambda n, hh: (n, col0 + hh))
    prev = lambda col0: pl.BlockSpec((c, HEAD_DIM), lambda n, hh: (jnp.maximum(n - 1, 0), col0 + hh))
    cast_specs, cast_shapes = _slab_specs(cast_weights, n_chunks * ATTN_HEADS, lambda n, hh: n * ATTN_HEADS + hh)

    def body(q_ref, kc_ref, kp_ref, vc_ref, vp_ref, *rest):
        cast_in, o_ref, cast_out = rest[:n_cast], rest[n_cast], rest[n_cast + 1:2 * n_cast + 1]
        bias, scr = rest[2 * n_cast + 1], rest[2 * n_cast + 2:]
        for src, dst in zip(cast_in, cast_out):
            dst[...] = src[...].astype(BF16)
        groups = [scr[i * npat:(i + 1) * npat] for i in range(6)]
        _attn_kernel(q_ref, kc_ref, kp_ref, vc_ref, vp_ref, o_ref, bias, *groups, scr[6 * npat:])

    kv_scratch = [pltpu.VMEM((dil * blk + c, HEAD_DIM), BF16) for dil in DILATIONS]
    outs = pl.pallas_call(
        body,
        grid=(n_chunks, ATTN_HEADS),
        in_specs=[chunk(0), chunk(ATTN_HEADS), prev(ATTN_HEADS), chunk(2 * ATTN_HEADS),
                  prev(2 * ATTN_HEADS)] + cast_specs,
        out_specs=[pl.BlockSpec((c, HEAD_DIM), lambda n, hh: (n, hh))] + cast_specs,
        out_shape=[jax.ShapeDtypeStruct((s, ATTN_WIDTH), BF16)] + cast_shapes,
        scratch_shapes=(
            [pltpu.VMEM((2, blk, 2 * blk), F32)]
            + [pltpu.VMEM((c, HEAD_DIM), BF16)] * npat
            + kv_scratch + kv_scratch
            + [pltpu.VMEM((c, HEAD_DIM), F32)] * (3 * npat)
            + [pltpu.VMEM((c, HEAD_DIM), F32)] * 5),
        compiler_params=pltpu.CompilerParams(
            dimension_semantics=("arbitrary", "arbitrary"), vmem_limit_bytes=VMEM_LIMIT_BYTES),
        name="dilated_attn",
    )(h, h, h, h, h, *cast_weights)
    return outs[0], outs[1:]


def _mix_kernel(oat_ref, u_ref, uh_ref, ga_ref, gp_ref, x_ref, wpool_ref, pscale_ref,
                wba_ref, wbb_ref, wout_ref, g_ref, b_ref, out_ref, ubuf_ref, pm_ref, mg_ref):
    i = pl.program_id(0)
    tm = u_ref.shape[0]
    halo = MAX_POOL_WINDOW

    @pl.when(i == 0)
    def _():
        ubuf_ref[0:halo, :] = jnp.zeros((halo, POOL_WIDTH), F32)

    @pl.when(i > 0)
    def _():
        ubuf_ref[0:halo, :] = uh_ref[...]

    ubuf_ref[halo:halo + tm, :] = u_ref[...]

    t_glob = i * tm + lax.broadcasted_iota(jnp.int32, (tm, 1), 0)
    for g, w in enumerate(POOL_WINDOWS):
        cols = slice(g * POOL_GROUP_WIDTH, (g + 1) * POOL_GROUP_WIDTH)
        ug = u_ref[:, cols]
        wsum = ug
        for jj in range(1, w):
            wsum = wsum + ubuf_ref[halo - jj:halo - jj + tm, cols]
        count = jnp.minimum(t_glob + 1, w).astype(F32)
        pooled = wsum / count - ug
        y = jnp.dot(pooled.astype(BF16), wpool_ref[g], preferred_element_type=F32)
        pm_ref[:, cols] = (y * pscale_ref[:, cols]).astype(BF16)

    for c in range(D_MODEL // MIX_TN):
        cols = slice(c * MIX_TN, (c + 1) * MIX_TN)
        y_attn = jnp.dot(oat_ref[...], wba_ref[:, cols], preferred_element_type=F32)
        y_pool = jnp.dot(pm_ref[...], wbb_ref[:, cols], preferred_element_type=F32)
        merged = (jax.nn.sigmoid(ga_ref[:, cols].astype(F32)) * y_attn
                  + jax.nn.sigmoid(gp_ref[:, cols].astype(F32)) * y_pool)
        mg_ref[:, cols] = merged.astype(BF16)

    mix = jnp.dot(mg_ref[...], wout_ref[...], preferred_element_type=F32)
    out_ref[...] = _layer_norm(DEEPNORM_ALPHA * x_ref[...] + mix, g_ref[...], b_ref[...])


def _mix(o_attn, h_main, h_gates, x2, w_pool_b, pool_scale, w_ba_b, w_bb_b, w_out_b, ln_g, ln_b):
    s, d = x2.shape
    tm = MIX_TM
    halo = MAX_POOL_WINDOW
    u_col = (3 * ATTN_WIDTH) // POOL_WIDTH
    resident = lambda shape: pl.BlockSpec(shape, lambda i: (0,) * len(shape), pipeline_mode=pl.Buffered(1))
    return pl.pallas_call(
        _mix_kernel,
        grid=(s // tm,),
        in_specs=[
            pl.BlockSpec((tm, ATTN_WIDTH), lambda i: (i, 0)),
            pl.BlockSpec((tm, POOL_WIDTH), lambda i: (i, u_col)),
            pl.BlockSpec((halo, POOL_WIDTH), lambda i: (jnp.maximum(i * (tm // halo) - 1, 0), u_col)),
            pl.BlockSpec((tm, d), lambda i: (i, 0)),
            pl.BlockSpec((tm, d), lambda i: (i, 1)),
            pl.BlockSpec((tm, d), lambda i: (i, 0)),
            resident(w_pool_b.shape),
            resident((1, POOL_WIDTH)),
            resident(w_ba_b.shape),
            resident(w_bb_b.shape),
            resident(w_out_b.shape),
            resident((1, d)),
            resident((1, d)),
        ],
        out_specs=pl.BlockSpec((tm, d), lambda i: (i, 0)),
        out_shape=jax.ShapeDtypeStruct((s, d), F32),
        scratch_shapes=[
            pltpu.VMEM((halo + tm, POOL_WIDTH), F32),
            pltpu.VMEM((tm, POOL_WIDTH), BF16),
            pltpu.VMEM((tm, d), BF16),
        ],
        compiler_params=pltpu.CompilerParams(
            dimension_semantics=("arbitrary",), vmem_limit_bytes=VMEM_LIMIT_BYTES),
        name="mix_ln",
    )(o_attn, h_main, h_main, h_gates, h_gates, x2, w_pool_b, pool_scale, w_ba_b, w_bb_b, w_out_b, ln_g, ln_b)


def _ffn_kernel(x_ref, w1_ref, w2_ref, g_ref, b_ref, out_ref, xb_ref):
    j = pl.program_id(1)

    @pl.when(j == 0)
    def _():
        xb_ref[...] = x_ref[...].astype(BF16)
        out_ref[...] = jnp.zeros(out_ref.shape, F32)

    hid = jnp.dot(xb_ref[...], w1_ref[...], preferred_element_type=F32)
    hid = jnp.square(jnp.maximum(hid, 0.0)).astype(BF16)
    out_ref[...] += jnp.dot(hid, w2_ref[...], preferred_element_type=F32)

    @pl.when(j == pl.num_programs(1) - 1)
    def _():
        out_ref[...] = _layer_norm(DEEPNORM_ALPHA * x_ref[...] + out_ref[...], g_ref[...], b_ref[...])


def _ffn(x1, w1_b, w2_b, ln_g, ln_b):
    s, d = x1.shape
    tm, tf = FFN_TM, FFN_TF
    return pl.pallas_call(
        _ffn_kernel,
        grid=(s // tm, D_FF // tf),
        in_specs=[
            pl.BlockSpec((tm, d), lambda i, j: (i, 0)),
            pl.BlockSpec((d, tf), lambda i, j: (0, j)),
            pl.BlockSpec((tf, d), lambda i, j: (j, 0)),
            pl.BlockSpec((1, d), lambda i, j: (0, 0)),
            pl.BlockSpec((1, d), lambda i, j: (0, 0)),
        ],
        out_specs=pl.BlockSpec((tm, d), lambda i, j: (i, 0)),
        out_shape=jax.ShapeDtypeStruct((s, d), F32),
        scratch_shapes=[pltpu.VMEM((tm, d), BF16)],
        compiler_params=pltpu.CompilerParams(
            dimension_semantics=("arbitrary", "arbitrary"), vmem_limit_bytes=VMEM_LIMIT_BYTES),
        name="ffn_ln",
    )(x1, w1_b, w2_b, ln_g, ln_b)


def kernel(x, positions, w_in, w_pool, pool_scale, w_branch_attn, w_branch_pool, w_out,
           ln_mix_g, ln_mix_b, w_ff1, w_ff2, ln_ff_g, ln_ff_b):
    b, s, d = x.shape
    assert (b, s, d) == (1, SEQ, D_MODEL) and w_in.shape[0] == DEPTH
    half = HEAD_DIM // 2
    inv_freq = ROPE_THETA ** (-jnp.arange(half, dtype=F32) / half)
    invf2 = jnp.concatenate([inv_freq, inv_freq]).reshape(1, HEAD_DIM)
    x2 = x.reshape(s, d)
    pos2 = positions.reshape(s, 1)
    for layer in range(DEPTH):
        w_pool2 = w_pool[layer].reshape(POOL_WIDTH, POOL_GROUP_WIDTH)
        h_main, h_gates, (w_ba_b, w_bb_b, w_out_b, w_pool_b) = _proj(
            x2, pos2, invf2, w_in[layer].astype(BF16),
            [w_branch_attn[layer], w_branch_pool[layer], w_out[layer], w_pool2])
        o_attn, (w1_b, w2_b) = _attn(h_main, [w_ff1[layer], w_ff2[layer]])
        x2 = _mix(o_attn, h_main, h_gates, x2, w_pool_b.reshape(w_pool[layer].shape),
                  pool_scale[layer].reshape(1, POOL_WIDTH), w_ba_b, w_bb_b, w_out_b,
                  ln_mix_g[layer].reshape(1, d), ln_mix_b[layer].reshape(1, d))
        x2 = _ffn(x2, w1_b, w2_b, ln_ff_g[layer].reshape(1, d), ln_ff_b[layer].reshape(1, d))
    return x2.reshape(b, s, d)
```

```python
import functools

import jax
import jax.numpy as jnp
from jax import lax
from jax.experimental import pallas as pl
from jax.experimental.pallas import tpu as pltpu

F32 = jnp.float32
BF16 = jnp.bfloat16

D_MODEL = 2048
SEQ = 8192
HEAD_DIM = 128
ATTN_WIDTH = D_MODEL // 2
ATTN_HEADS = ATTN_WIDTH // HEAD_DIM
POOL_WIDTH = D_MODEL // 2
POOL_WINDOWS = (2, 4, 8, 16)
POOL_GROUP_WIDTH = POOL_WIDTH // len(POOL_WINDOWS)
MAX_POOL_WINDOW = max(POOL_WINDOWS)
DILATIONS = (1, 4, 16)
SUB_BLOCK = 128
D_FF = 4 * D_MODEL
IN_WIDTH = 3 * ATTN_WIDTH + POOL_WIDTH + 2 * D_MODEL
MAIN_WIDTH = 3 * ATTN_WIDTH + POOL_WIDTH
ROPE_THETA = 10000.0
LN_EPS = 1e-5
DEPTH = 1
DEEPNORM_ALPHA = (2.0 * DEPTH) ** 0.25
SM_SCALE = HEAD_DIM ** -0.5
LOG2_E = 1.4426950408889634
MASK_VALUE = -1e30

VMEM_LIMIT_BYTES = 56 * 1024 * 1024

ATTN_CHUNK = max(DILATIONS) * SUB_BLOCK
PROJ_TM, PROJ_TN = 1024, 1024
MIX_TM, MIX_TN = 512, 512
FFN_TM, FFN_TF = 1024, 512


def _layer_norm(y, g, b):
    mu = jnp.mean(y, axis=-1, keepdims=True)
    yc = y - mu
    var = jnp.mean(yc * yc, axis=-1, keepdims=True)
    return yc * lax.rsqrt(var + LN_EPS) * g + b


def _proj_kernel(x_ref, pos_ref, invf_ref, w_ref, *rest, n_rope_blocks, n_f32_blocks, n_cast):
    cast_in, (main_ref, gate_ref) = rest[:n_cast], rest[n_cast:n_cast + 2]
    cast_out = rest[n_cast + 2:2 * n_cast + 2]
    xb_ref, cos_ref, sin_ref = rest[2 * n_cast + 2:]
    j = pl.program_id(1)
    tm = x_ref.shape[0]
    half = HEAD_DIM // 2

    @pl.when(j == 0)
    def _():
        xb_ref[...] = x_ref[...].astype(BF16)
        lane = lax.broadcasted_iota(jnp.int32, (1, HEAD_DIM), 1)
        low = lane < half
        pos = jnp.where(low, pos_ref[0:tm // 2, :], pos_ref[tm // 2:tm, :])
        ang = pos.astype(F32) * invf_ref[...]
        sign = jnp.where(low, -1.0, 1.0).astype(F32)
        for table_ref, tab in ((cos_ref, jnp.cos(ang)), (sin_ref, jnp.sin(ang))):
            swapped = pltpu.roll(tab, half, 1)
            scale = sign if table_ref is sin_ref else 1.0
            table_ref[0:tm // 2, :] = jnp.where(low, tab, swapped) * scale
            table_ref[tm // 2:tm, :] = jnp.where(low, swapped, tab) * scale

    for src, dst in zip(cast_in, cast_out):
        dst[...] = src[...].astype(BF16)

    def project():
        return jnp.dot(xb_ref[...], w_ref[...], preferred_element_type=F32)

    @pl.when(j < n_rope_blocks)
    def _():
        acc = project()
        cos = cos_ref[...]
        sin = sin_ref[...]
        for hh in range(acc.shape[1] // HEAD_DIM):
            cols = slice(hh * HEAD_DIM, (hh + 1) * HEAD_DIM)
            t = acc[:, cols]
            main_ref[:, cols] = t * cos + pltpu.roll(t, half, 1) * sin

    @pl.when(jnp.logical_and(j >= n_rope_blocks, j < n_f32_blocks))
    def _():
        main_ref[...] = project()

    @pl.when(j >= n_f32_blocks)
    def _():
        gate_ref[...] = project().astype(BF16)


def _slab_specs(weights, n_steps, step_index):
    specs, shapes = [], []
    for w in weights:
        rows, cols = w.shape
        specs.append(pl.BlockSpec((rows // n_steps, cols), lambda *g: (step_index(*g), 0)))
        shapes.append(jax.ShapeDtypeStruct(w.shape, BF16))
    return specs, shapes


def _proj(x2, pos2, invf2, w_in_b, cast_weights):
    s, d = x2.shape
    tm, tn = PROJ_TM, PROJ_TN
    n_i, n_j = s // tm, IN_WIDTH // tn
    n_rope_blocks = (2 * ATTN_WIDTH) // tn
    n_f32_blocks = MAIN_WIDTH // tn
    cast_specs, cast_shapes = _slab_specs(cast_weights, n_i * n_j, lambda i, j: i * n_j + j)
    main_spec = pl.BlockSpec((tm, tn), lambda i, j: (i, jnp.minimum(j, n_f32_blocks - 1)))
    gate_spec = pl.BlockSpec((tm, tn), lambda i, j: (i, jnp.maximum(j - n_f32_blocks, 0)))
    outs = pl.pallas_call(
        functools.partial(_proj_kernel, n_rope_blocks=n_rope_blocks, n_f32_blocks=n_f32_blocks,
                          n_cast=len(cast_weights)),
        grid=(n_i, n_j),
        in_specs=[
            pl.BlockSpec((tm, d), lambda i, j: (i, 0)),
            pl.BlockSpec((tm, 1), lambda i, j: (i, 0)),
            pl.BlockSpec((1, HEAD_DIM), lambda i, j: (0, 0)),
            pl.BlockSpec((d, tn), lambda i, j: (0, j)),
        ] + cast_specs,
        out_specs=[main_spec, gate_spec] + cast_specs,
        out_shape=[jax.ShapeDtypeStruct((s, MAIN_WIDTH), F32),
                   jax.ShapeDtypeStruct((s, IN_WIDTH - MAIN_WIDTH), BF16)] + cast_shapes,
        scratch_shapes=[
            pltpu.VMEM((tm, d), BF16),
            pltpu.VMEM((tm, HEAD_DIM), F32),
            pltpu.VMEM((tm, HEAD_DIM), F32),
        ],
        compiler_params=pltpu.CompilerParams(
            dimension_semantics=("arbitrary", "arbitrary"), vmem_limit_bytes=VMEM_LIMIT_BYTES),
        name="proj_rope",
    )(x2, pos2, invf2, w_in_b, *cast_weights)
    return outs[0], outs[1], outs[2:]


def _attn_kernel(q_ref, kc_ref, kp_ref, vc_ref, vp_ref, o_ref, bias_ref,
                 qd_refs, kd_refs, vd_refs, o_refs, lse_refs, stage_refs):
    n = pl.program_id(0)
    blk = SUB_BLOCK

    @pl.when(jnp.logical_and(n == 0, pl.program_id(1) == 0))
    def _():
        for vd_ref in vd_refs:
            vd_ref[:, HEAD_DIM:] = jnp.ones((vd_ref.shape[0], HEAD_DIM), BF16)

    names = ("q", "kc", "kp", "vc", "vp")
    src = dict(zip(names, (q_ref, kc_ref, kp_ref, vc_ref, vp_ref)))
    src_dil = 1
    stage = dict(zip(names, stage_refs))

    qi = lax.broadcasted_iota(jnp.int32, (blk, 2 * blk), 0)
    kj = lax.broadcasted_iota(jnp.int32, (blk, 2 * blk), 1)
    band = (kj >= qi) & (kj <= qi + blk)
    bias_ref[0] = jnp.where(band, 0.0, MASK_VALUE).astype(F32)
    bias_ref[1] = jnp.where(band & (kj >= blk), 0.0, MASK_VALUE).astype(F32)
    first_chunk = (n == 0).astype(jnp.int32)

    for p, dil in enumerate(DILATIONS):
        m_len = ATTN_CHUNK // dil
        nb = m_len // blk
        krows = blk + m_len
        qd_ref, kd_ref, vd_ref = qd_refs[p], kd_refs[p], vd_refs[p]

        f = dil // src_dil
        src_len = ATTN_CHUNK // src_dil
        keep_f32 = f > 1 and p + 1 < len(DILATIONS)

        def seg(name, r, m0, rows, f=f, src=src, src_dil=src_dil, src_len=src_len, m_len=m_len):
            if f == 1:
                return src[name][pl.ds(r * m_len + m0, rows), :]
            row0 = (r % src_dil) * src_len + r // src_dil + f * m0
            return src[name][pl.ds(row0, rows, stride=f), :]

        for r in range(dil):
            for name, dst, dst_rows in (("q", qd_ref, m_len), ("k", kd_ref, krows), ("v", vd_ref, krows)):
                if name == "q":
                    cur = seg("q", r, 0, m_len)
                    if keep_f32:
                        stage["q"][r * m_len:(r + 1) * m_len, :] = cur
                    dst[r * m_len:(r + 1) * m_len, :] = cur.astype(BF16)
                    continue
                if keep_f32:
                    prev_full = seg(name + "p", r, 0, m_len)
                    stage[name + "p"][r * m_len:(r + 1) * m_len, :] = prev_full
                    prev_tail = prev_full[m_len - blk:, :]
                else:
                    prev_tail = seg(name + "p", r, m_len - blk, blk)
                cur = seg(name + "c", r, 0, m_len)
                if keep_f32:
                    stage[name + "c"][r * m_len:(r + 1) * m_len, :] = cur
                lanes = slice(0, HEAD_DIM) if name == "v" else slice(None)
                dst[r * dst_rows:r * dst_rows + blk, lanes] = prev_tail.astype(BF16)
                dst[r * dst_rows + blk:(r + 1) * dst_rows, lanes] = cur.astype(BF16)
        if keep_f32:
            src, src_dil = stage, dil

        for r in range(dil):
            for b in range(nb):
                q0 = r * m_len + b * blk
                k0 = r * krows + b * blk
                qb = qd_ref[q0:q0 + blk, :]
                kb = kd_ref[k0:k0 + 2 * blk, :]
                vb = vd_ref[k0:k0 + 2 * blk, :]
                s = lax.dot_general(qb, kb, (((1,), (1,)), ((), ())), preferred_element_type=F32)
                s = s + (bias_ref[first_chunk] if b == 0 else bias_ref[0])
                m_raw = jnp.max(s, axis=-1, keepdims=True)
                pr = jnp.exp2((s - m_raw) * (SM_SCALE * LOG2_E))
                ext = jnp.dot(pr.astype(BF16), vb, preferred_element_type=F32)
                acc, l = ext[:, :HEAD_DIM], ext[:, HEAD_DIM:]
                rows = pl.ds(b * (blk * dil) + r, blk, stride=dil)
                o_refs[p][rows, :] = acc / l
                lse_refs[p][rows, :] = m_raw * SM_SCALE + jnp.log(l)

    for c in range(ATTN_CHUNK // blk):
        rows = slice(c * blk, (c + 1) * blk)
        lse = [ref[rows, :] for ref in lse_refs]
        top = functools.reduce(jnp.maximum, lse)
        w = [jnp.exp(x - top) for x in lse]
        num = sum(wp * ref[rows, :] for wp, ref in zip(w, o_refs))
        o_ref[rows, :] = (num / sum(w)).astype(o_ref.dtype)


def _attn(h, cast_weights):
    s = h.shape[0]
    c = ATTN_CHUNK
    blk = SUB_BLOCK
    npat = len(DILATIONS)
    n_cast = len(cast_weights)
    n_chunks = s // c
    chunk = lambda col0: pl.BlockSpec((c, HEAD_DIM), lambda n, hh: (n, col0 + hh))
    prev = lambda col0: pl.BlockSpec((c, HEAD_DIM), lambda n, hh: (jnp.maximum(n - 1, 0), col0 + hh))
    cast_specs, cast_shapes = _slab_specs(cast_weights, n_chunks * ATTN_HEADS, lambda n, hh: n * ATTN_HEADS + hh)

    def body(q_ref, kc_ref, kp_ref, vc_ref, vp_ref, *rest):
        cast_in, o_ref, cast_out = rest[:n_cast], rest[n_cast], rest[n_cast + 1:2 * n_cast + 1]
        bias, scr = rest[2 * n_cast + 1], rest[2 * n_cast + 2:]
        for src, dst in zip(cast_in, cast_out):
            dst[...] = src[...].astype(BF16)
        groups = [scr[i * npat:(i + 1) * npat] for i in range(5)]
        _attn_kernel(q_ref, kc_ref, kp_ref, vc_ref, vp_ref, o_ref, bias, *groups, scr[5 * npat:])

    k_scratch = [pltpu.VMEM((dil * blk + c, HEAD_DIM), BF16) for dil in DILATIONS]
    v_scratch = [pltpu.VMEM((dil * blk + c, 2 * HEAD_DIM), BF16) for dil in DILATIONS]
    outs = pl.pallas_call(
        body,
        grid=(n_chunks, ATTN_HEADS),
        in_specs=[chunk(0), chunk(ATTN_HEADS), prev(ATTN_HEADS), chunk(2 * ATTN_HEADS),
                  prev(2 * ATTN_HEADS)] + cast_specs,
        out_specs=[pl.BlockSpec((c, HEAD_DIM), lambda n, hh: (n, hh))] + cast_specs,
        out_shape=[jax.ShapeDtypeStruct((s, ATTN_WIDTH), BF16)] + cast_shapes,
        scratch_shapes=(
            [pltpu.VMEM((2, blk, 2 * blk), F32)]
            + [pltpu.VMEM((c, HEAD_DIM), BF16)] * npat
            + k_scratch + v_scratch
            + [pltpu.VMEM((c, HEAD_DIM), F32)] * (2 * npat)
            + [pltpu.VMEM((c, HEAD_DIM), F32)] * 5),
        compiler_params=pltpu.CompilerParams(
            dimension_semantics=("arbitrary", "arbitrary"), vmem_limit_bytes=VMEM_LIMIT_BYTES),
        name="dilated_attn",
    )(h, h, h, h, h, *cast_weights)
    return outs[0], outs[1:]


def _mix_kernel(oat_ref, u_ref, uh_ref, ga_ref, gp_ref, x_ref, wpool_ref, pscale_ref,
                wba_ref, wbb_ref, wout_ref, g_ref, b_ref, out_ref, ubuf_ref, pm_ref, mg_ref):
    i = pl.program_id(0)
    tm = u_ref.shape[0]
    halo = MAX_POOL_WINDOW

    @pl.when(i == 0)
    def _():
        ubuf_ref[0:halo, :] = jnp.zeros((halo, POOL_WIDTH), F32)

    @pl.when(i > 0)
    def _():
        ubuf_ref[0:halo, :] = uh_ref[...]

    ubuf_ref[halo:halo + tm, :] = u_ref[...]

    t_glob = i * tm + lax.broadcasted_iota(jnp.int32, (tm, 1), 0)
    for g, w in enumerate(POOL_WINDOWS):
        cols = slice(g * POOL_GROUP_WIDTH, (g + 1) * POOL_GROUP_WIDTH)
        ug = u_ref[:, cols]
        wsum = ug
        for jj in range(1, w):
            wsum = wsum + ubuf_ref[halo - jj:halo - jj + tm, cols]
        count = jnp.minimum(t_glob + 1, w).astype(F32)
        pooled = wsum / count - ug
        y = jnp.dot(pooled.astype(BF16), wpool_ref[g], preferred_element_type=F32)
        pm_ref[:, cols] = (y * pscale_ref[:, cols]).astype(BF16)

    for c in range(D_MODEL // MIX_TN):
        cols = slice(c * MIX_TN, (c + 1) * MIX_TN)
        y_attn = jnp.dot(oat_ref[...], wba_ref[:, cols], preferred_element_type=F32)
        y_pool = jnp.dot(pm_ref[...], wbb_ref[:, cols], preferred_element_type=F32)
        merged = (jax.nn.sigmoid(ga_ref[:, cols].astype(F32)) * y_attn
                  + jax.nn.sigmoid(gp_ref[:, cols].astype(F32)) * y_pool)
        mg_ref[:, cols] = merged.astype(BF16)

    mix = jnp.dot(mg_ref[...], wout_ref[...], preferred_element_type=F32)
    out_ref[...] = _layer_norm(DEEPNORM_ALPHA * x_ref[...] + mix, g_ref[...], b_ref[...])


def _mix(o_attn, h_main, h_gates, x2, w_pool_b, pool_scale, w_ba_b, w_bb_b, w_out_b, ln_g, ln_b):
    s, d = x2.shape
    tm = MIX_TM
    halo = MAX_POOL_WINDOW
    u_col = (3 * ATTN_WIDTH) // POOL_WIDTH
    resident = lambda shape: pl.BlockSpec(shape, lambda i: (0,) * len(shape), pipeline_mode=pl.Buffered(1))
    return pl.pallas_call(
        _mix_kernel,
        grid=(s // tm,),
        in_specs=[
            pl.BlockSpec((tm, ATTN_WIDTH), lambda i: (i, 0)),
            pl.BlockSpec((tm, POOL_WIDTH), lambda i: (i, u_col)),
            pl.BlockSpec((halo, POOL_WIDTH), lambda i: (jnp.maximum(i * (tm // halo) - 1, 0), u_col)),
            pl.BlockSpec((tm, d), lambda i: (i, 0)),
            pl.BlockSpec((tm, d), lambda i: (i, 1)),
            pl.BlockSpec((tm, d), lambda i: (i, 0)),
            resident(w_pool_b.shape),
            resident((1, POOL_WIDTH)),
            resident(w_ba_b.shape),
            resident(w_bb_b.shape),
            resident(w_out_b.shape),
            resident((1, d)),
            resident((1, d)),
        ],
        out_specs=pl.BlockSpec((tm, d), lambda i: (i, 0)),
        out_shape=jax.ShapeDtypeStruct((s, d), F32),
        scratch_shapes=[
            pltpu.VMEM((halo + tm, POOL_WIDTH), F32),
            pltpu.VMEM((tm, POOL_WIDTH), BF16),
            pltpu.VMEM((tm, d), BF16),
        ],
        compiler_params=pltpu.CompilerParams(
            dimension_semantics=("arbitrary",), vmem_limit_bytes=VMEM_LIMIT_BYTES),
        name="mix_ln",
    )(o_attn, h_main, h_main, h_gates, h_gates, x2, w_pool_b, pool_scale, w_ba_b, w_bb_b, w_out_b, ln_g, ln_b)


def _ffn_kernel(x_ref, w1_ref, w2_ref, g_ref, b_ref, out_ref, xb_ref):
    j = pl.program_id(1)

    @pl.when(j == 0)
    def _():
        xb_ref[...] = x_ref[...].astype(BF16)
        out_ref[...] = jnp.zeros(out_ref.shape, F32)

    hid = jnp.dot(xb_ref[...], w1_ref[...], preferred_element_type=F32)
    hid = jnp.square(jnp.maximum(hid, 0.0)).astype(BF16)
    out_ref[...] += jnp.dot(hid, w2_ref[...], preferred_element_type=F32)

    @pl.when(j == pl.num_programs(1) - 1)
    def _():
        out_ref[...] = _layer_norm(DEEPNORM_ALPHA * x_ref[...] + out_ref[...], g_ref[...], b_ref[...])


def _ffn(x1, w1_b, w2_b, ln_g, ln_b):
    s, d = x1.shape
    tm, tf = FFN_TM, FFN_TF
    return pl.pallas_call(
        _ffn_kernel,
        grid=(s // tm, D_FF // tf),
        in_specs=[
            pl.BlockSpec((tm, d), lambda i, j: (i, 0)),
            pl.BlockSpec((d, tf), lambda i, j: (0, j)),
            pl.BlockSpec((tf, d), lambda i, j: (j, 0)),
            pl.BlockSpec((1, d), lambda i, j: (0, 0)),
            pl.BlockSpec((1, d), lambda i, j: (0, 0)),
        ],
        out_specs=pl.BlockSpec((tm, d), lambda i, j: (i, 0)),
        out_shape=jax.ShapeDtypeStruct((s, d), F32),
        scratch_shapes=[pltpu.VMEM((tm, d), BF16)],
        compiler_params=pltpu.CompilerParams(
            dimension_semantics=("arbitrary", "arbitrary"), vmem_limit_bytes=VMEM_LIMIT_BYTES),
        name="ffn_ln",
    )(x1, w1_b, w2_b, ln_g, ln_b)


def kernel(x, positions, w_in, w_pool, pool_scale, w_branch_attn, w_branch_pool, w_out,
           ln_mix_g, ln_mix_b, w_ff1, w_ff2, ln_ff_g, ln_ff_b):
    b, s, d = x.shape
    assert (b, s, d) == (1, SEQ, D_MODEL) and w_in.shape[0] == DEPTH
    half = HEAD_DIM // 2
    inv_freq = ROPE_THETA ** (-jnp.arange(half, dtype=F32) / half)
    invf2 = jnp.concatenate([inv_freq, inv_freq]).reshape(1, HEAD_DIM)
    x2 = x.reshape(s, d)
    pos2 = positions.reshape(s, 1)
    for layer in range(DEPTH):
        w_pool2 = w_pool[layer].reshape(POOL_WIDTH, POOL_GROUP_WIDTH)
        h_main, h_gates, (w_ba_b, w_bb_b, w_out_b, w_pool_b) = _proj(
            x2, pos2, invf2, w_in[layer].astype(BF16),
            [w_branch_attn[layer], w_branch_pool[layer], w_out[layer], w_pool2])
        o_attn, (w1_b, w2_b) = _attn(h_main, [w_ff1[layer], w_ff2[layer]])
        x2 = _mix(o_attn, h_main, h_gates, x2, w_pool_b.reshape(w_pool[layer].shape),
                  pool_scale[layer].reshape(1, POOL_WIDTH), w_ba_b, w_bb_b, w_out_b,
                  ln_mix_g[layer].reshape(1, d), ln_mix_b[layer].reshape(1, d))
        x2 = _ffn(x2, w1_b, w2_b, ln_ff_g[layer].reshape(1, d), ln_ff_b[layer].reshape(1, d))
    return x2.reshape(b, s, d)
```

```python
import functools

import jax
import jax.numpy as jnp
from jax import lax
from jax.experimental import pallas as pl
from jax.experimental.pallas import tpu as pltpu

F32 = jnp.float32
BF16 = jnp.bfloat16

D_MODEL = 2048
SEQ = 8192
HEAD_DIM = 128
ATTN_WIDTH = D_MODEL // 2
ATTN_HEADS = ATTN_WIDTH // HEAD_DIM
POOL_WIDTH = D_MODEL // 2
POOL_WINDOWS = (2, 4, 8, 16)
POOL_GROUP_WIDTH = POOL_WIDTH // len(POOL_WINDOWS)
MAX_POOL_WINDOW = max(POOL_WINDOWS)
DILATIONS = (1, 4, 16)
SUB_BLOCK = 128
D_FF = 4 * D_MODEL
IN_WIDTH = 3 * ATTN_WIDTH + POOL_WIDTH + 2 * D_MODEL
MAIN_WIDTH = 3 * ATTN_WIDTH + POOL_WIDTH
ROPE_THETA = 10000.0
LN_EPS = 1e-5
DEPTH = 1
DEEPNORM_ALPHA = (2.0 * DEPTH) ** 0.25
SM_SCALE = HEAD_DIM ** -0.5
LOG2_E = 1.4426950408889634
MASK_VALUE = -1e30

VMEM_LIMIT_BYTES = 60 * 1024 * 1024

ATTN_CHUNK = max(DILATIONS) * SUB_BLOCK
PROJ_TM, PROJ_TN = 1024, 1024
MIX_TM, MIX_TN = 512, 512
FFN_TM, FFN_TF = 1024, 1024
FFN_LN_ROWS = 128


def _layer_norm(y, g, b):
    mu = jnp.mean(y, axis=-1, keepdims=True)
    yc = y - mu
    var = jnp.mean(yc * yc, axis=-1, keepdims=True)
    return yc * lax.rsqrt(var + LN_EPS) * g + b


def _residual_layer_norm(x_ref, y_ref, g_ref, b_ref, out_ref, chunk_rows):
    g = g_ref[...]
    b = b_ref[...]
    n_chunks = x_ref.shape[0] // chunk_rows
    if n_chunks == 1:
        out_ref[...] = _layer_norm(DEEPNORM_ALPHA * x_ref[...] + y_ref[...], g, b)
        return

    def body(c, carry):
        rows = pl.ds(pl.multiple_of(c * chunk_rows, chunk_rows), chunk_rows)
        out_ref[rows, :] = _layer_norm(DEEPNORM_ALPHA * x_ref[rows, :] + y_ref[rows, :], g, b)
        return carry

    lax.fori_loop(0, n_chunks, body, 0)


def _proj_kernel(x_ref, pos_ref, invf_ref, w_ref, *rest, n_rope_blocks, n_f32_blocks, n_cast):
    cast_in, (main_ref, gate_ref) = rest[:n_cast], rest[n_cast:n_cast + 2]
    cast_out = rest[n_cast + 2:2 * n_cast + 2]
    xb_ref, cos_ref, sin_ref = rest[2 * n_cast + 2:]
    j = pl.program_id(1)
    tm = x_ref.shape[0]
    half = HEAD_DIM // 2

    @pl.when(j == 0)
    def _():
        xb_ref[...] = x_ref[...].astype(BF16)
        lane = lax.broadcasted_iota(jnp.int32, (1, HEAD_DIM), 1)
        low = lane < half
        pos = jnp.where(low, pos_ref[0:tm // 2, :], pos_ref[tm // 2:tm, :])
        ang = pos.astype(F32) * invf_ref[...]
        sign = jnp.where(low, -1.0, 1.0).astype(F32)
        for table_ref, tab in ((cos_ref, jnp.cos(ang)), (sin_ref, jnp.sin(ang))):
            swapped = pltpu.roll(tab, half, 1)
            scale = sign if table_ref is sin_ref else 1.0
            table_ref[0:tm // 2, :] = jnp.where(low, tab, swapped) * scale
            table_ref[tm // 2:tm, :] = jnp.where(low, swapped, tab) * scale

    for src, dst in zip(cast_in, cast_out):
        dst[...] = src[...].astype(BF16)

    def project():
        return jnp.dot(xb_ref[...], w_ref[...], preferred_element_type=F32)

    @pl.when(j < n_rope_blocks)
    def _():
        acc = project()
        cos = cos_ref[...]
        sin = sin_ref[...]
        for hh in range(acc.shape[1] // HEAD_DIM):
            cols = slice(hh * HEAD_DIM, (hh + 1) * HEAD_DIM)
            t = acc[:, cols]
            main_ref[:, cols] = t * cos + pltpu.roll(t, half, 1) * sin

    @pl.when(jnp.logical_and(j >= n_rope_blocks, j < n_f32_blocks))
    def _():
        main_ref[...] = project()

    @pl.when(j >= n_f32_blocks)
    def _():
        gate_ref[...] = project().astype(BF16)


def _slab_specs(weights, n_steps, step_index):
    specs, shapes = [], []
    for w in weights:
        rows, cols = w.shape
        specs.append(pl.BlockSpec((rows // n_steps, cols), lambda *g: (step_index(*g), 0)))
        shapes.append(jax.ShapeDtypeStruct(w.shape, BF16))
    return specs, shapes


def _proj(x2, pos2, invf2, w_in_b, cast_weights):
    s, d = x2.shape
    tm, tn = PROJ_TM, PROJ_TN
    n_i, n_j = s // tm, IN_WIDTH // tn
    n_rope_blocks = (2 * ATTN_WIDTH) // tn
    n_f32_blocks = MAIN_WIDTH // tn
    cast_specs, cast_shapes = _slab_specs(cast_weights, n_i * n_j, lambda i, j: i * n_j + j)
    main_spec = pl.BlockSpec((tm, tn), lambda i, j: (i, jnp.minimum(j, n_f32_blocks - 1)))
    gate_spec = pl.BlockSpec((tm, tn), lambda i, j: (i, jnp.maximum(j - n_f32_blocks, 0)))
    outs = pl.pallas_call(
        functools.partial(_proj_kernel, n_rope_blocks=n_rope_blocks, n_f32_blocks=n_f32_blocks,
                          n_cast=len(cast_weights)),
        grid=(n_i, n_j),
        in_specs=[
            pl.BlockSpec((tm, d), lambda i, j: (i, 0)),
            pl.BlockSpec((tm, 1), lambda i, j: (i, 0)),
            pl.BlockSpec((1, HEAD_DIM), lambda i, j: (0, 0)),
            pl.BlockSpec((d, tn), lambda i, j: (0, j)),
        ] + cast_specs,
        out_specs=[main_spec, gate_spec] + cast_specs,
        out_shape=[jax.ShapeDtypeStruct((s, MAIN_WIDTH), F32),
                   jax.ShapeDtypeStruct((s, IN_WIDTH - MAIN_WIDTH), BF16)] + cast_shapes,
        scratch_shapes=[
            pltpu.VMEM((tm, d), BF16),
            pltpu.VMEM((tm, HEAD_DIM), F32),
            pltpu.VMEM((tm, HEAD_DIM), F32),
        ],
        compiler_params=pltpu.CompilerParams(
            dimension_semantics=("arbitrary", "arbitrary"), vmem_limit_bytes=VMEM_LIMIT_BYTES),
        name="proj_rope",
    )(x2, pos2, invf2, w_in_b, *cast_weights)
    return outs[0], outs[1], outs[2:]


def _attn_kernel(q_ref, kc_ref, kp_ref, vc_ref, vp_ref, o_ref, bias_ref,
                 qd_refs, kd_refs, vd_refs, o_refs, lse_refs, stage_refs):
    n = pl.program_id(0)
    blk = SUB_BLOCK

    @pl.when(jnp.logical_and(n == 0, pl.program_id(1) == 0))
    def _():
        for vd_ref in vd_refs:
            vd_ref[:, HEAD_DIM:] = jnp.ones((vd_ref.shape[0], HEAD_DIM), BF16)

    names = ("q", "kc", "kp", "vc", "vp")
    src = dict(zip(names, (q_ref, kc_ref, kp_ref, vc_ref, vp_ref)))
    src_dil = 1
    stage = dict(zip(names, stage_refs))

    qi = lax.broadcasted_iota(jnp.int32, (blk, 2 * blk), 0)
    kj = lax.broadcasted_iota(jnp.int32, (blk, 2 * blk), 1)
    band = (kj >= qi) & (kj <= qi + blk)
    bias_ref[0] = jnp.where(band, 0.0, MASK_VALUE).astype(F32)
    bias_ref[1] = jnp.where(band & (kj >= blk), 0.0, MASK_VALUE).astype(F32)
    first_chunk = (n == 0).astype(jnp.int32)

    for p, dil in enumerate(DILATIONS):
        m_len = ATTN_CHUNK // dil
        nb = m_len // blk
        krows = blk + m_len
        qd_ref, kd_ref, vd_ref = qd_refs[p], kd_refs[p], vd_refs[p]

        f = dil // src_dil
        src_len = ATTN_CHUNK // src_dil
        keep_f32 = f > 1 and p + 1 < len(DILATIONS)

        def seg(name, r, m0, rows, f=f, src=src, src_dil=src_dil, src_len=src_len, m_len=m_len):
            if f == 1:
                return src[name][pl.ds(r * m_len + m0, rows), :]
            row0 = (r % src_dil) * src_len + r // src_dil + f * m0
            return src[name][pl.ds(row0, rows, stride=f), :]

        for r in range(dil):
            for name, dst, dst_rows in (("q", qd_ref, m_len), ("k", kd_ref, krows), ("v", vd_ref, krows)):
                if name == "q":
                    cur = seg("q", r, 0, m_len)
                    if keep_f32:
                        stage["q"][r * m_len:(r + 1) * m_len, :] = cur
                    dst[r * m_len:(r + 1) * m_len, :] = cur.astype(BF16)
                    continue
                if keep_f32:
                    prev_full = seg(name + "p", r, 0, m_len)
                    stage[name + "p"][r * m_len:(r + 1) * m_len, :] = prev_full
                    prev_tail = prev_full[m_len - blk:, :]
                else:
                    prev_tail = seg(name + "p", r, m_len - blk, blk)
                cur = seg(name + "c", r, 0, m_len)
                if keep_f32:
                    stage[name + "c"][r * m_len:(r + 1) * m_len, :] = cur
                lanes = slice(0, HEAD_DIM) if name == "v" else slice(None)
                dst[r * dst_rows:r * dst_rows + blk, lanes] = prev_tail.astype(BF16)
                dst[r * dst_rows + blk:(r + 1) * dst_rows, lanes] = cur.astype(BF16)
        if keep_f32:
            src, src_dil = stage, dil

        for r in range(dil):
            for b in range(nb):
                q0 = r * m_len + b * blk
                k0 = r * krows + b * blk
                qb = qd_ref[q0:q0 + blk, :]
                kb = kd_ref[k0:k0 + 2 * blk, :]
                vb = vd_ref[k0:k0 + 2 * blk, :]
                s = lax.dot_general(qb, kb, (((1,), (1,)), ((), ())), preferred_element_type=F32)
                s = s + (bias_ref[first_chunk] if b == 0 else bias_ref[0])
                m_raw = jnp.max(s, axis=-1, keepdims=True)
                pr = jnp.exp2((s - m_raw) * (SM_SCALE * LOG2_E))
                ext = jnp.dot(pr.astype(BF16), vb, preferred_element_type=F32)
                acc, l = ext[:, :HEAD_DIM], ext[:, HEAD_DIM:]
                rows = pl.ds(b * (blk * dil) + r, blk, stride=dil)
                o_refs[p][rows, :] = acc / l
                lse_refs[p][rows, :] = m_raw * SM_SCALE + jnp.log(l)

    for c in range(ATTN_CHUNK // blk):
        rows = slice(c * blk, (c + 1) * blk)
        lse = [ref[rows, :] for ref in lse_refs]
        top = functools.reduce(jnp.maximum, lse)
        w = [jnp.exp(x - top) for x in lse]
        num = sum(wp * ref[rows, :] for wp, ref in zip(w, o_refs))
        o_ref[rows, :] = (num / sum(w)).astype(o_ref.dtype)


def _attn(h, cast_weights):
    s = h.shape[0]
    c = ATTN_CHUNK
    blk = SUB_BLOCK
    npat = len(DILATIONS)
    n_cast = len(cast_weights)
    n_chunks = s // c
    chunk = lambda col0: pl.BlockSpec((c, HEAD_DIM), lambda n, hh: (n, col0 + hh))
    prev = lambda col0: pl.BlockSpec((c, HEAD_DIM), lambda n, hh: (jnp.maximum(n - 1, 0), col0 + hh))
    cast_specs, cast_shapes = _slab_specs(cast_weights, n_chunks * ATTN_HEADS, lambda n, hh: n * ATTN_HEADS + hh)

    def body(q_ref, kc_ref, kp_ref, vc_ref, vp_ref, *rest):
        cast_in, o_ref, cast_out = rest[:n_cast], rest[n_cast], rest[n_cast + 1:2 * n_cast + 1]
        bias, scr = rest[2 * n_cast + 1], rest[2 * n_cast + 2:]
        for src, dst in zip(cast_in, cast_out):
            dst[...] = src[...].astype(BF16)
        groups = [scr[i * npat:(i + 1) * npat] for i in range(5)]
        _attn_kernel(q_ref, kc_ref, kp_ref, vc_ref, vp_ref, o_ref, bias, *groups, scr[5 * npat:])

    k_scratch = [pltpu.VMEM((dil * blk + c, HEAD_DIM), BF16) for dil in DILATIONS]
    v_scratch = [pltpu.VMEM((dil * blk + c, 2 * HEAD_DIM), BF16) for dil in DILATIONS]
    outs = pl.pallas_call(
        body,
        grid=(n_chunks, ATTN_HEADS),
        in_specs=[chunk(0), chunk(ATTN_HEADS), prev(ATTN_HEADS), chunk(2 * ATTN_HEADS),
                  prev(2 * ATTN_HEADS)] + cast_specs,
        out_specs=[pl.BlockSpec((c, HEAD_DIM), lambda n, hh: (n, hh))] + cast_specs,
        out_shape=[jax.ShapeDtypeStruct((s, ATTN_WIDTH), BF16)] + cast_shapes,
        scratch_shapes=(
            [pltpu.VMEM((2, blk, 2 * blk), F32)]
            + [pltpu.VMEM((c, HEAD_DIM), BF16)] * npat
            + k_scratch + v_scratch
            + [pltpu.VMEM((c, HEAD_DIM), F32)] * (2 * npat)
            + [pltpu.VMEM((c, HEAD_DIM), F32)] * 5),
        compiler_params=pltpu.CompilerParams(
            dimension_semantics=("arbitrary", "arbitrary"), vmem_limit_bytes=VMEM_LIMIT_BYTES),
        name="dilated_attn",
    )(h, h, h, h, h, *cast_weights)
    return outs[0], outs[1:]


def _mix_kernel(oat_ref, u_ref, uh_ref, ga_ref, gp_ref, x_ref, wpool_ref, pscale_ref,
                wba_ref, wbb_ref, wout_ref, g_ref, b_ref, out_ref, ubuf_ref, pm_ref, mg_ref):
    i = pl.program_id(0)
    tm = u_ref.shape[0]
    halo = MAX_POOL_WINDOW

    @pl.when(i == 0)
    def _():
        ubuf_ref[0:halo, :] = jnp.zeros((halo, POOL_WIDTH), F32)

    @pl.when(i > 0)
    def _():
        ubuf_ref[0:halo, :] = uh_ref[...]

    ubuf_ref[halo:halo + tm, :] = u_ref[...]

    t_glob = i * tm + lax.broadcasted_iota(jnp.int32, (tm, 1), 0)
    for g, w in enumerate(POOL_WINDOWS):
        cols = slice(g * POOL_GROUP_WIDTH, (g + 1) * POOL_GROUP_WIDTH)
        ug = u_ref[:, cols]
        wsum = ug
        for jj in range(1, w):
            wsum = wsum + ubuf_ref[halo - jj:halo - jj + tm, cols]
        count = jnp.minimum(t_glob + 1, w).astype(F32)
        pooled = wsum / count - ug
        y = jnp.dot(pooled.astype(BF16), wpool_ref[g], preferred_element_type=F32)
        pm_ref[:, cols] = (y * pscale_ref[:, cols]).astype(BF16)

    for c in range(D_MODEL // MIX_TN):
        cols = slice(c * MIX_TN, (c + 1) * MIX_TN)
        y_attn = jnp.dot(oat_ref[...], wba_ref[:, cols], preferred_element_type=F32)
        y_pool = jnp.dot(pm_ref[...], wbb_ref[:, cols], preferred_element_type=F32)
        merged = (jax.nn.sigmoid(ga_ref[:, cols].astype(F32)) * y_attn
                  + jax.nn.sigmoid(gp_ref[:, cols].astype(F32)) * y_pool)
        mg_ref[:, cols] = merged.astype(BF16)

    mix = jnp.dot(mg_ref[...], wout_ref[...], preferred_element_type=F32)
    out_ref[...] = _layer_norm(DEEPNORM_ALPHA * x_ref[...] + mix, g_ref[...], b_ref[...])


def _mix(o_attn, h_main, h_gates, x2, w_pool_b, pool_scale, w_ba_b, w_bb_b, w_out_b, ln_g, ln_b):
    s, d = x2.shape
    tm = MIX_TM
    halo = MAX_POOL_WINDOW
    u_col = (3 * ATTN_WIDTH) // POOL_WIDTH
    resident = lambda shape: pl.BlockSpec(shape, lambda i: (0,) * len(shape), pipeline_mode=pl.Buffered(1))
    return pl.pallas_call(
        _mix_kernel,
        grid=(s // tm,),
        in_specs=[
            pl.BlockSpec((tm, ATTN_WIDTH), lambda i: (i, 0)),
            pl.BlockSpec((tm, POOL_WIDTH), lambda i: (i, u_col)),
            pl.BlockSpec((halo, POOL_WIDTH), lambda i: (jnp.maximum(i * (tm // halo) - 1, 0), u_col)),
            pl.BlockSpec((tm, d), lambda i: (i, 0)),
            pl.BlockSpec((tm, d), lambda i: (i, 1)),
            pl.BlockSpec((tm, d), lambda i: (i, 0)),
            resident(w_pool_b.shape),
            resident((1, POOL_WIDTH)),
            resident(w_ba_b.shape),
            resident(w_bb_b.shape),
            resident(w_out_b.shape),
            resident((1, d)),
            resident((1, d)),
        ],
        out_specs=pl.BlockSpec((tm, d), lambda i: (i, 0)),
        out_shape=jax.ShapeDtypeStruct((s, d), F32),
        scratch_shapes=[
            pltpu.VMEM((halo + tm, POOL_WIDTH), F32),
            pltpu.VMEM((tm, POOL_WIDTH), BF16),
            pltpu.VMEM((tm, d), BF16),
        ],
        compiler_params=pltpu.CompilerParams(
            dimension_semantics=("arbitrary",), vmem_limit_bytes=VMEM_LIMIT_BYTES),
        name="mix_ln",
    )(o_attn, h_main, h_main, h_gates, h_gates, x2, w_pool_b, pool_scale, w_ba_b, w_bb_b, w_out_b, ln_g, ln_b)


def _ffn_kernel(x_ref, w1_ref, w2_ref, g_ref, b_ref, out_ref, xb_ref):
    j = pl.program_id(1)

    @pl.when(j == 0)
    def _():
        xb_ref[...] = x_ref[...].astype(BF16)
        out_ref[...] = jnp.zeros(out_ref.shape, F32)

    hid = jnp.dot(xb_ref[...], w1_ref[...], preferred_element_type=F32)
    hid = jnp.square(jnp.maximum(hid, 0.0)).astype(BF16)
    out_ref[...] += jnp.dot(hid, w2_ref[...], preferred_element_type=F32)

    @pl.when(j == pl.num_programs(1) - 1)
    def _():
        _residual_layer_norm(x_ref, out_ref, g_ref, b_ref, out_ref, FFN_LN_ROWS)


def _ffn(x1, w1_b, w2_b, ln_g, ln_b):
    s, d = x1.shape
    tm, tf = FFN_TM, FFN_TF
    return pl.pallas_call(
        _ffn_kernel,
        grid=(s // tm, D_FF // tf),
        in_specs=[
            pl.BlockSpec((tm, d), lambda i, j: (i, 0)),
            pl.BlockSpec((d, tf), lambda i, j: (0, j)),
            pl.BlockSpec((tf, d), lambda i, j: (j, 0)),
            pl.BlockSpec((1, d), lambda i, j: (0, 0)),
            pl.BlockSpec((1, d), lambda i, j: (0, 0)),
        ],
        out_specs=pl.BlockSpec((tm, d), lambda i, j: (i, 0)),
        out_shape=jax.ShapeDtypeStruct((s, d), F32),
        scratch_shapes=[pltpu.VMEM((tm, d), BF16)],
        compiler_params=pltpu.CompilerParams(
            dimension_semantics=("arbitrary", "arbitrary"), vmem_limit_bytes=VMEM_LIMIT_BYTES),
        name="ffn_ln",
    )(x1, w1_b, w2_b, ln_g, ln_b)


def kernel(x, positions, w_in, w_pool, pool_scale, w_branch_attn, w_branch_pool, w_out,
           ln_mix_g, ln_mix_b, w_ff1, w_ff2, ln_ff_g, ln_ff_b):
    b, s, d = x.shape
    assert (b, s, d) == (1, SEQ, D_MODEL) and w_in.shape[0] == DEPTH
    half = HEAD_DIM // 2
    inv_freq = ROPE_THETA ** (-jnp.arange(half, dtype=F32) / half)
    invf2 = jnp.concatenate([inv_freq, inv_freq]).reshape(1, HEAD_DIM)
    x2 = x.reshape(s, d)
    pos2 = positions.reshape(s, 1)
    for layer in range(DEPTH):
        w_pool2 = w_pool[layer].reshape(POOL_WIDTH, POOL_GROUP_WIDTH)
        h_main, h_gates, (w_ba_b, w_bb_b, w_out_b, w_pool_b) = _proj(
            x2, pos2, invf2, w_in[layer].astype(BF16),
            [w_branch_attn[layer], w_branch_pool[layer], w_out[layer], w_pool2])
        o_attn, (w1_b, w2_b) = _attn(h_main, [w_ff1[layer], w_ff2[layer]])
        x2 = _mix(o_attn, h_main, h_gates, x2, w_pool_b.reshape(w_pool[layer].shape),
                  pool_scale[layer].reshape(1, POOL_WIDTH), w_ba_b, w_bb_b, w_out_b,
                  ln_mix_g[layer].reshape(1, d), ln_mix_b[layer].reshape(1, d))
        x2 = _ffn(x2, w1_b, w2_b, ln_ff_g[layer].reshape(1, d), ln_ff_b[layer].reshape(1, d))
    return x2.reshape(b, s, d)
```

```python
import functools

import jax
import jax.numpy as jnp
from jax import lax
from jax.experimental import pallas as pl
from jax.experimental.pallas import tpu as pltpu

F32 = jnp.float32
BF16 = jnp.bfloat16

D_MODEL = 2048
SEQ = 8192
HEAD_DIM = 128
ATTN_WIDTH = D_MODEL // 2
ATTN_HEADS = ATTN_WIDTH // HEAD_DIM
POOL_WIDTH = D_MODEL // 2
POOL_WINDOWS = (2, 4, 8, 16)
POOL_GROUP_WIDTH = POOL_WIDTH // len(POOL_WINDOWS)
MAX_POOL_WINDOW = max(POOL_WINDOWS)
POOL_PAD = 8
DILATIONS = (1, 4, 16)
SUB_BLOCK = 128
D_FF = 4 * D_MODEL
IN_WIDTH = 3 * ATTN_WIDTH + POOL_WIDTH + 2 * D_MODEL
MAIN_WIDTH = 3 * ATTN_WIDTH + POOL_WIDTH
ROPE_THETA = 10000.0
LN_EPS = 1e-5
DEPTH = 1
DEEPNORM_ALPHA = (2.0 * DEPTH) ** 0.25
SM_SCALE = HEAD_DIM ** -0.5
LOG2_E = 1.4426950408889634
Q_SCALE = SM_SCALE * LOG2_E
MASK_VALUE = -1e30

VMEM_LIMIT_BYTES = 60 * 1024 * 1024

ATTN_CHUNK = max(DILATIONS) * SUB_BLOCK
PROJ_TM, PROJ_TN = 1024, 1024
MIX_TM, MIX_TN = 512, 512
FFN_TM, FFN_TF = 1024, 1024
FFN_LN_ROWS = 128


def _layer_norm(y, g, b):
    mu = jnp.mean(y, axis=-1, keepdims=True)
    yc = y - mu
    var = jnp.mean(yc * yc, axis=-1, keepdims=True)
    return yc * lax.rsqrt(var + LN_EPS) * g + b


def _residual_layer_norm(x_ref, y_ref, g_ref, b_ref, out_ref, chunk_rows):
    g = g_ref[...]
    b = b_ref[...]
    n_chunks = x_ref.shape[0] // chunk_rows
    if n_chunks == 1:
        out_ref[...] = _layer_norm(DEEPNORM_ALPHA * x_ref[...] + y_ref[...], g, b)
        return

    def body(c, carry):
        rows = pl.ds(pl.multiple_of(c * chunk_rows, chunk_rows), chunk_rows)
        out_ref[rows, :] = _layer_norm(DEEPNORM_ALPHA * x_ref[rows, :] + y_ref[rows, :], g, b)
        return carry

    lax.fori_loop(0, n_chunks, body, 0)


def _proj_kernel(x_ref, pos_ref, invf_ref, w_ref, *rest, n_q_blocks, n_rope_blocks, n_f32_blocks, n_cast):
    cast_in, (main_ref, gate_ref) = rest[:n_cast], rest[n_cast:n_cast + 2]
    cast_out = rest[n_cast + 2:2 * n_cast + 2]
    xb_ref, cos_ref, sin_ref = rest[2 * n_cast + 2:]
    j = pl.program_id(1)
    tm = x_ref.shape[0]
    half = HEAD_DIM // 2

    @pl.when(j == 0)
    def _():
        xb_ref[...] = x_ref[...].astype(BF16)
        lane = lax.broadcasted_iota(jnp.int32, (1, HEAD_DIM), 1)
        low = lane < half
        pos = jnp.where(low, pos_ref[0:tm // 2, :], pos_ref[tm // 2:tm, :])
        ang = pos.astype(F32) * invf_ref[...]
        sign = jnp.where(low, -1.0, 1.0).astype(F32)
        for table_ref, tab in ((cos_ref, jnp.cos(ang)), (sin_ref, jnp.sin(ang))):
            swapped = pltpu.roll(tab, half, 1)
            scale = sign if table_ref is sin_ref else 1.0
            top = jnp.where(low, tab, swapped) * scale
            bottom = jnp.where(low, swapped, tab) * scale
            table_ref[1, 0:tm // 2, :] = top
            table_ref[1, tm // 2:tm, :] = bottom
            table_ref[0, 0:tm // 2, :] = top * Q_SCALE
            table_ref[0, tm // 2:tm, :] = bottom * Q_SCALE

    for src, dst in zip(cast_in, cast_out):
        dst[...] = src[...].astype(BF16)

    def project():
        return jnp.dot(xb_ref[...], w_ref[...], preferred_element_type=F32)

    @pl.when(j < n_rope_blocks)
    def _():
        acc = project()
        which = (j >= n_q_blocks).astype(jnp.int32)
        cos = cos_ref[which]
        sin = sin_ref[which]
        for hh in range(acc.shape[1] // HEAD_DIM):
            cols = slice(hh * HEAD_DIM, (hh + 1) * HEAD_DIM)
            t = acc[:, cols]
            main_ref[:, cols] = t * cos + pltpu.roll(t, half, 1) * sin

    @pl.when(jnp.logical_and(j >= n_rope_blocks, j < n_f32_blocks))
    def _():
        main_ref[...] = project()

    @pl.when(j >= n_f32_blocks)
    def _():
        gate_ref[...] = project().astype(BF16)


def _slab_specs(weights, n_steps, step_index):
    specs, shapes = [], []
    for w in weights:
        rows, cols = w.shape
        specs.append(pl.BlockSpec((rows // n_steps, cols), lambda *g: (step_index(*g), 0)))
        shapes.append(jax.ShapeDtypeStruct(w.shape, BF16))
    return specs, shapes


def _proj(x2, pos2, invf2, w_in_b, cast_weights):
    s, d = x2.shape
    tm, tn = PROJ_TM, PROJ_TN
    n_i, n_j = s // tm, IN_WIDTH // tn
    n_rope_blocks = (2 * ATTN_WIDTH) // tn
    n_f32_blocks = MAIN_WIDTH // tn
    cast_specs, cast_shapes = _slab_specs(cast_weights, n_i * n_j, lambda i, j: i * n_j + j)
    main_spec = pl.BlockSpec((tm, tn), lambda i, j: (i, jnp.minimum(j, n_f32_blocks - 1)))
    gate_spec = pl.BlockSpec((tm, tn), lambda i, j: (i, jnp.maximum(j - n_f32_blocks, 0)))
    outs = pl.pallas_call(
        functools.partial(_proj_kernel, n_q_blocks=ATTN_WIDTH // tn, n_rope_blocks=n_rope_blocks,
                          n_f32_blocks=n_f32_blocks, n_cast=len(cast_weights)),
        grid=(n_i, n_j),
        in_specs=[
            pl.BlockSpec((tm, d), lambda i, j: (i, 0)),
            pl.BlockSpec((tm, 1), lambda i, j: (i, 0)),
            pl.BlockSpec((1, HEAD_DIM), lambda i, j: (0, 0)),
            pl.BlockSpec((d, tn), lambda i, j: (0, j)),
        ] + cast_specs,
        out_specs=[main_spec, gate_spec] + cast_specs,
        out_shape=[jax.ShapeDtypeStruct((s, MAIN_WIDTH), F32),
                   jax.ShapeDtypeStruct((s, IN_WIDTH - MAIN_WIDTH), BF16)] + cast_shapes,
        scratch_shapes=[
            pltpu.VMEM((tm, d), BF16),
            pltpu.VMEM((2, tm, HEAD_DIM), F32),
            pltpu.VMEM((2, tm, HEAD_DIM), F32),
        ],
        compiler_params=pltpu.CompilerParams(
            dimension_semantics=("arbitrary", "arbitrary"), vmem_limit_bytes=VMEM_LIMIT_BYTES),
        name="proj_rope",
    )(x2, pos2, invf2, w_in_b, *cast_weights)
    return outs[0], outs[1], outs[2:]


def _attn_kernel(q_ref, kc_ref, kp_ref, vc_ref, vp_ref, o_ref, bias_ref,
                 qd_refs, kd_refs, vd_refs, o_refs, lse_refs, stage_refs):
    n = pl.program_id(0)
    blk = SUB_BLOCK

    @pl.when(jnp.logical_and(n == 0, pl.program_id(1) == 0))
    def _():
        for vd_ref in vd_refs:
            vd_ref[:, HEAD_DIM:] = jnp.ones((vd_ref.shape[0], HEAD_DIM), BF16)

    names = ("q", "kc", "kp", "vc", "vp")
    src = dict(zip(names, (q_ref, kc_ref, kp_ref, vc_ref, vp_ref)))
    src_dil = 1
    stage = dict(zip(names, stage_refs))

    qi = lax.broadcasted_iota(jnp.int32, (blk, 2 * blk), 0)
    kj = lax.broadcasted_iota(jnp.int32, (blk, 2 * blk), 1)
    band = (kj >= qi) & (kj <= qi + blk)
    bias_ref[0] = jnp.where(band, 0.0, MASK_VALUE).astype(F32)
    bias_ref[1] = jnp.where(band & (kj >= blk), 0.0, MASK_VALUE).astype(F32)
    first_chunk = (n == 0).astype(jnp.int32)

    for p, dil in enumerate(DILATIONS):
        m_len = ATTN_CHUNK // dil
        nb = m_len // blk
        krows = blk + m_len
        qd_ref, kd_ref, vd_ref = qd_refs[p], kd_refs[p], vd_refs[p]

        f = dil // src_dil
        src_len = ATTN_CHUNK // src_dil
        keep_f32 = f > 1 and p + 1 < len(DILATIONS)

        def seg(name, r, m0, rows, f=f, src=src, src_dil=src_dil, src_len=src_len, m_len=m_len):
            if f == 1:
                return src[name][pl.ds(r * m_len + m0, rows), :]
            row0 = (r % src_dil) * src_len + r // src_dil + f * m0
            return src[name][pl.ds(row0, rows, stride=f), :]

        for r in range(dil):
            for name, dst, dst_rows in (("q", qd_ref, m_len), ("k", kd_ref, krows), ("v", vd_ref, krows)):
                if name == "q":
                    cur = seg("q", r, 0, m_len)
                    if keep_f32:
                        stage["q"][r * m_len:(r + 1) * m_len, :] = cur
                    dst[r * m_len:(r + 1) * m_len, :] = cur.astype(BF16)
                    continue
                if keep_f32:
                    prev_full = seg(name + "p", r, 0, m_len)
                    stage[name + "p"][r * m_len:(r + 1) * m_len, :] = prev_full
                    prev_tail = prev_full[m_len - blk:, :]
                else:
                    prev_tail = seg(name + "p", r, m_len - blk, blk)
                cur = seg(name + "c", r, 0, m_len)
                if keep_f32:
                    stage[name + "c"][r * m_len:(r + 1) * m_len, :] = cur
                lanes = slice(0, HEAD_DIM) if name == "v" else slice(None)
                dst[r * dst_rows:r * dst_rows + blk, lanes] = prev_tail.astype(BF16)
                dst[r * dst_rows + blk:(r + 1) * dst_rows, lanes] = cur.astype(BF16)
        if keep_f32:
            src, src_dil = stage, dil

        for r in range(dil):
            for b in range(nb):
                q0 = r * m_len + b * blk
                k0 = r * krows + b * blk
                qb = qd_ref[q0:q0 + blk, :]
                kb = kd_ref[k0:k0 + 2 * blk, :]
                vb = vd_ref[k0:k0 + 2 * blk, :]
                s = lax.dot_general(qb, kb, (((1,), (1,)), ((), ())), preferred_element_type=F32)
                s = s + (bias_ref[first_chunk] if b == 0 else bias_ref[0])
                m = jnp.max(s, axis=-1, keepdims=True)
                pr = jnp.exp2(s - m)
                ext = jnp.dot(pr.astype(BF16), vb, preferred_element_type=F32)
                acc, l = ext[:, :HEAD_DIM], ext[:, HEAD_DIM:]
                rows = pl.ds(b * (blk * dil) + r, blk, stride=dil)
                o_refs[p][rows, :] = acc / l
                lse_refs[p][rows, :] = m + jnp.log2(l)

    for c in range(ATTN_CHUNK // blk):
        rows = slice(c * blk, (c + 1) * blk)
        lse = [ref[rows, :] for ref in lse_refs]
        top = functools.reduce(jnp.maximum, lse)
        w = [jnp.exp2(x - top) for x in lse]
        num = sum(wp * ref[rows, :] for wp, ref in zip(w, o_refs))
        o_ref[rows, :] = (num / sum(w)).astype(o_ref.dtype)


def _attn(h, cast_weights):
    s = h.shape[0]
    c = ATTN_CHUNK
    blk = SUB_BLOCK
    npat = len(DILATIONS)
    n_cast = len(cast_weights)
    n_chunks = s // c
    chunk = lambda col0: pl.BlockSpec((c, HEAD_DIM), lambda n, hh: (n, col0 + hh))
    prev = lambda col0: pl.BlockSpec((c, HEAD_DIM), lambda n, hh: (jnp.maximum(n - 1, 0), col0 + hh))
    cast_specs, cast_shapes = _slab_specs(cast_weights, n_chunks * ATTN_HEADS, lambda n, hh: n * ATTN_HEADS + hh)

    def body(q_ref, kc_ref, kp_ref, vc_ref, vp_ref, *rest):
        cast_in, o_ref, cast_out = rest[:n_cast], rest[n_cast], rest[n_cast + 1:2 * n_cast + 1]
        bias, scr = rest[2 * n_cast + 1], rest[2 * n_cast + 2:]
        for src, dst in zip(cast_in, cast_out):
            dst[...] = src[...].astype(BF16)
        groups = [scr[i * npat:(i + 1) * npat] for i in range(5)]
        _attn_kernel(q_ref, kc_ref, kp_ref, vc_ref, vp_ref, o_ref, bias, *groups, scr[5 * npat:])

    k_scratch = [pltpu.VMEM((dil * blk + c, HEAD_DIM), BF16) for dil in DILATIONS]
    v_scratch = [pltpu.VMEM((dil * blk + c, 2 * HEAD_DIM), BF16) for dil in DILATIONS]
    outs = pl.pallas_call(
        body,
        grid=(n_chunks, ATTN_HEADS),
        in_specs=[chunk(0), chunk(ATTN_HEADS), prev(ATTN_HEADS), chunk(2 * ATTN_HEADS),
                  prev(2 * ATTN_HEADS)] + cast_specs,
        out_specs=[pl.BlockSpec((c, HEAD_DIM), lambda n, hh: (n, hh))] + cast_specs,
        out_shape=[jax.ShapeDtypeStruct((s, ATTN_WIDTH), BF16)] + cast_shapes,
        scratch_shapes=(
            [pltpu.VMEM((2, blk, 2 * blk), F32)]
            + [pltpu.VMEM((c, HEAD_DIM), BF16)] * npat
            + k_scratch + v_scratch
            + [pltpu.VMEM((c, HEAD_DIM), F32)] * (2 * npat)
            + [pltpu.VMEM((c, HEAD_DIM), F32)] * 5),
        compiler_params=pltpu.CompilerParams(
            dimension_semantics=("arbitrary", "arbitrary"), vmem_limit_bytes=VMEM_LIMIT_BYTES),
        name="dilated_attn",
    )(h, h, h, h, h, *cast_weights)
    return outs[0], outs[1:]


def _mix_kernel(oat_ref, u_ref, uh_ref, ga_ref, gp_ref, x_ref, wpool_ref, pscale_ref,
                wba_ref, wbb_ref, wout_ref, g_ref, b_ref, out_ref, ubuf_ref, pm_ref, mg_ref, *lv_refs):
    i = pl.program_id(0)
    tm = u_ref.shape[0]
    halo = MAX_POOL_WINDOW

    pad = POOL_PAD
    top = pad + halo
    rows_all = top + tm

    @pl.when(i == 0)
    def _():
        ubuf_ref[0:top, :] = jnp.zeros((top, POOL_WIDTH), F32)
        for lv_ref in lv_refs:
            lv_ref[0:pad, :] = jnp.zeros((pad, POOL_GROUP_WIDTH), F32)

    @pl.when(i > 0)
    def _():
        ubuf_ref[pad:top, :] = uh_ref[...]

    ubuf_ref[top:rows_all, :] = u_ref[...]

    t_glob = i * tm + lax.broadcasted_iota(jnp.int32, (tm, 1), 0)
    for g, w in enumerate(POOL_WINDOWS):
        cols = slice(g * POOL_GROUP_WIDTH, (g + 1) * POOL_GROUP_WIDTH)
        cur, cur_cols, shift, level = ubuf_ref, cols, 1, 0
        while shift < w:
            wsum = cur[pad:rows_all, cur_cols] + cur[pad - shift:rows_all - shift, cur_cols]
            shift *= 2
            if shift < w:
                lv_refs[level % 2][pad:rows_all, :] = wsum
                cur, cur_cols, level = lv_refs[level % 2], slice(None), level + 1
        inv_count = 1.0 / jnp.minimum(t_glob + 1, w).astype(F32)
        pooled = wsum[halo:, :] * inv_count - u_ref[:, cols]
        y = jnp.dot(pooled.astype(BF16), wpool_ref[g], preferred_element_type=F32)
        pm_ref[:, cols] = (y * pscale_ref[:, cols]).astype(BF16)

    for c in range(D_MODEL // MIX_TN):
        cols = slice(c * MIX_TN, (c + 1) * MIX_TN)
        y_attn = jnp.dot(oat_ref[...], wba_ref[:, cols], preferred_element_type=F32)
        y_pool = jnp.dot(pm_ref[...], wbb_ref[:, cols], preferred_element_type=F32)
        merged = (jax.nn.sigmoid(ga_ref[:, cols].astype(F32)) * y_attn
                  + jax.nn.sigmoid(gp_ref[:, cols].astype(F32)) * y_pool)
        mg_ref[:, cols] = merged.astype(BF16)

    mix = jnp.dot(mg_ref[...], wout_ref[...], preferred_element_type=F32)
    out_ref[...] = _layer_norm(DEEPNORM_ALPHA * x_ref[...] + mix, g_ref[...], b_ref[...])


def _mix(o_attn, h_main, h_gates, x2, w_pool_b, pool_scale, w_ba_b, w_bb_b, w_out_b, ln_g, ln_b):
    s, d = x2.shape
    tm = MIX_TM
    halo = MAX_POOL_WINDOW
    u_col = (3 * ATTN_WIDTH) // POOL_WIDTH
    resident = lambda shape: pl.BlockSpec(shape, lambda i: (0,) * len(shape), pipeline_mode=pl.Buffered(1))
    return pl.pallas_call(
        _mix_kernel,
        grid=(s // tm,),
        in_specs=[
            pl.BlockSpec((tm, ATTN_WIDTH), lambda i: (i, 0)),
            pl.BlockSpec((tm, POOL_WIDTH), lambda i: (i, u_col)),
            pl.BlockSpec((halo, POOL_WIDTH), lambda i: (jnp.maximum(i * (tm // halo) - 1, 0), u_col)),
            pl.BlockSpec((tm, d), lambda i: (i, 0)),
            pl.BlockSpec((tm, d), lambda i: (i, 1)),
            pl.BlockSpec((tm, d), lambda i: (i, 0)),
            resident(w_pool_b.shape),
            resident((1, POOL_WIDTH)),
            resident(w_ba_b.shape),
            resident(w_bb_b.shape),
            resident(w_out_b.shape),
            resident((1, d)),
            resident((1, d)),
        ],
        out_specs=pl.BlockSpec((tm, d), lambda i: (i, 0)),
        out_shape=jax.ShapeDtypeStruct((s, d), F32),
        scratch_shapes=[
            pltpu.VMEM((POOL_PAD + halo + tm, POOL_WIDTH), F32),
            pltpu.VMEM((tm, POOL_WIDTH), BF16),
            pltpu.VMEM((tm, d), BF16),
            pltpu.VMEM((POOL_PAD + halo + tm, POOL_GROUP_WIDTH), F32),
            pltpu.VMEM((POOL_PAD + halo + tm, POOL_GROUP_WIDTH), F32),
        ],
        compiler_params=pltpu.CompilerParams(
            dimension_semantics=("arbitrary",), vmem_limit_bytes=VMEM_LIMIT_BYTES),
        name="mix_ln",
    )(o_attn, h_main, h_main, h_gates, h_gates, x2, w_pool_b, pool_scale, w_ba_b, w_bb_b, w_out_b, ln_g, ln_b)


def _ffn_kernel(x_ref, w1_ref, w2_ref, g_ref, b_ref, out_ref, xb_ref):
    j = pl.program_id(1)

    @pl.when(j == 0)
    def _():
        xb_ref[...] = x_ref[...].astype(BF16)
        out_ref[...] = jnp.zeros(out_ref.shape, F32)

    hid = jnp.dot(xb_ref[...], w1_ref[...], preferred_element_type=F32)
    hid = jnp.square(jnp.maximum(hid, 0.0)).astype(BF16)
    out_ref[...] += jnp.dot(hid, w2_ref[...], preferred_element_type=F32)

    @pl.when(j == pl.num_programs(1) - 1)
    def _():
        _residual_layer_norm(x_ref, out_ref, g_ref, b_ref, out_ref, FFN_LN_ROWS)


def _ffn(x1, w1_b, w2_b, ln_g, ln_b):
    s, d = x1.shape
    tm, tf = FFN_TM, FFN_TF
    return pl.pallas_call(
        _ffn_kernel,
        grid=(s // tm, D_FF // tf),
        in_specs=[
            pl.BlockSpec((tm, d), lambda i, j: (i, 0)),
            pl.BlockSpec((d, tf), lambda i, j: (0, j)),
            pl.BlockSpec((tf, d), lambda i, j: (j, 0)),
            pl.BlockSpec((1, d), lambda i, j: (0, 0)),
            pl.BlockSpec((1, d), lambda i, j: (0, 0)),
        ],
        out_specs=pl.BlockSpec((tm, d), lambda i, j: (i, 0)),
        out_shape=jax.ShapeDtypeStruct((s, d), F32),
        scratch_shapes=[pltpu.VMEM((tm, d), BF16)],
        compiler_params=pltpu.CompilerParams(
            dimension_semantics=("arbitrary", "arbitrary"), vmem_limit_bytes=VMEM_LIMIT_BYTES),
        name="ffn_ln",
    )(x1, w1_b, w2_b, ln_g, ln_b)


def kernel(x, positions, w_in, w_pool, pool_scale, w_branch_attn, w_branch_pool, w_out,
           ln_mix_g, ln_mix_b, w_ff1, w_ff2, ln_ff_g, ln_ff_b):
    b, s, d = x.shape
    assert (b, s, d) == (1, SEQ, D_MODEL) and w_in.shape[0] == DEPTH
    half = HEAD_DIM // 2
    inv_freq = ROPE_THETA ** (-jnp.arange(half, dtype=F32) / half)
    invf2 = jnp.concatenate([inv_freq, inv_freq]).reshape(1, HEAD_DIM)
    x2 = x.reshape(s, d)
    pos2 = positions.reshape(s, 1)
    for layer in range(DEPTH):
        w_pool2 = w_pool[layer].reshape(POOL_WIDTH, POOL_GROUP_WIDTH)
        h_main, h_gates, (w_ba_b, w_bb_b, w_out_b, w_pool_b) = _proj(
            x2, pos2, invf2, w_in[layer].astype(BF16),
            [w_branch_attn[layer], w_branch_pool[layer], w_out[layer], w_pool2])
        o_attn, (w1_b, w2_b) = _attn(h_main, [w_ff1[layer], w_ff2[layer]])
        x2 = _mix(o_attn, h_main, h_gates, x2, w_pool_b.reshape(w_pool[layer].shape),
                  pool_scale[layer].reshape(1, POOL_WIDTH), w_ba_b, w_bb_b, w_out_b,
                  ln_mix_g[layer].reshape(1, d), ln_mix_b[layer].reshape(1, d))
        x2 = _ffn(x2, w1_b, w2_b, ln_ff_g[layer].reshape(1, d), ln_ff_b[layer].reshape(1, d))
    return x2.reshape(b, s, d)
```

```python
import functools

import jax
import jax.numpy as jnp
from jax import lax
from jax.experimental import pallas as pl
from jax.experimental.pallas import tpu as pltpu

F32 = jnp.float32
BF16 = jnp.bfloat16

D_MODEL = 2048
SEQ = 8192
HEAD_DIM = 128
ATTN_WIDTH = D_MODEL // 2
ATTN_HEADS = ATTN_WIDTH // HEAD_DIM
POOL_WIDTH = D_MODEL // 2
POOL_WINDOWS = (2, 4, 8, 16)
POOL_GROUP_WIDTH = POOL_WIDTH // len(POOL_WINDOWS)
MAX_POOL_WINDOW = max(POOL_WINDOWS)
POOL_PAD = 8
DILATIONS = (1, 4, 16)
SUB_BLOCK = 128
D_FF = 4 * D_MODEL
IN_WIDTH = 3 * ATTN_WIDTH + POOL_WIDTH + 2 * D_MODEL
MAIN_WIDTH = 3 * ATTN_WIDTH + POOL_WIDTH
ROPE_THETA = 10000.0
LN_EPS = 1e-5
DEPTH = 1
DEEPNORM_ALPHA = (2.0 * DEPTH) ** 0.25
SM_SCALE = HEAD_DIM ** -0.5
LOG2_E = 1.4426950408889634
Q_SCALE = SM_SCALE * LOG2_E
MASK_VALUE = -1e30

VMEM_LIMIT_BYTES = 60 * 1024 * 1024

ATTN_CHUNK = max(DILATIONS) * SUB_BLOCK
PROJ_TM, PROJ_TN = 1024, 1024
MIX_TM, MIX_TN, MIX_PARTS = 512, 512, 2
FFN_TM, FFN_TF = 1024, 1024
FFN_LN_ROWS = 128


def _layer_norm(y, g, b):
    mu = jnp.mean(y, axis=-1, keepdims=True)
    yc = y - mu
    var = jnp.mean(yc * yc, axis=-1, keepdims=True)
    return yc * lax.rsqrt(var + LN_EPS) * g + b


def _residual_layer_norm(x_ref, y_ref, g_ref, b_ref, out_ref, chunk_rows):
    g = g_ref[...]
    b = b_ref[...]
    n_chunks = x_ref.shape[0] // chunk_rows
    if n_chunks == 1:
        out_ref[...] = _layer_norm(DEEPNORM_ALPHA * x_ref[...] + y_ref[...], g, b)
        return

    def body(c, carry):
        rows = pl.ds(pl.multiple_of(c * chunk_rows, chunk_rows), chunk_rows)
        out_ref[rows, :] = _layer_norm(DEEPNORM_ALPHA * x_ref[rows, :] + y_ref[rows, :], g, b)
        return carry

    lax.fori_loop(0, n_chunks, body, 0)


def _proj_kernel(x_ref, pos_ref, invf_ref, w_ref, *rest, n_rope_blocks, n_f32_blocks, n_cast):
    cast_in, (main_ref, gate_ref) = rest[:n_cast], rest[n_cast:n_cast + 2]
    cast_out = rest[n_cast + 2:2 * n_cast + 2]
    xb_ref, cos_ref, sin_ref = rest[2 * n_cast + 2:]
    j = pl.program_id(1)
    tm = x_ref.shape[0]
    half = HEAD_DIM // 2

    def prepare_row_block():
        xb_ref[...] = x_ref[...].astype(BF16)
        lane = lax.broadcasted_iota(jnp.int32, (1, HEAD_DIM), 1)
        low = lane < half
        pos = jnp.where(low, pos_ref[0:tm // 2, :], pos_ref[tm // 2:tm, :])
        ang = pos.astype(F32) * invf_ref[...]
        sign = jnp.where(low, -1.0, 1.0).astype(F32)
        for table_ref, tab in ((cos_ref, jnp.cos(ang)), (sin_ref, jnp.sin(ang))):
            swapped = pltpu.roll(tab, half, 1)
            scale = sign if table_ref is sin_ref else 1.0
            top = jnp.where(low, tab, swapped) * scale
            bottom = jnp.where(low, swapped, tab) * scale
            table_ref[1, 0:tm // 2, :] = top
            table_ref[1, tm // 2:tm, :] = bottom
            table_ref[0, 0:tm // 2, :] = top * Q_SCALE
            table_ref[0, tm // 2:tm, :] = bottom * Q_SCALE

    for src, dst in zip(cast_in, cast_out):
        dst[...] = src[...].astype(BF16)

    def project():
        return jnp.dot(xb_ref[...], w_ref[...], preferred_element_type=F32)

    def project_rotated(table):
        acc = project()
        cos = cos_ref[table]
        sin = sin_ref[table]
        for hh in range(acc.shape[1] // HEAD_DIM):
            cols = slice(hh * HEAD_DIM, (hh + 1) * HEAD_DIM)
            t = acc[:, cols]
            main_ref[:, cols] = t * cos + pltpu.roll(t, half, 1) * sin

    @pl.when(j == 0)
    def _():
        prepare_row_block()
        project_rotated(0)

    @pl.when(j == 1)
    def _():
        project_rotated(1)

    @pl.when(jnp.logical_and(j >= n_rope_blocks, j < n_f32_blocks))
    def _():
        main_ref[...] = project()

    @pl.when(j >= n_f32_blocks)
    def _():
        gate_ref[...] = project().astype(BF16)


def _slab_specs(weights, n_steps, step_index):
    specs, shapes = [], []
    for w in weights:
        rows, cols = w.shape
        specs.append(pl.BlockSpec((rows // n_steps, cols), lambda *g: (step_index(*g), 0)))
        shapes.append(jax.ShapeDtypeStruct(w.shape, BF16))
    return specs, shapes


def _proj(x2, pos2, invf2, w_in_b, cast_weights):
    s, d = x2.shape
    tm, tn = PROJ_TM, PROJ_TN
    n_i, n_j = s // tm, IN_WIDTH // tn
    assert tn == ATTN_WIDTH
    n_rope_blocks = 2
    n_f32_blocks = MAIN_WIDTH // tn
    cast_specs, cast_shapes = _slab_specs(cast_weights, n_i * n_j, lambda i, j: i * n_j + j)
    main_spec = pl.BlockSpec((tm, tn), lambda i, j: (i, jnp.minimum(j, n_f32_blocks - 1)))
    gate_spec = pl.BlockSpec((tm, tn), lambda i, j: (i, jnp.maximum(j - n_f32_blocks, 0)))
    outs = pl.pallas_call(
        functools.partial(_proj_kernel, n_rope_blocks=n_rope_blocks, n_f32_blocks=n_f32_blocks,
                          n_cast=len(cast_weights)),
        grid=(n_i, n_j),
        in_specs=[
            pl.BlockSpec((tm, d), lambda i, j: (i, 0)),
            pl.BlockSpec((tm, 1), lambda i, j: (i, 0)),
            pl.BlockSpec((1, HEAD_DIM), lambda i, j: (0, 0)),
            pl.BlockSpec((d, tn), lambda i, j: (0, j)),
        ] + cast_specs,
        out_specs=[main_spec, gate_spec] + cast_specs,
        out_shape=[jax.ShapeDtypeStruct((s, MAIN_WIDTH), F32),
                   jax.ShapeDtypeStruct((s, IN_WIDTH - MAIN_WIDTH), BF16)] + cast_shapes,
        scratch_shapes=[
            pltpu.VMEM((tm, d), BF16),
            pltpu.VMEM((2, tm, HEAD_DIM), F32),
            pltpu.VMEM((2, tm, HEAD_DIM), F32),
        ],
        compiler_params=pltpu.CompilerParams(
            dimension_semantics=("arbitrary", "arbitrary"), vmem_limit_bytes=VMEM_LIMIT_BYTES),
        name="proj_rope",
    )(x2, pos2, invf2, w_in_b, *cast_weights)
    return outs[0], outs[1], outs[2:]


def _attn_kernel(q_ref, kc_ref, kp_ref, vc_ref, vp_ref, o_ref, bias_ref,
                 qd_refs, kd_refs, vd_refs, o_refs, lse_refs, stage_refs):
    n = pl.program_id(0)
    blk = SUB_BLOCK

    @pl.when(jnp.logical_and(n == 0, pl.program_id(1) == 0))
    def _():
        for vd_ref in vd_refs:
            vd_ref[:, HEAD_DIM:] = jnp.ones((vd_ref.shape[0], HEAD_DIM), BF16)
        qi = lax.broadcasted_iota(jnp.int32, (blk, 2 * blk), 0)
        kj = lax.broadcasted_iota(jnp.int32, (blk, 2 * blk), 1)
        band = (kj >= qi) & (kj <= qi + blk)
        bias_ref[0] = jnp.where(band, 0.0, MASK_VALUE).astype(F32)
        bias_ref[1] = jnp.where(band & (kj >= blk), 0.0, MASK_VALUE).astype(F32)

    names = ("q", "kc", "kp", "vc", "vp")
    src = dict(zip(names, (q_ref, kc_ref, kp_ref, vc_ref, vp_ref)))
    src_dil = 1
    stage = dict(zip(names, stage_refs))

    first_chunk = (n == 0).astype(jnp.int32)

    for p, dil in enumerate(DILATIONS):
        m_len = ATTN_CHUNK // dil
        nb = m_len // blk
        krows = blk + m_len
        qd_ref, kd_ref, vd_ref = qd_refs[p], kd_refs[p], vd_refs[p]

        f = dil // src_dil
        src_len = ATTN_CHUNK // src_dil
        keep_f32 = f > 1 and p + 1 < len(DILATIONS)

        def seg(name, r, m0, rows, f=f, src=src, src_dil=src_dil, src_len=src_len, m_len=m_len):
            if f == 1:
                return src[name][pl.ds(r * m_len + m0, rows), :]
            row0 = (r % src_dil) * src_len + r // src_dil + f * m0
            return src[name][pl.ds(row0, rows, stride=f), :]

        for r in range(dil):
            for name, dst, dst_rows in (("q", qd_ref, m_len), ("k", kd_ref, krows), ("v", vd_ref, krows)):
                if name == "q":
                    cur = seg("q", r, 0, m_len)
                    if keep_f32:
                        stage["q"][r * m_len:(r + 1) * m_len, :] = cur
                    dst[r * m_len:(r + 1) * m_len, :] = cur.astype(BF16)
                    continue
                if keep_f32:
                    prev_full = seg(name + "p", r, 0, m_len)
                    stage[name + "p"][r * m_len:(r + 1) * m_len, :] = prev_full
                    prev_tail = prev_full[m_len - blk:, :]
                else:
                    prev_tail = seg(name + "p", r, m_len - blk, blk)
                cur = seg(name + "c", r, 0, m_len)
                if keep_f32:
                    stage[name + "c"][r * m_len:(r + 1) * m_len, :] = cur
                lanes = slice(0, HEAD_DIM) if name == "v" else slice(None)
                dst[r * dst_rows:r * dst_rows + blk, lanes] = prev_tail.astype(BF16)
                dst[r * dst_rows + blk:(r + 1) * dst_rows, lanes] = cur.astype(BF16)
        if keep_f32:
            src, src_dil = stage, dil

        for r in range(dil):
            for b in range(nb):
                q0 = r * m_len + b * blk
                k0 = r * krows + b * blk
                qb = qd_ref[q0:q0 + blk, :]
                kb = kd_ref[k0:k0 + 2 * blk, :]
                vb = vd_ref[k0:k0 + 2 * blk, :]
                s = lax.dot_general(qb, kb, (((1,), (1,)), ((), ())), preferred_element_type=F32)
                s = s + (bias_ref[first_chunk] if b == 0 else bias_ref[0])
                m = jnp.max(s, axis=-1, keepdims=True)
                pr = jnp.exp2(s - m)
                ext = jnp.dot(pr.astype(BF16), vb, preferred_element_type=F32)
                acc, l = ext[:, :HEAD_DIM], ext[:, HEAD_DIM:]
                rows = pl.ds(b * (blk * dil) + r, blk, stride=dil)
                o_refs[p][rows, :] = acc / l
                lse_refs[p][rows, :] = m + jnp.log2(l)

    for c in range(ATTN_CHUNK // blk):
        rows = slice(c * blk, (c + 1) * blk)
        lse = [ref[rows, :] for ref in lse_refs]
        top = functools.reduce(jnp.maximum, lse)
        w = [jnp.exp2(x - top) for x in lse]
        num = sum(wp * ref[rows, :] for wp, ref in zip(w, o_refs))
        o_ref[rows, :] = (num / sum(w)).astype(o_ref.dtype)


def _attn(h, cast_weights):
    s = h.shape[0]
    c = ATTN_CHUNK
    blk = SUB_BLOCK
    npat = len(DILATIONS)
    n_cast = len(cast_weights)
    n_chunks = s // c
    chunk = lambda col0: pl.BlockSpec((c, HEAD_DIM), lambda n, hh: (n, col0 + hh))
    prev = lambda col0: pl.BlockSpec((c, HEAD_DIM), lambda n, hh: (jnp.maximum(n - 1, 0), col0 + hh))
    cast_specs, cast_shapes = _slab_specs(cast_weights, n_chunks * ATTN_HEADS, lambda n, hh: n * ATTN_HEADS + hh)

    def body(q_ref, kc_ref, kp_ref, vc_ref, vp_ref, *rest):
        cast_in, o_ref, cast_out = rest[:n_cast], rest[n_cast], rest[n_cast + 1:2 * n_cast + 1]
        bias, scr = rest[2 * n_cast + 1], rest[2 * n_cast + 2:]
        for src, dst in zip(cast_in, cast_out):
            dst[...] = src[...].astype(BF16)
        groups = [scr[i * npat:(i + 1) * npat] for i in range(5)]
        _attn_kernel(q_ref, kc_ref, kp_ref, vc_ref, vp_ref, o_ref, bias, *groups, scr[5 * npat:])

    k_scratch = [pltpu.VMEM((dil * blk + c, HEAD_DIM), BF16) for dil in DILATIONS]
    v_scratch = [pltpu.VMEM((dil * blk + c, 2 * HEAD_DIM), BF16) for dil in DILATIONS]
    outs = pl.pallas_call(
        body,
        grid=(n_chunks, ATTN_HEADS),
        in_specs=[chunk(0), chunk(ATTN_HEADS), prev(ATTN_HEADS), chunk(2 * ATTN_HEADS),
                  prev(2 * ATTN_HEADS)] + cast_specs,
        out_specs=[pl.BlockSpec((c, HEAD_DIM), lambda n, hh: (n, hh))] + cast_specs,
        out_shape=[jax.ShapeDtypeStruct((s, ATTN_WIDTH), BF16)] + cast_shapes,
        scratch_shapes=(
            [pltpu.VMEM((2, blk, 2 * blk), F32)]
            + [pltpu.VMEM((c, HEAD_DIM), BF16)] * npat
            + k_scratch + v_scratch
            + [pltpu.VMEM((c, HEAD_DIM), F32)] * (2 * npat)
            + [pltpu.VMEM((c, HEAD_DIM), F32)] * 5),
        compiler_params=pltpu.CompilerParams(
            dimension_semantics=("arbitrary", "arbitrary"), vmem_limit_bytes=VMEM_LIMIT_BYTES),
        name="dilated_attn",
    )(h, h, h, h, h, *cast_weights)
    return outs[0], outs[1:]


def _mix_kernel(oat_ref, u_ref, uh_ref, ga_ref, gp_ref, x_ref, wpool_ref, pscale_ref,
                wba_ref, wbb_ref, wout_ref, g_ref, b_ref, out_ref, ubuf_ref, pm_ref, mg_ref, *lv_refs):
    i = pl.program_id(0)
    tm = u_ref.shape[0]
    halo = MAX_POOL_WINDOW

    pad = POOL_PAD
    top = pad + halo
    rows_all = top + tm

    @pl.when(i == 0)
    def _():
        ubuf_ref[0:top, :] = jnp.zeros((top, POOL_WIDTH), F32)
        for lv_ref in lv_refs:
            lv_ref[0:pad, :] = jnp.zeros((pad, POOL_GROUP_WIDTH), F32)

    @pl.when(i > 0)
    def _():
        ubuf_ref[pad:top, :] = uh_ref[...]

    ubuf_ref[top:rows_all, :] = u_ref[...]

    hm = tm // MIX_PARTS
    parts = [slice(part * hm, (part + 1) * hm) for part in range(MIX_PARTS)]
    n_chunks = D_MODEL // MIX_TN
    chunk_cols = [slice(c * MIX_TN, (c + 1) * MIX_TN) for c in range(n_chunks)]

    def pool_group(part, g):
        rows, w = parts[part], POOL_WINDOWS[g]
        lo, hi = pad + part * hm, top + (part + 1) * hm
        t_glob = i * tm + part * hm + lax.broadcasted_iota(jnp.int32, (hm, 1), 0)
        cols = slice(g * POOL_GROUP_WIDTH, (g + 1) * POOL_GROUP_WIDTH)
        cur, cur_cols, shift, level = ubuf_ref, cols, 1, 0
        while shift < w:
            wsum = cur[lo:hi, cur_cols] + cur[lo - shift:hi - shift, cur_cols]
            shift *= 2
            if shift < w:
                lv_refs[level % 2][lo:hi, :] = wsum
                cur, cur_cols, level = lv_refs[level % 2], slice(None), level + 1
        inv_count = 1.0 / jnp.minimum(t_glob + 1, w).astype(F32)
        pooled = wsum[halo:, :] * inv_count - u_ref[rows, cols]
        y = jnp.dot(pooled.astype(BF16), wpool_ref[g], preferred_element_type=F32)
        pm_ref[rows, cols] = (y * pscale_ref[:, cols]).astype(BF16)

    def attn_branch(part, c):
        rows, cols = parts[part], chunk_cols[c]
        y_attn = jnp.dot(oat_ref[rows, :], wba_ref[:, cols], preferred_element_type=F32)
        out_ref[rows, cols] = jax.nn.sigmoid(ga_ref[rows, cols].astype(F32)) * y_attn

    def pool_branch(part, c):
        rows, cols = parts[part], chunk_cols[c]
        y_pool = jnp.dot(pm_ref[rows, :], wbb_ref[:, cols], preferred_element_type=F32)
        merged = out_ref[rows, cols] + jax.nn.sigmoid(gp_ref[rows, cols].astype(F32)) * y_pool
        mg_ref[rows, cols] = merged.astype(BF16)

    def out_proj(part):
        rows = parts[part]
        out_ref[rows, :] = jnp.dot(mg_ref[rows, :], wout_ref[...], preferred_element_type=F32)

    def norm_rows(rows):
        out_ref[rows, :] = _layer_norm(DEEPNORM_ALPHA * x_ref[rows, :] + out_ref[rows, :], g_ref[...], b_ref[...])

    assert MIX_PARTS == 2 and n_chunks == len(POOL_WINDOWS)
    for part in range(MIX_PARTS):
        for c in range(n_chunks):
            attn_branch(part, c)
            pool_group(part, c)
    for c in range(n_chunks):
        pool_branch(0, c)
    out_proj(0)
    ln_rows = hm // n_chunks
    for c in range(n_chunks):
        pool_branch(1, c)
        norm_rows(slice(c * ln_rows, (c + 1) * ln_rows))
    out_proj(1)
    norm_rows(parts[1])


def _mix(o_attn, h_main, h_gates, x2, w_pool_b, pool_scale, w_ba_b, w_bb_b, w_out_b, ln_g, ln_b):
    s, d = x2.shape
    tm = MIX_TM
    halo = MAX_POOL_WINDOW
    u_col = (3 * ATTN_WIDTH) // POOL_WIDTH
    resident = lambda shape: pl.BlockSpec(shape, lambda i: (0,) * len(shape), pipeline_mode=pl.Buffered(1))
    return pl.pallas_call(
        _mix_kernel,
        grid=(s // tm,),
        in_specs=[
            pl.BlockSpec((tm, ATTN_WIDTH), lambda i: (i, 0)),
            pl.BlockSpec((tm, POOL_WIDTH), lambda i: (i, u_col)),
            pl.BlockSpec((halo, POOL_WIDTH), lambda i: (jnp.maximum(i * (tm // halo) - 1, 0), u_col)),
            pl.BlockSpec((tm, d), lambda i: (i, 0)),
            pl.BlockSpec((tm, d), lambda i: (i, 1)),
            pl.BlockSpec((tm, d), lambda i: (i, 0)),
            resident(w_pool_b.shape),
            resident((1, POOL_WIDTH)),
            resident(w_ba_b.shape),
            resident(w_bb_b.shape),
            resident(w_out_b.shape),
            resident((1, d)),
            resident((1, d)),
        ],
        out_specs=pl.BlockSpec((tm, d), lambda i: (i, 0)),
        out_shape=jax.ShapeDtypeStruct((s, d), F32),
        scratch_shapes=[
            pltpu.VMEM((POOL_PAD + halo + tm, POOL_WIDTH), F32),
            pltpu.VMEM((tm, POOL_WIDTH), BF16),
            pltpu.VMEM((tm, d), BF16),
            pltpu.VMEM((POOL_PAD + halo + tm, POOL_GROUP_WIDTH), F32),
            pltpu.VMEM((POOL_PAD + halo + tm, POOL_GROUP_WIDTH), F32),
        ],
        compiler_params=pltpu.CompilerParams(
            dimension_semantics=("arbitrary",), vmem_limit_bytes=VMEM_LIMIT_BYTES),
        name="mix_ln",
    )(o_attn, h_main, h_main, h_gates, h_gates, x2, w_pool_b, pool_scale, w_ba_b, w_bb_b, w_out_b, ln_g, ln_b)


def _ffn_kernel(x_ref, w1_ref, w2_ref, g_ref, b_ref, out_ref, xb_ref):
    j = pl.program_id(1)

    @pl.when(j == 0)
    def _():
        xb_ref[...] = x_ref[...].astype(BF16)
        out_ref[...] = jnp.zeros(out_ref.shape, F32)

    hid = jnp.dot(xb_ref[...], w1_ref[...], preferred_element_type=F32)
    hid = jnp.square(jnp.maximum(hid, 0.0)).astype(BF16)
    out_ref[...] += jnp.dot(hid, w2_ref[...], preferred_element_type=F32)

    @pl.when(j == pl.num_programs(1) - 1)
    def _():
        _residual_layer_norm(x_ref, out_ref, g_ref, b_ref, out_ref, FFN_LN_ROWS)


def _ffn(x1, w1_b, w2_b, ln_g, ln_b):
    s, d = x1.shape
    tm, tf = FFN_TM, FFN_TF
    return pl.pallas_call(
        _ffn_kernel,
        grid=(s // tm, D_FF // tf),
        in_specs=[
            pl.BlockSpec((tm, d), lambda i, j: (i, 0)),
            pl.BlockSpec((d, tf), lambda i, j: (0, j)),
            pl.BlockSpec((tf, d), lambda i, j: (j, 0)),
            pl.BlockSpec((1, d), lambda i, j: (0, 0)),
            pl.BlockSpec((1, d), lambda i, j: (0, 0)),
        ],
        out_specs=pl.BlockSpec((tm, d), lambda i, j: (i, 0)),
        out_shape=jax.ShapeDtypeStruct((s, d), F32),
        scratch_shapes=[pltpu.VMEM((tm, d), BF16)],
        compiler_params=pltpu.CompilerParams(
            dimension_semantics=("arbitrary", "arbitrary"), vmem_limit_bytes=VMEM_LIMIT_BYTES),
        name="ffn_ln",
    )(x1, w1_b, w2_b, ln_g, ln_b)


def kernel(x, positions, w_in, w_pool, pool_scale, w_branch_attn, w_branch_pool, w_out,
           ln_mix_g, ln_mix_b, w_ff1, w_ff2, ln_ff_g, ln_ff_b):
    b, s, d = x.shape
    assert (b, s, d) == (1, SEQ, D_MODEL) and w_in.shape[0] == DEPTH
    half = HEAD_DIM // 2
    inv_freq = ROPE_THETA ** (-jnp.arange(half, dtype=F32) / half)
    invf2 = jnp.concatenate([inv_freq, inv_freq]).reshape(1, HEAD_DIM)
    x2 = x.reshape(s, d)
    pos2 = positions.reshape(s, 1)
    for layer in range(DEPTH):
        w_pool2 = w_pool[layer].reshape(POOL_WIDTH, POOL_GROUP_WIDTH)
        h_main, h_gates, (w_ba_b, w_bb_b, w_out_b, w_pool_b) = _proj(
            x2, pos2, invf2, w_in[layer].astype(BF16),
            [w_branch_attn[layer], w_branch_pool[layer], w_out[layer], w_pool2])
        o_attn, (w1_b, w2_b) = _attn(h_main, [w_ff1[layer], w_ff2[layer]])
        x2 = _mix(o_attn, h_main, h_gates, x2, w_pool_b.reshape(w_pool[layer].shape),
                  pool_scale[layer].reshape(1, POOL_WIDTH), w_ba_b, w_bb_b, w_out_b,
                  ln_mix_g[layer].reshape(1, d), ln_mix_b[layer].reshape(1, d))
        x2 = _ffn(x2, w1_b, w2_b, ln_ff_g[layer].reshape(1, d), ln_ff_b[layer].reshape(1, d))
    return x2.reshape(b, s, d)
```

```python
import functools

import jax
import jax.numpy as jnp
from jax import lax
from jax.experimental import pallas as pl
from jax.experimental.pallas import tpu as pltpu

F32 = jnp.float32
BF16 = jnp.bfloat16

D_MODEL = 2048
SEQ = 8192
HEAD_DIM = 128
ATTN_WIDTH = D_MODEL // 2
ATTN_HEADS = ATTN_WIDTH // HEAD_DIM
POOL_WIDTH = D_MODEL // 2
POOL_WINDOWS = (2, 4, 8, 16)
POOL_GROUP_WIDTH = POOL_WIDTH // len(POOL_WINDOWS)
MAX_POOL_WINDOW = max(POOL_WINDOWS)
POOL_PAD = 8
DILATIONS = (1, 4, 16)
SUB_BLOCK = 128
D_FF = 4 * D_MODEL
IN_WIDTH = 3 * ATTN_WIDTH + POOL_WIDTH + 2 * D_MODEL
MAIN_WIDTH = 3 * ATTN_WIDTH + POOL_WIDTH
ROPE_THETA = 10000.0
LN_EPS = 1e-5
DEPTH = 1
DEEPNORM_ALPHA = (2.0 * DEPTH) ** 0.25
SM_SCALE = HEAD_DIM ** -0.5
LOG2_E = 1.4426950408889634
Q_SCALE = SM_SCALE * LOG2_E
MASK_VALUE = -1e30

VMEM_LIMIT_BYTES = 60 * 1024 * 1024

ATTN_CHUNK = max(DILATIONS) * SUB_BLOCK
PROJ_TM, PROJ_TN = 1024, 1024
MIX_TM, MIX_TN, MIX_PARTS = 512, 512, 2
FFN_TM, FFN_TF = 1024, 1024
FFN_LN_ROWS = 128


def _layer_norm(y, g, b):
    mu = jnp.mean(y, axis=-1, keepdims=True)
    yc = y - mu
    var = jnp.mean(yc * yc, axis=-1, keepdims=True)
    return yc * lax.rsqrt(var + LN_EPS) * g + b


def _residual_layer_norm(x_ref, y_ref, g_ref, b_ref, out_ref, chunk_rows):
    g = g_ref[...]
    b = b_ref[...]
    n_chunks = x_ref.shape[0] // chunk_rows
    if n_chunks == 1:
        out_ref[...] = _layer_norm(DEEPNORM_ALPHA * x_ref[...] + y_ref[...], g, b)
        return

    def body(c, carry):
        rows = pl.ds(pl.multiple_of(c * chunk_rows, chunk_rows), chunk_rows)
        out_ref[rows, :] = _layer_norm(DEEPNORM_ALPHA * x_ref[rows, :] + y_ref[rows, :], g, b)
        return carry

    lax.fori_loop(0, n_chunks, body, 0)


def _proj_kernel(x_ref, pos_ref, invf_ref, w_ref, *rest, n_f32_blocks, n_cast):
    cast_in, (qkv_ref, u_ref, gate_ref) = rest[:n_cast], rest[n_cast:n_cast + 3]
    cast_out = rest[n_cast + 3:2 * n_cast + 3]
    xb_ref, cos_ref, sin_ref = rest[2 * n_cast + 3:]
    j = pl.program_id(1)
    tm = x_ref.shape[0]
    half = HEAD_DIM // 2

    def prepare_row_block():
        xb_ref[...] = x_ref[...].astype(BF16)
        lane = lax.broadcasted_iota(jnp.int32, (1, HEAD_DIM), 1)
        low = lane < half
        pos = jnp.where(low, pos_ref[0:tm // 2, :], pos_ref[tm // 2:tm, :])
        ang = pos.astype(F32) * invf_ref[...]
        sign = jnp.where(low, -1.0, 1.0).astype(F32)
        for table_ref, tab in ((cos_ref, jnp.cos(ang)), (sin_ref, jnp.sin(ang))):
            swapped = pltpu.roll(tab, half, 1)
            scale = sign if table_ref is sin_ref else 1.0
            top = jnp.where(low, tab, swapped) * scale
            bottom = jnp.where(low, swapped, tab) * scale
            table_ref[1, 0:tm // 2, :] = top
            table_ref[1, tm // 2:tm, :] = bottom
            table_ref[0, 0:tm // 2, :] = top * Q_SCALE
            table_ref[0, tm // 2:tm, :] = bottom * Q_SCALE

    for src, dst in zip(cast_in, cast_out):
        dst[...] = src[...].astype(BF16)

    def project():
        return jnp.dot(xb_ref[...], w_ref[...], preferred_element_type=F32)

    def project_rotated(table):
        acc = project()
        cos = cos_ref[table]
        sin = sin_ref[table]
        for hh in range(acc.shape[1] // HEAD_DIM):
            t = acc[:, hh * HEAD_DIM:(hh + 1) * HEAD_DIM]
            qkv_ref[hh] = t * cos + pltpu.roll(t, half, 1) * sin

    @pl.when(j == 0)
    def _():
        prepare_row_block()
        project_rotated(0)

    @pl.when(j == 1)
    def _():
        project_rotated(1)

    @pl.when(j == 2)
    def _():
        acc = project()
        for hh in range(acc.shape[1] // HEAD_DIM):
            qkv_ref[hh] = acc[:, hh * HEAD_DIM:(hh + 1) * HEAD_DIM]

    @pl.when(j == 3)
    def _():
        u_ref[...] = project()

    @pl.when(j >= n_f32_blocks)
    def _():
        gate_ref[...] = project().astype(BF16)


def _slab_specs(weights, n_steps, step_index):
    specs, shapes = [], []
    for w in weights:
        rows, cols = w.shape
        specs.append(pl.BlockSpec((rows // n_steps, cols), lambda *g: (step_index(*g), 0)))
        shapes.append(jax.ShapeDtypeStruct(w.shape, BF16))
    return specs, shapes


def _proj(x2, pos2, invf2, w_in_b, cast_weights):
    s, d = x2.shape
    tm, tn = PROJ_TM, PROJ_TN
    n_i, n_j = s // tm, IN_WIDTH // tn
    assert tn == ATTN_WIDTH == POOL_WIDTH
    n_f32_blocks = MAIN_WIDTH // tn
    cast_specs, cast_shapes = _slab_specs(cast_weights, n_i * n_j, lambda i, j: i * n_j + j)
    qkv_spec = pl.BlockSpec((ATTN_HEADS, tm, HEAD_DIM), lambda i, j: (jnp.minimum(j, 2), i, 0))
    u_spec = pl.BlockSpec((tm, POOL_WIDTH), lambda i, j: (i, 0))
    gate_spec = pl.BlockSpec((tm, tn), lambda i, j: (i, jnp.maximum(j - n_f32_blocks, 0)))
    outs = pl.pallas_call(
        functools.partial(_proj_kernel, n_f32_blocks=n_f32_blocks, n_cast=len(cast_weights)),
        grid=(n_i, n_j),
        in_specs=[
            pl.BlockSpec((tm, d), lambda i, j: (i, 0)),
            pl.BlockSpec((tm, 1), lambda i, j: (i, 0)),
            pl.BlockSpec((1, HEAD_DIM), lambda i, j: (0, 0)),
            pl.BlockSpec((d, tn), lambda i, j: (0, j)),
        ] + cast_specs,
        out_specs=[qkv_spec, u_spec, gate_spec] + cast_specs,
        out_shape=[jax.ShapeDtypeStruct((3 * ATTN_HEADS, s, HEAD_DIM), F32),
                   jax.ShapeDtypeStruct((s, POOL_WIDTH), F32),
                   jax.ShapeDtypeStruct((s, IN_WIDTH - MAIN_WIDTH), BF16)] + cast_shapes,
        scratch_shapes=[
            pltpu.VMEM((tm, d), BF16),
            pltpu.VMEM((2, tm, HEAD_DIM), F32),
            pltpu.VMEM((2, tm, HEAD_DIM), F32),
        ],
        compiler_params=pltpu.CompilerParams(
            dimension_semantics=("arbitrary", "arbitrary"), vmem_limit_bytes=VMEM_LIMIT_BYTES),
        name="proj_rope",
    )(x2, pos2, invf2, w_in_b, *cast_weights)
    return outs[0], outs[1], outs[2], outs[3:]


def _attn_kernel(q_ref, kc_ref, kp_ref, vc_ref, vp_ref, o_ref, bias_ref,
                 qd_refs, kd_refs, vd_refs, o_refs, lse_refs, stage_refs):
    n = pl.program_id(0)
    blk = SUB_BLOCK

    @pl.when(jnp.logical_and(n == 0, pl.program_id(1) == 0))
    def _():
        for vd_ref in vd_refs:
            vd_ref[:, HEAD_DIM:] = jnp.ones((vd_ref.shape[0], HEAD_DIM), BF16)
        qi = lax.broadcasted_iota(jnp.int32, (blk, 2 * blk), 0)
        kj = lax.broadcasted_iota(jnp.int32, (blk, 2 * blk), 1)
        band = (kj >= qi) & (kj <= qi + blk)
        bias_ref[0] = jnp.where(band, 0.0, MASK_VALUE).astype(F32)
        bias_ref[1] = jnp.where(band & (kj >= blk), 0.0, MASK_VALUE).astype(F32)

    names = ("q", "kc", "kp", "vc", "vp")
    src = dict(zip(names, (q_ref, kc_ref, kp_ref, vc_ref, vp_ref)))
    src_dil = 1
    stage = dict(zip(names, stage_refs))

    first_chunk = (n == 0).astype(jnp.int32)

    for p, dil in enumerate(DILATIONS):
        m_len = ATTN_CHUNK // dil
        nb = m_len // blk
        krows = blk + m_len
        qd_ref, kd_ref, vd_ref = qd_refs[p], kd_refs[p], vd_refs[p]

        f = dil // src_dil
        src_len = ATTN_CHUNK // src_dil
        keep_f32 = f > 1 and p + 1 < len(DILATIONS)

        def seg(name, r, m0, rows, f=f, src=src, src_dil=src_dil, src_len=src_len, m_len=m_len):
            if f == 1:
                return src[name][pl.ds(r * m_len + m0, rows), :]
            row0 = (r % src_dil) * src_len + r // src_dil + f * m0
            return src[name][pl.ds(row0, rows, stride=f), :]

        for r in range(dil):
            for name, dst, dst_rows in (("q", qd_ref, m_len), ("k", kd_ref, krows), ("v", vd_ref, krows)):
                if name == "q":
                    cur = seg("q", r, 0, m_len)
                    if keep_f32:
                        stage["q"][r * m_len:(r + 1) * m_len, :] = cur
                    dst[r * m_len:(r + 1) * m_len, :] = cur.astype(BF16)
                    continue
                if keep_f32:
                    prev_full = seg(name + "p", r, 0, m_len)
                    stage[name + "p"][r * m_len:(r + 1) * m_len, :] = prev_full
                    prev_tail = prev_full[m_len - blk:, :]
                else:
                    prev_tail = seg(name + "p", r, m_len - blk, blk)
                cur = seg(name + "c", r, 0, m_len)
                if keep_f32:
                    stage[name + "c"][r * m_len:(r + 1) * m_len, :] = cur
                lanes = slice(0, HEAD_DIM) if name == "v" else slice(None)
                dst[r * dst_rows:r * dst_rows + blk, lanes] = prev_tail.astype(BF16)
                dst[r * dst_rows + blk:(r + 1) * dst_rows, lanes] = cur.astype(BF16)
        if keep_f32:
            src, src_dil = stage, dil

        for r in range(dil):
            for b in range(nb):
                q0 = r * m_len + b * blk
                k0 = r * krows + b * blk
                qb = qd_ref[q0:q0 + blk, :]
                kb = kd_ref[k0:k0 + 2 * blk, :]
                vb = vd_ref[k0:k0 + 2 * blk, :]
                s = lax.dot_general(qb, kb, (((1,), (1,)), ((), ())), preferred_element_type=F32)
                s = s + (bias_ref[first_chunk] if b == 0 else bias_ref[0])
                m = jnp.max(s, axis=-1, keepdims=True)
                pr = jnp.exp2(s - m)
                ext = jnp.dot(pr.astype(BF16), vb, preferred_element_type=F32)
                acc, l = ext[:, :HEAD_DIM], ext[:, HEAD_DIM:]
                rows = pl.ds(b * (blk * dil) + r, blk, stride=dil)
                o_refs[p][rows, :] = acc / l
                lse_refs[p][rows, :] = m + jnp.log2(l)

    for c in range(ATTN_CHUNK // blk):
        rows = slice(c * blk, (c + 1) * blk)
        lse = [ref[rows, :] for ref in lse_refs]
        top = functools.reduce(jnp.maximum, lse)
        w = [jnp.exp2(x - top) for x in lse]
        num = sum(wp * ref[rows, :] for wp, ref in zip(w, o_refs))
        o_ref[rows, :] = (num / sum(w)).astype(o_ref.dtype)


def _attn(qkv, cast_weights):
    s = qkv.shape[1]
    c = ATTN_CHUNK
    blk = SUB_BLOCK
    npat = len(DILATIONS)
    n_cast = len(cast_weights)
    n_chunks = s // c
    chunk = lambda head0: pl.BlockSpec((None, c, HEAD_DIM), lambda n, hh: (head0 + hh, n, 0))
    prev = lambda head0: pl.BlockSpec((None, c, HEAD_DIM), lambda n, hh: (head0 + hh, jnp.maximum(n - 1, 0), 0))
    cast_specs, cast_shapes = _slab_specs(cast_weights, n_chunks * ATTN_HEADS, lambda n, hh: n * ATTN_HEADS + hh)

    def body(q_ref, kc_ref, kp_ref, vc_ref, vp_ref, *rest):
        cast_in, o_ref, cast_out = rest[:n_cast], rest[n_cast], rest[n_cast + 1:2 * n_cast + 1]
        bias, scr = rest[2 * n_cast + 1], rest[2 * n_cast + 2:]
        for src, dst in zip(cast_in, cast_out):
            dst[...] = src[...].astype(BF16)
        groups = [scr[i * npat:(i + 1) * npat] for i in range(5)]
        _attn_kernel(q_ref, kc_ref, kp_ref, vc_ref, vp_ref, o_ref, bias, *groups, scr[5 * npat:])

    k_scratch = [pltpu.VMEM((dil * blk + c, HEAD_DIM), BF16) for dil in DILATIONS]
    v_scratch = [pltpu.VMEM((dil * blk + c, 2 * HEAD_DIM), BF16) for dil in DILATIONS]
    outs = pl.pallas_call(
        body,
        grid=(n_chunks, ATTN_HEADS),
        in_specs=[chunk(0), chunk(ATTN_HEADS), prev(ATTN_HEADS), chunk(2 * ATTN_HEADS),
                  prev(2 * ATTN_HEADS)] + cast_specs,
        out_specs=[pl.BlockSpec((c, HEAD_DIM), lambda n, hh: (n, hh))] + cast_specs,
        out_shape=[jax.ShapeDtypeStruct((s, ATTN_WIDTH), BF16)] + cast_shapes,
        scratch_shapes=(
            [pltpu.VMEM((2, blk, 2 * blk), F32)]
            + [pltpu.VMEM((c, HEAD_DIM), BF16)] * npat
            + k_scratch + v_scratch
            + [pltpu.VMEM((c, HEAD_DIM), F32)] * (2 * npat)
            + [pltpu.VMEM((c, HEAD_DIM), F32)] * 5),
        compiler_params=pltpu.CompilerParams(
            dimension_semantics=("arbitrary", "arbitrary"), vmem_limit_bytes=VMEM_LIMIT_BYTES),
        name="dilated_attn",
    )(qkv, qkv, qkv, qkv, qkv, *cast_weights)
    return outs[0], outs[1:]


def _mix_kernel(oat_ref, u_ref, uh_ref, ga_ref, gp_ref, x_ref, wpool_ref, pscale_ref,
                wba_ref, wbb_ref, wout_ref, g_ref, b_ref, out_ref, ubuf_ref, pm_ref, mg_ref, *lv_refs):
    i = pl.program_id(0)
    tm = u_ref.shape[0]
    halo = MAX_POOL_WINDOW

    pad = POOL_PAD
    top = pad + halo
    rows_all = top + tm

    @pl.when(i == 0)
    def _():
        ubuf_ref[0:top, :] = jnp.zeros((top, POOL_WIDTH), F32)
        for lv_ref in lv_refs:
            lv_ref[0:pad, :] = jnp.zeros((pad, POOL_GROUP_WIDTH), F32)

    @pl.when(i > 0)
    def _():
        ubuf_ref[pad:top, :] = uh_ref[...]

    ubuf_ref[top:rows_all, :] = u_ref[...]

    hm = tm // MIX_PARTS
    parts = [slice(part * hm, (part + 1) * hm) for part in range(MIX_PARTS)]
    n_chunks = D_MODEL // MIX_TN
    chunk_cols = [slice(c * MIX_TN, (c + 1) * MIX_TN) for c in range(n_chunks)]

    def pool_group(part, g):
        rows, w = parts[part], POOL_WINDOWS[g]
        lo, hi = pad + part * hm, top + (part + 1) * hm
        t_glob = i * tm + part * hm + lax.broadcasted_iota(jnp.int32, (hm, 1), 0)
        cols = slice(g * POOL_GROUP_WIDTH, (g + 1) * POOL_GROUP_WIDTH)
        cur, cur_cols, shift, level = ubuf_ref, cols, 1, 0
        while shift < w:
            wsum = cur[lo:hi, cur_cols] + cur[lo - shift:hi - shift, cur_cols]
            shift *= 2
            if shift < w:
                lv_refs[level % 2][lo:hi, :] = wsum
                cur, cur_cols, level = lv_refs[level % 2], slice(None), level + 1
        inv_count = 1.0 / jnp.minimum(t_glob + 1, w).astype(F32)
        pooled = wsum[halo:, :] * inv_count - u_ref[rows, cols]
        y = jnp.dot(pooled.astype(BF16), wpool_ref[g], preferred_element_type=F32)
        pm_ref[rows, cols] = (y * pscale_ref[:, cols]).astype(BF16)

    def attn_branch(part, c):
        rows, cols = parts[part], chunk_cols[c]
        y_attn = jnp.dot(oat_ref[rows, :], wba_ref[:, cols], preferred_element_type=F32)
        out_ref[rows, cols] = jax.nn.sigmoid(ga_ref[rows, cols].astype(F32)) * y_attn

    def pool_branch(part, c):
        rows, cols = parts[part], chunk_cols[c]
        y_pool = jnp.dot(pm_ref[rows, :], wbb_ref[:, cols], preferred_element_type=F32)
        merged = out_ref[rows, cols] + jax.nn.sigmoid(gp_ref[rows, cols].astype(F32)) * y_pool
        mg_ref[rows, cols] = merged.astype(BF16)

    def out_proj(part):
        rows = parts[part]
        out_ref[rows, :] = jnp.dot(mg_ref[rows, :], wout_ref[...], preferred_element_type=F32)

    def norm_rows(rows):
        out_ref[rows, :] = _layer_norm(DEEPNORM_ALPHA * x_ref[rows, :] + out_ref[rows, :], g_ref[...], b_ref[...])

    assert MIX_PARTS == 2 and n_chunks == len(POOL_WINDOWS)
    for part in range(MIX_PARTS):
        for c in range(n_chunks):
            attn_branch(part, c)
            pool_group(part, c)
    for c in range(n_chunks):
        pool_branch(0, c)
    out_proj(0)
    ln_rows = hm // n_chunks
    for c in range(n_chunks):
        pool_branch(1, c)
        norm_rows(slice(c * ln_rows, (c + 1) * ln_rows))
    out_proj(1)
    norm_rows(parts[1])


def _mix(o_attn, h_pool, h_gates, x2, w_pool_b, pool_scale, w_ba_b, w_bb_b, w_out_b, ln_g, ln_b):
    s, d = x2.shape
    tm = MIX_TM
    halo = MAX_POOL_WINDOW
    resident = lambda shape: pl.BlockSpec(shape, lambda i: (0,) * len(shape), pipeline_mode=pl.Buffered(1))
    return pl.pallas_call(
        _mix_kernel,
        grid=(s // tm,),
        in_specs=[
            pl.BlockSpec((tm, ATTN_WIDTH), lambda i: (i, 0)),
            pl.BlockSpec((tm, POOL_WIDTH), lambda i: (i, 0)),
            pl.BlockSpec((halo, POOL_WIDTH), lambda i: (jnp.maximum(i * (tm // halo) - 1, 0), 0)),
            pl.BlockSpec((tm, d), lambda i: (i, 0)),
            pl.BlockSpec((tm, d), lambda i: (i, 1)),
            pl.BlockSpec((tm, d), lambda i: (i, 0)),
            resident(w_pool_b.shape),
            resident((1, POOL_WIDTH)),
            resident(w_ba_b.shape),
            resident(w_bb_b.shape),
            resident(w_out_b.shape),
            resident((1, d)),
            resident((1, d)),
        ],
        out_specs=pl.BlockSpec((tm, d), lambda i: (i, 0)),
        out_shape=jax.ShapeDtypeStruct((s, d), F32),
        scratch_shapes=[
            pltpu.VMEM((POOL_PAD + halo + tm, POOL_WIDTH), F32),
            pltpu.VMEM((tm, POOL_WIDTH), BF16),
            pltpu.VMEM((tm, d), BF16),
            pltpu.VMEM((POOL_PAD + halo + tm, POOL_GROUP_WIDTH), F32),
            pltpu.VMEM((POOL_PAD + halo + tm, POOL_GROUP_WIDTH), F32),
        ],
        compiler_params=pltpu.CompilerParams(
            dimension_semantics=("arbitrary",), vmem_limit_bytes=VMEM_LIMIT_BYTES),
        name="mix_ln",
    )(o_attn, h_pool, h_pool, h_gates, h_gates, x2, w_pool_b, pool_scale, w_ba_b, w_bb_b, w_out_b, ln_g, ln_b)


def _ffn_kernel(x_ref, w1_ref, w2_ref, g_ref, b_ref, out_ref, xb_ref):
    j = pl.program_id(1)

    @pl.when(j == 0)
    def _():
        xb_ref[...] = x_ref[...].astype(BF16)
        out_ref[...] = jnp.zeros(out_ref.shape, F32)

    hid = jnp.dot(xb_ref[...], w1_ref[...], preferred_element_type=F32)
    hid = jnp.square(jnp.maximum(hid, 0.0)).astype(BF16)
    out_ref[...] += jnp.dot(hid, w2_ref[...], preferred_element_type=F32)

    @pl.when(j == pl.num_programs(1) - 1)
    def _():
        _residual_layer_norm(x_ref, out_ref, g_ref, b_ref, out_ref, FFN_LN_ROWS)


def _ffn(x1, w1_b, w2_b, ln_g, ln_b):
    s, d = x1.shape
    tm, tf = FFN_TM, FFN_TF
    return pl.pallas_call(
        _ffn_kernel,
        grid=(s // tm, D_FF // tf),
        in_specs=[
            pl.BlockSpec((tm, d), lambda i, j: (i, 0)),
            pl.BlockSpec((d, tf), lambda i, j: (0, j)),
            pl.BlockSpec((tf, d), lambda i, j: (j, 0)),
            pl.BlockSpec((1, d), lambda i, j: (0, 0)),
            pl.BlockSpec((1, d), lambda i, j: (0, 0)),
        ],
        out_specs=pl.BlockSpec((tm, d), lambda i, j: (i, 0)),
        out_shape=jax.ShapeDtypeStruct((s, d), F32),
        scratch_shapes=[pltpu.VMEM((tm, d), BF16)],
        compiler_params=pltpu.CompilerParams(
            dimension_semantics=("arbitrary", "arbitrary"), vmem_limit_bytes=VMEM_LIMIT_BYTES),
        name="ffn_ln",
    )(x1, w1_b, w2_b, ln_g, ln_b)


def kernel(x, positions, w_in, w_pool, pool_scale, w_branch_attn, w_branch_pool, w_out,
           ln_mix_g, ln_mix_b, w_ff1, w_ff2, ln_ff_g, ln_ff_b):
    b, s, d = x.shape
    assert (b, s, d) == (1, SEQ, D_MODEL) and w_in.shape[0] == DEPTH
    half = HEAD_DIM // 2
    inv_freq = ROPE_THETA ** (-jnp.arange(half, dtype=F32) / half)
    invf2 = jnp.concatenate([inv_freq, inv_freq]).reshape(1, HEAD_DIM)
    x2 = x.reshape(s, d)
    pos2 = positions.reshape(s, 1)
    for layer in range(DEPTH):
        w_pool2 = w_pool[layer].reshape(POOL_WIDTH, POOL_GROUP_WIDTH)
        qkv, h_pool, h_gates, (w_ba_b, w_bb_b, w_out_b, w_pool_b) = _proj(
            x2, pos2, invf2, w_in[layer].astype(BF16),
            [w_branch_attn[layer], w_branch_pool[layer], w_out[layer], w_pool2])
        o_attn, (w1_b, w2_b) = _attn(qkv, [w_ff1[layer], w_ff2[layer]])
        x2 = _mix(o_attn, h_pool, h_gates, x2, w_pool_b.reshape(w_pool[layer].shape),
                  pool_scale[layer].reshape(1, POOL_WIDTH), w_ba_b, w_bb_b, w_out_b,
                  ln_mix_g[layer].reshape(1, d), ln_mix_b[layer].reshape(1, d))
        x2 = _ffn(x2, w1_b, w2_b, ln_ff_g[layer].reshape(1, d), ln_ff_b[layer].reshape(1, d))
    return x2.reshape(b, s, d)
```

```python
import functools

import jax
import jax.numpy as jnp
from jax import lax
from jax.experimental import pallas as pl
from jax.experimental.pallas import tpu as pltpu

F32 = jnp.float32
BF16 = jnp.bfloat16

D_MODEL = 2048
SEQ = 8192
HEAD_DIM = 128
ATTN_WIDTH = D_MODEL // 2
ATTN_HEADS = ATTN_WIDTH // HEAD_DIM
POOL_WIDTH = D_MODEL // 2
POOL_WINDOWS = (2, 4, 8, 16)
POOL_GROUP_WIDTH = POOL_WIDTH // len(POOL_WINDOWS)
MAX_POOL_WINDOW = max(POOL_WINDOWS)
POOL_PAD = 8
DILATIONS = (1, 4, 16)
SUB_BLOCK = 128
D_FF = 4 * D_MODEL
IN_WIDTH = 3 * ATTN_WIDTH + POOL_WIDTH + 2 * D_MODEL
MAIN_WIDTH = 3 * ATTN_WIDTH + POOL_WIDTH
ROPE_THETA = 10000.0
LN_EPS = 1e-5
DEPTH = 1
DEEPNORM_ALPHA = (2.0 * DEPTH) ** 0.25
SM_SCALE = HEAD_DIM ** -0.5
LOG2_E = 1.4426950408889634
Q_SCALE = SM_SCALE * LOG2_E
MASK_VALUE = -1e30

VMEM_LIMIT_BYTES = 60 * 1024 * 1024

ATTN_CHUNK = max(DILATIONS) * SUB_BLOCK
PROJ_TM, PROJ_TN = 1024, 1024
MIX_TM, MIX_TN, MIX_PARTS = 512, 512, 2
FFN_TM, FFN_TF = 1024, 1024
FFN_LN_ROWS = 128


def _layer_norm(y, g, b):
    mu = jnp.mean(y, axis=-1, keepdims=True)
    yc = y - mu
    var = jnp.mean(yc * yc, axis=-1, keepdims=True)
    return yc * lax.rsqrt(var + LN_EPS) * g + b


def _proj_kernel(x_ref, pos_ref, invf_ref, w_ref, *rest, n_rope_blocks, n_f32_blocks, n_cast):
    cast_in, (main_ref, gate_ref) = rest[:n_cast], rest[n_cast:n_cast + 2]
    cast_out = rest[n_cast + 2:2 * n_cast + 2]
    xb_ref, cos_ref, sin_ref = rest[2 * n_cast + 2:]
    j = pl.program_id(1)
    tm = x_ref.shape[0]
    half = HEAD_DIM // 2

    def prepare_row_block():
        xb_ref[...] = x_ref[...].astype(BF16)
        lane = lax.broadcasted_iota(jnp.int32, (1, HEAD_DIM), 1)
        low = lane < half
        pos = jnp.where(low, pos_ref[0:tm // 2, :], pos_ref[tm // 2:tm, :])
        ang = pos.astype(F32) * invf_ref[...]
        sign = jnp.where(low, -1.0, 1.0).astype(F32)
        for table_ref, tab in ((cos_ref, jnp.cos(ang)), (sin_ref, jnp.sin(ang))):
            swapped = pltpu.roll(tab, half, 1)
            scale = sign if table_ref is sin_ref else 1.0
            top = jnp.where(low, tab, swapped) * scale
            bottom = jnp.where(low, swapped, tab) * scale
            table_ref[1, 0:tm // 2, :] = top
            table_ref[1, tm // 2:tm, :] = bottom
            table_ref[0, 0:tm // 2, :] = top * Q_SCALE
            table_ref[0, tm // 2:tm, :] = bottom * Q_SCALE

    for src, dst in zip(cast_in, cast_out):
        dst[...] = src[...].astype(BF16)

    def project():
        return jnp.dot(xb_ref[...], w_ref[...], preferred_element_type=F32)

    def project_rotated(table):
        acc = project()
        cos = cos_ref[table]
        sin = sin_ref[table]
        for hh in range(acc.shape[1] // HEAD_DIM):
            cols = slice(hh * HEAD_DIM, (hh + 1) * HEAD_DIM)
            t = acc[:, cols]
            main_ref[:, cols] = t * cos + pltpu.roll(t, half, 1) * sin

    @pl.when(j == 0)
    def _():
        prepare_row_block()
        project_rotated(0)

    @pl.when(j == 1)
    def _():
        project_rotated(1)

    @pl.when(jnp.logical_and(j >= n_rope_blocks, j < n_f32_blocks))
    def _():
        main_ref[...] = project()

    @pl.when(j >= n_f32_blocks)
    def _():
        gate_ref[...] = project().astype(BF16)


def _slab_specs(weights, n_steps, step_index):
    specs, shapes = [], []
    for w in weights:
        rows, cols = w.shape
        specs.append(pl.BlockSpec((rows // n_steps, cols), lambda *g: (step_index(*g), 0)))
        shapes.append(jax.ShapeDtypeStruct(w.shape, BF16))
    return specs, shapes


def _proj(x2, pos2, invf2, w_in_b, cast_weights):
    s, d = x2.shape
    tm, tn = PROJ_TM, PROJ_TN
    n_i, n_j = s // tm, IN_WIDTH // tn
    assert tn == ATTN_WIDTH
    n_rope_blocks = 2
    n_f32_blocks = MAIN_WIDTH // tn
    cast_specs, cast_shapes = _slab_specs(cast_weights, n_i * n_j, lambda i, j: i * n_j + j)
    main_spec = pl.BlockSpec((tm, tn), lambda i, j: (i, jnp.minimum(j, n_f32_blocks - 1)))
    gate_spec = pl.BlockSpec((tm, tn), lambda i, j: (i, jnp.maximum(j - n_f32_blocks, 0)))
    outs = pl.pallas_call(
        functools.partial(_proj_kernel, n_rope_blocks=n_rope_blocks, n_f32_blocks=n_f32_blocks,
                          n_cast=len(cast_weights)),
        grid=(n_i, n_j),
        in_specs=[
            pl.BlockSpec((tm, d), lambda i, j: (i, 0)),
            pl.BlockSpec((tm, 1), lambda i, j: (i, 0)),
            pl.BlockSpec((1, HEAD_DIM), lambda i, j: (0, 0)),
            pl.BlockSpec((d, tn), lambda i, j: (0, j)),
        ] + cast_specs,
        out_specs=[main_spec, gate_spec] + cast_specs,
        out_shape=[jax.ShapeDtypeStruct((s, MAIN_WIDTH), F32),
                   jax.ShapeDtypeStruct((s, IN_WIDTH - MAIN_WIDTH), BF16)] + cast_shapes,
        scratch_shapes=[
            pltpu.VMEM((tm, d), BF16),
            pltpu.VMEM((2, tm, HEAD_DIM), F32),
            pltpu.VMEM((2, tm, HEAD_DIM), F32),
        ],
        compiler_params=pltpu.CompilerParams(
            dimension_semantics=("arbitrary", "arbitrary"), vmem_limit_bytes=VMEM_LIMIT_BYTES),
        name="proj_rope",
    )(x2, pos2, invf2, w_in_b, *cast_weights)
    return outs[0], outs[1], outs[2:]


def _attn_kernel(q_ref, kc_ref, kp_ref, vc_ref, vp_ref, o_ref, bias_ref,
                 qd_refs, kd_refs, vd_refs, o_refs, lse_refs, stage_refs):
    n = pl.program_id(0)
    blk = SUB_BLOCK

    @pl.when(jnp.logical_and(n == 0, pl.program_id(1) == 0))
    def _():
        for vd_ref in vd_refs:
            vd_ref[:, HEAD_DIM:] = jnp.ones((vd_ref.shape[0], HEAD_DIM), BF16)
        qi = lax.broadcasted_iota(jnp.int32, (blk, 2 * blk), 0)
        kj = lax.broadcasted_iota(jnp.int32, (blk, 2 * blk), 1)
        band = (kj >= qi) & (kj <= qi + blk)
        bias_ref[0] = jnp.where(band, 0.0, MASK_VALUE).astype(F32)
        bias_ref[1] = jnp.where(band & (kj >= blk), 0.0, MASK_VALUE).astype(F32)

    names = ("q", "kc", "kp", "vc", "vp")
    src = dict(zip(names, (q_ref, kc_ref, kp_ref, vc_ref, vp_ref)))
    src_dil = 1
    stage = dict(zip(names, stage_refs))

    first_chunk = (n == 0).astype(jnp.int32)

    for p, dil in enumerate(DILATIONS):
        m_len = ATTN_CHUNK // dil
        nb = m_len // blk
        krows = blk + m_len
        qd_ref, kd_ref, vd_ref = qd_refs[p], kd_refs[p], vd_refs[p]

        f = dil // src_dil
        src_len = ATTN_CHUNK // src_dil
        keep_f32 = f > 1 and p + 1 < len(DILATIONS)

        def seg(name, r, m0, rows, f=f, src=src, src_dil=src_dil, src_len=src_len, m_len=m_len):
            if f == 1:
                return src[name][pl.ds(r * m_len + m0, rows), :]
            row0 = (r % src_dil) * src_len + r // src_dil + f * m0
            return src[name][pl.ds(row0, rows, stride=f), :]

        for r in range(dil):
            for name, dst, dst_rows in (("q", qd_ref, m_len), ("k", kd_ref, krows), ("v", vd_ref, krows)):
                if name == "q":
                    cur = seg("q", r, 0, m_len)
                    if keep_f32:
                        stage["q"][r * m_len:(r + 1) * m_len, :] = cur
                    dst[r * m_len:(r + 1) * m_len, :] = cur.astype(BF16)
                    continue
                if keep_f32:
                    prev_full = seg(name + "p", r, 0, m_len)
                    stage[name + "p"][r * m_len:(r + 1) * m_len, :] = prev_full
                    prev_tail = prev_full[m_len - blk:, :]
                else:
                    prev_tail = seg(name + "p", r, m_len - blk, blk)
                cur = seg(name + "c", r, 0, m_len)
                if keep_f32:
                    stage[name + "c"][r * m_len:(r + 1) * m_len, :] = cur
                lanes = slice(0, HEAD_DIM) if name == "v" else slice(None)
                dst[r * dst_rows:r * dst_rows + blk, lanes] = prev_tail.astype(BF16)
                dst[r * dst_rows + blk:(r + 1) * dst_rows, lanes] = cur.astype(BF16)
        if keep_f32:
            src, src_dil = stage, dil

        for r in range(dil):
            for b in range(nb):
                q0 = r * m_len + b * blk
                k0 = r * krows + b * blk
                qb = qd_ref[q0:q0 + blk, :]
                kb = kd_ref[k0:k0 + 2 * blk, :]
                vb = vd_ref[k0:k0 + 2 * blk, :]
                s = lax.dot_general(qb, kb, (((1,), (1,)), ((), ())), preferred_element_type=F32)
                s = s + (bias_ref[first_chunk] if b == 0 else bias_ref[0])
                m = jnp.max(s, axis=-1, keepdims=True)
                pr = jnp.exp2(s - m)
                ext = jnp.dot(pr.astype(BF16), vb, preferred_element_type=F32)
                acc, l = ext[:, :HEAD_DIM], ext[:, HEAD_DIM:]
                rows = pl.ds(b * (blk * dil) + r, blk, stride=dil)
                o_refs[p][rows, :] = acc / l
                lse_refs[p][rows, :] = m + jnp.log2(l)

    for c in range(ATTN_CHUNK // blk):
        rows = slice(c * blk, (c + 1) * blk)
        lse = [ref[rows, :] for ref in lse_refs]
        top = functools.reduce(jnp.maximum, lse)
        w = [jnp.exp2(x - top) for x in lse]
        num = sum(wp * ref[rows, :] for wp, ref in zip(w, o_refs))
        o_ref[rows, :] = (num / sum(w)).astype(o_ref.dtype)


def _attn(h, cast_weights):
    s = h.shape[0]
    c = ATTN_CHUNK
    blk = SUB_BLOCK
    npat = len(DILATIONS)
    n_cast = len(cast_weights)
    n_chunks = s // c
    chunk = lambda col0: pl.BlockSpec((c, HEAD_DIM), lambda n, hh: (n, col0 + hh))
    prev = lambda col0: pl.BlockSpec((c, HEAD_DIM), lambda n, hh: (jnp.maximum(n - 1, 0), col0 + hh))
    cast_specs, cast_shapes = _slab_specs(cast_weights, n_chunks * ATTN_HEADS, lambda n, hh: n * ATTN_HEADS + hh)

    def body(q_ref, kc_ref, kp_ref, vc_ref, vp_ref, *rest):
        cast_in, o_ref, cast_out = rest[:n_cast], rest[n_cast], rest[n_cast + 1:2 * n_cast + 1]
        bias, scr = rest[2 * n_cast + 1], rest[2 * n_cast + 2:]
        for src, dst in zip(cast_in, cast_out):
            dst[...] = src[...].astype(BF16)
        groups = [scr[i * npat:(i + 1) * npat] for i in range(5)]
        _attn_kernel(q_ref, kc_ref, kp_ref, vc_ref, vp_ref, o_ref, bias, *groups, scr[5 * npat:])

    k_scratch = [pltpu.VMEM((dil * blk + c, HEAD_DIM), BF16) for dil in DILATIONS]
    v_scratch = [pltpu.VMEM((dil * blk + c, 2 * HEAD_DIM), BF16) for dil in DILATIONS]
    outs = pl.pallas_call(
        body,
        grid=(n_chunks, ATTN_HEADS),
        in_specs=[chunk(0), chunk(ATTN_HEADS), prev(ATTN_HEADS), chunk(2 * ATTN_HEADS),
                  prev(2 * ATTN_HEADS)] + cast_specs,
        out_specs=[pl.BlockSpec((c, HEAD_DIM), lambda n, hh: (n, hh))] + cast_specs,
        out_shape=[jax.ShapeDtypeStruct((s, ATTN_WIDTH), BF16)] + cast_shapes,
        scratch_shapes=(
            [pltpu.VMEM((2, blk, 2 * blk), F32)]
            + [pltpu.VMEM((c, HEAD_DIM), BF16)] * npat
            + k_scratch + v_scratch
            + [pltpu.VMEM((c, HEAD_DIM), F32)] * (2 * npat)
            + [pltpu.VMEM((c, HEAD_DIM), F32)] * 5),
        compiler_params=pltpu.CompilerParams(
            dimension_semantics=("arbitrary", "arbitrary"), vmem_limit_bytes=VMEM_LIMIT_BYTES),
        name="dilated_attn",
    )(h, h, h, h, h, *cast_weights)
    return outs[0], outs[1:]


def _mix_kernel(oat_ref, u_ref, uh_ref, ga_ref, gp_ref, x_ref, wpool_ref, pscale_ref,
                wba_ref, wbb_ref, wout_ref, g_ref, b_ref, out_ref, ubuf_ref, pm_ref, mg_ref, *lv_refs):
    i = pl.program_id(0)
    tm = u_ref.shape[0]
    halo = MAX_POOL_WINDOW

    pad = POOL_PAD
    top = pad + halo
    rows_all = top + tm

    @pl.when(i == 0)
    def _():
        ubuf_ref[0:top, :] = jnp.zeros((top, POOL_WIDTH), F32)
        for lv_ref in lv_refs:
            lv_ref[0:pad, :] = jnp.zeros((pad, POOL_GROUP_WIDTH), F32)

    @pl.when(i > 0)
    def _():
        ubuf_ref[pad:top, :] = uh_ref[...]

    ubuf_ref[top:rows_all, :] = u_ref[...]

    hm = tm // MIX_PARTS
    parts = [slice(part * hm, (part + 1) * hm) for part in range(MIX_PARTS)]
    n_chunks = D_MODEL // MIX_TN
    chunk_cols = [slice(c * MIX_TN, (c + 1) * MIX_TN) for c in range(n_chunks)]

    def pool_group(part, g):
        rows, w = parts[part], POOL_WINDOWS[g]
        lo, hi = pad + part * hm, top + (part + 1) * hm
        t_glob = i * tm + part * hm + lax.broadcasted_iota(jnp.int32, (hm, 1), 0)
        cols = slice(g * POOL_GROUP_WIDTH, (g + 1) * POOL_GROUP_WIDTH)
        cur, cur_cols, shift, level = ubuf_ref, cols, 1, 0
        while shift < w:
            wsum = cur[lo:hi, cur_cols] + cur[lo - shift:hi - shift, cur_cols]
            shift *= 2
            if shift < w:
                lv_refs[level % 2][lo:hi, :] = wsum
                cur, cur_cols, level = lv_refs[level % 2], slice(None), level + 1
        inv_count = 1.0 / jnp.minimum(t_glob + 1, w).astype(F32)
        pooled = wsum[halo:, :] * inv_count - u_ref[rows, cols]
        y = jnp.dot(pooled.astype(BF16), wpool_ref[g], preferred_element_type=F32)
        pm_ref[rows, cols] = (y * pscale_ref[:, cols]).astype(BF16)

    def attn_branch(part, c):
        rows, cols = parts[part], chunk_cols[c]
        y_attn = jnp.dot(oat_ref[rows, :], wba_ref[:, cols], preferred_element_type=F32)
        out_ref[rows, cols] = jax.nn.sigmoid(ga_ref[rows, cols].astype(F32)) * y_attn

    def pool_branch(part, c):
        rows, cols = parts[part], chunk_cols[c]
        y_pool = jnp.dot(pm_ref[rows, :], wbb_ref[:, cols], preferred_element_type=F32)
        merged = out_ref[rows, cols] + jax.nn.sigmoid(gp_ref[rows, cols].astype(F32)) * y_pool
        mg_ref[rows, cols] = merged.astype(BF16)

    def out_proj(part):
        rows = parts[part]
        out_ref[rows, :] = jnp.dot(mg_ref[rows, :], wout_ref[...], preferred_element_type=F32)

    def norm_rows(rows):
        out_ref[rows, :] = _layer_norm(DEEPNORM_ALPHA * x_ref[rows, :] + out_ref[rows, :], g_ref[...], b_ref[...])

    assert MIX_PARTS == 2 and n_chunks == len(POOL_WINDOWS)
    for part in range(MIX_PARTS):
        for c in range(n_chunks):
            attn_branch(part, c)
            pool_group(part, c)
    for c in range(n_chunks):
        pool_branch(0, c)
    out_proj(0)
    ln_rows = hm // n_chunks
    for c in range(n_chunks):
        pool_branch(1, c)
        norm_rows(slice(c * ln_rows, (c + 1) * ln_rows))
    out_proj(1)
    norm_rows(parts[1])


def _mix(o_attn, h_main, h_gates, x2, w_pool_b, pool_scale, w_ba_b, w_bb_b, w_out_b, ln_g, ln_b):
    s, d = x2.shape
    tm = MIX_TM
    halo = MAX_POOL_WINDOW
    u_col = (3 * ATTN_WIDTH) // POOL_WIDTH
    resident = lambda shape: pl.BlockSpec(shape, lambda i: (0,) * len(shape), pipeline_mode=pl.Buffered(1))
    return pl.pallas_call(
        _mix_kernel,
        grid=(s // tm,),
        in_specs=[
            pl.BlockSpec((tm, ATTN_WIDTH), lambda i: (i, 0)),
            pl.BlockSpec((tm, POOL_WIDTH), lambda i: (i, u_col)),
            pl.BlockSpec((halo, POOL_WIDTH), lambda i: (jnp.maximum(i * (tm // halo) - 1, 0), u_col)),
            pl.BlockSpec((tm, d), lambda i: (i, 0)),
            pl.BlockSpec((tm, d), lambda i: (i, 1)),
            pl.BlockSpec((tm, d), lambda i: (i, 0)),
            resident(w_pool_b.shape),
            resident((1, POOL_WIDTH)),
            resident(w_ba_b.shape),
            resident(w_bb_b.shape),
            resident(w_out_b.shape),
            resident((1, d)),
            resident((1, d)),
        ],
        out_specs=pl.BlockSpec((tm, d), lambda i: (i, 0)),
        out_shape=jax.ShapeDtypeStruct((s, d), F32),
        scratch_shapes=[
            pltpu.VMEM((POOL_PAD + halo + tm, POOL_WIDTH), F32),
            pltpu.VMEM((tm, POOL_WIDTH), BF16),
            pltpu.VMEM((tm, d), BF16),
            pltpu.VMEM((POOL_PAD + halo + tm, POOL_GROUP_WIDTH), F32),
            pltpu.VMEM((POOL_PAD + halo + tm, POOL_GROUP_WIDTH), F32),
        ],
        compiler_params=pltpu.CompilerParams(
            dimension_semantics=("arbitrary",), vmem_limit_bytes=VMEM_LIMIT_BYTES),
        name="mix_ln",
    )(o_attn, h_main, h_main, h_gates, h_gates, x2, w_pool_b, pool_scale, w_ba_b, w_bb_b, w_out_b, ln_g, ln_b)


def _ffn_kernel(x_ref, w1_ref, w2_ref, g_ref, b_ref, out_ref, xb_ref):
    j = pl.program_id(1)
    last = pl.num_programs(1) - 1
    tm = x_ref.shape[0]

    def accumulate(rows, first=False):
        hid = jnp.dot(xb_ref[rows, :], w1_ref[...], preferred_element_type=F32)
        hid = jnp.square(jnp.maximum(hid, 0.0)).astype(BF16)
        part = jnp.dot(hid, w2_ref[...], preferred_element_type=F32)
        if first:
            out_ref[rows, :] = part
        else:
            out_ref[rows, :] += part

    def normalize(row0, n_rows):
        for c in range(n_rows // FFN_LN_ROWS):
            rows = slice(row0 + c * FFN_LN_ROWS, row0 + (c + 1) * FFN_LN_ROWS)
            out_ref[rows, :] = _layer_norm(DEEPNORM_ALPHA * x_ref[rows, :] + out_ref[rows, :],
                                           g_ref[...], b_ref[...])

    @pl.when(j == 0)
    def _():
        xb_ref[...] = x_ref[...].astype(BF16)
        accumulate(slice(None), first=True)

    @pl.when(jnp.logical_and(j > 0, j < last))
    def _():
        accumulate(slice(None))

    @pl.when(j == last)
    def _():
        half = tm // 2
        accumulate(slice(0, half))
        normalize(0, half)
        accumulate(slice(half, tm))
        normalize(half, half)


def _ffn(x1, w1_b, w2_b, ln_g, ln_b):
    s, d = x1.shape
    tm, tf = FFN_TM, FFN_TF
    return pl.pallas_call(
        _ffn_kernel,
        grid=(s // tm, D_FF // tf),
        in_specs=[
            pl.BlockSpec((tm, d), lambda i, j: (i, 0)),
            pl.BlockSpec((d, tf), lambda i, j: (0, j)),
            pl.BlockSpec((tf, d), lambda i, j: (j, 0)),
            pl.BlockSpec((1, d), lambda i, j: (0, 0)),
            pl.BlockSpec((1, d), lambda i, j: (0, 0)),
        ],
        out_specs=pl.BlockSpec((tm, d), lambda i, j: (i, 0)),
        out_shape=jax.ShapeDtypeStruct((s, d), F32),
        scratch_shapes=[pltpu.VMEM((tm, d), BF16)],
        compiler_params=pltpu.CompilerParams(
            dimension_semantics=("arbitrary", "arbitrary"), vmem_limit_bytes=VMEM_LIMIT_BYTES),
        name="ffn_ln",
    )(x1, w1_b, w2_b, ln_g, ln_b)


def kernel(x, positions, w_in, w_pool, pool_scale, w_branch_attn, w_branch_pool, w_out,
           ln_mix_g, ln_mix_b, w_ff1, w_ff2, ln_ff_g, ln_ff_b):
    b, s, d = x.shape
    assert (b, s, d) == (1, SEQ, D_MODEL) and w_in.shape[0] == DEPTH
    half = HEAD_DIM // 2
    inv_freq = ROPE_THETA ** (-jnp.arange(half, dtype=F32) / half)
    invf2 = jnp.concatenate([inv_freq, inv_freq]).reshape(1, HEAD_DIM)
    x2 = x.reshape(s, d)
    pos2 = positions.reshape(s, 1)
    for layer in range(DEPTH):
        w_pool2 = w_pool[layer].reshape(POOL_WIDTH, POOL_GROUP_WIDTH)
        h_main, h_gates, (w_ba_b, w_bb_b, w_out_b, w_pool_b) = _proj(
            x2, pos2, invf2, w_in[layer].astype(BF16),
            [w_branch_attn[layer], w_branch_pool[layer], w_out[layer], w_pool2])
        o_attn, (w1_b, w2_b) = _attn(h_main, [w_ff1[layer], w_ff2[layer]])
        x2 = _mix(o_attn, h_main, h_gates, x2, w_pool_b.reshape(w_pool[layer].shape),
                  pool_scale[layer].reshape(1, POOL_WIDTH), w_ba_b, w_bb_b, w_out_b,
                  ln_mix_g[layer].reshape(1, d), ln_mix_b[layer].reshape(1, d))
        x2 = _ffn(x2, w1_b, w2_b, ln_ff_g[layer].reshape(1, d), ln_ff_b[layer].reshape(1, d))
    return x2.reshape(b, s, d)
```

```python
import functools

import jax
import jax.numpy as jnp
from jax import lax
from jax.experimental import pallas as pl
from jax.experimental.pallas import tpu as pltpu

F32 = jnp.float32
BF16 = jnp.bfloat16

D_MODEL = 2048
SEQ = 8192
HEAD_DIM = 128
ATTN_WIDTH = D_MODEL // 2
ATTN_HEADS = ATTN_WIDTH // HEAD_DIM
POOL_WIDTH = D_MODEL // 2
POOL_WINDOWS = (2, 4, 8, 16)
POOL_GROUP_WIDTH = POOL_WIDTH // len(POOL_WINDOWS)
MAX_POOL_WINDOW = max(POOL_WINDOWS)
POOL_PAD = 8
DILATIONS = (1, 4, 16)
SUB_BLOCK = 128
D_FF = 4 * D_MODEL
IN_WIDTH = 3 * ATTN_WIDTH + POOL_WIDTH + 2 * D_MODEL
MAIN_WIDTH = 3 * ATTN_WIDTH + POOL_WIDTH
ROPE_THETA = 10000.0
LN_EPS = 1e-5
DEPTH = 1
DEEPNORM_ALPHA = (2.0 * DEPTH) ** 0.25
SM_SCALE = HEAD_DIM ** -0.5
LOG2_E = 1.4426950408889634
Q_SCALE = SM_SCALE * LOG2_E
MASK_VALUE = -1e30

VMEM_LIMIT_BYTES = 60 * 1024 * 1024

ATTN_CHUNK = max(DILATIONS) * SUB_BLOCK
PROJ_TM, PROJ_TN = 1024, 1024
MIX_TM, MIX_TN, MIX_PARTS = 512, 512, 2
FFN_TM, FFN_TF = 1024, 1024
FFN_LN_ROWS = 128


def _layer_norm(y, g, b):
    mu = jnp.mean(y, axis=-1, keepdims=True)
    yc = y - mu
    var = jnp.mean(yc * yc, axis=-1, keepdims=True)
    return yc * lax.rsqrt(var + LN_EPS) * g + b


def _residual_layer_norm(x_ref, y_ref, g_ref, b_ref, out_ref, chunk_rows):
    g = g_ref[...]
    b = b_ref[...]

    def body(c, carry):
        rows = pl.ds(pl.multiple_of(c * chunk_rows, chunk_rows), chunk_rows)
        out_ref[rows, :] = _layer_norm(DEEPNORM_ALPHA * x_ref[rows, :] + y_ref[rows, :], g, b)
        return carry

    lax.fori_loop(0, x_ref.shape[0] // chunk_rows, body, 0)


def _proj_kernel(x_ref, pos_ref, invf_ref, w_ref, *rest, n_rope_blocks, n_f32_blocks, n_cast):
    cast_in, (main_ref, gate_ref) = rest[:n_cast], rest[n_cast:n_cast + 2]
    cast_out = rest[n_cast + 2:2 * n_cast + 2]
    xb_ref, cos_ref, sin_ref = rest[2 * n_cast + 2:]
    j = pl.program_id(1)
    tm = x_ref.shape[0]
    half = HEAD_DIM // 2

    def prepare_row_block():
        xb_ref[...] = x_ref[...].astype(BF16)
        lane = lax.broadcasted_iota(jnp.int32, (1, HEAD_DIM), 1)
        low = lane < half
        pos = jnp.where(low, pos_ref[0:tm // 2, :], pos_ref[tm // 2:tm, :])
        ang = pos.astype(F32) * invf_ref[...]
        sign = jnp.where(low, -1.0, 1.0).astype(F32)
        for table_ref, tab in ((cos_ref, jnp.cos(ang)), (sin_ref, jnp.sin(ang))):
            swapped = pltpu.roll(tab, half, 1)
            scale = sign if table_ref is sin_ref else 1.0
            top = jnp.where(low, tab, swapped) * scale
            bottom = jnp.where(low, swapped, tab) * scale
            table_ref[1, 0:tm // 2, :] = top
            table_ref[1, tm // 2:tm, :] = bottom
            table_ref[0, 0:tm // 2, :] = top * Q_SCALE
            table_ref[0, tm // 2:tm, :] = bottom * Q_SCALE

    for src, dst in zip(cast_in, cast_out):
        dst[...] = src[...].astype(BF16)

    def project():
        return jnp.dot(xb_ref[...], w_ref[...], preferred_element_type=F32)

    def project_rotated(table):
        acc = project()
        cos = cos_ref[table]
        sin = sin_ref[table]
        for hh in range(acc.shape[1] // HEAD_DIM):
            cols = slice(hh * HEAD_DIM, (hh + 1) * HEAD_DIM)
            t = acc[:, cols]
            main_ref[:, cols] = t * cos + pltpu.roll(t, half, 1) * sin

    @pl.when(j == 0)
    def _():
        prepare_row_block()
        project_rotated(0)

    @pl.when(j == 1)
    def _():
        project_rotated(1)

    @pl.when(jnp.logical_and(j >= n_rope_blocks, j < n_f32_blocks))
    def _():
        main_ref[...] = project()

    @pl.when(j >= n_f32_blocks)
    def _():
        gate_ref[...] = project().astype(BF16)


def _slab_specs(weights, n_steps, step_index):
    specs, shapes = [], []
    for w in weights:
        rows, cols = w.shape
        specs.append(pl.BlockSpec((rows // n_steps, cols), lambda *g: (step_index(*g), 0)))
        shapes.append(jax.ShapeDtypeStruct(w.shape, BF16))
    return specs, shapes


def _proj(x2, pos2, invf2, w_in_b, cast_weights):
    s, d = x2.shape
    tm, tn = PROJ_TM, PROJ_TN
    n_i, n_j = s // tm, IN_WIDTH // tn
    assert tn == ATTN_WIDTH
    n_rope_blocks = 2
    n_f32_blocks = MAIN_WIDTH // tn
    cast_specs, cast_shapes = _slab_specs(cast_weights, n_i * n_j, lambda i, j: i * n_j + j)
    main_spec = pl.BlockSpec((tm, tn), lambda i, j: (i, jnp.minimum(j, n_f32_blocks - 1)))
    gate_spec = pl.BlockSpec((tm, tn), lambda i, j: (i, jnp.maximum(j - n_f32_blocks, 0)))
    outs = pl.pallas_call(
        functools.partial(_proj_kernel, n_rope_blocks=n_rope_blocks, n_f32_blocks=n_f32_blocks,
                          n_cast=len(cast_weights)),
        grid=(n_i, n_j),
        in_specs=[
            pl.BlockSpec((tm, d), lambda i, j: (i, 0)),
            pl.BlockSpec((tm, 1), lambda i, j: (i, 0)),
            pl.BlockSpec((1, HEAD_DIM), lambda i, j: (0, 0)),
            pl.BlockSpec((d, tn), lambda i, j: (0, j)),
        ] + cast_specs,
        out_specs=[main_spec, gate_spec] + cast_specs,
        out_shape=[jax.ShapeDtypeStruct((s, MAIN_WIDTH), F32),
                   jax.ShapeDtypeStruct((s, IN_WIDTH - MAIN_WIDTH), BF16)] + cast_shapes,
        scratch_shapes=[
            pltpu.VMEM((tm, d), BF16),
            pltpu.VMEM((2, tm, HEAD_DIM), F32),
            pltpu.VMEM((2, tm, HEAD_DIM), F32),
        ],
        compiler_params=pltpu.CompilerParams(
            dimension_semantics=("arbitrary", "arbitrary"), vmem_limit_bytes=VMEM_LIMIT_BYTES),
        name="proj_rope",
    )(x2, pos2, invf2, w_in_b, *cast_weights)
    return outs[0], outs[1], outs[2:]


def _attn_kernel(q_ref, kc_ref, vc_ref, o_ref, bias_ref,
                 qd_refs, kd_refs, vd_refs, o_refs, lse_refs, stage_refs):
    n = pl.program_id(1)
    blk = SUB_BLOCK

    @pl.when(jnp.logical_and(n == 0, pl.program_id(0) == 0))
    def _():
        for kd_ref, vd_ref in zip(kd_refs, vd_refs):
            kd_ref[...] = jnp.zeros(kd_ref.shape, BF16)
            vd_ref[:, :HEAD_DIM] = jnp.zeros((vd_ref.shape[0], HEAD_DIM), BF16)
            vd_ref[:, HEAD_DIM:] = jnp.ones((vd_ref.shape[0], HEAD_DIM), BF16)
        qi = lax.broadcasted_iota(jnp.int32, (blk, 2 * blk), 0)
        kj = lax.broadcasted_iota(jnp.int32, (blk, 2 * blk), 1)
        band = (kj >= qi) & (kj <= qi + blk)
        bias_ref[0] = jnp.where(band, 0.0, MASK_VALUE).astype(F32)
        bias_ref[1] = jnp.where(band & (kj >= blk), 0.0, MASK_VALUE).astype(F32)

    names = ("q", "k", "v")
    src = dict(zip(names, (q_ref, kc_ref, vc_ref)))
    src_dil = 1
    stage = dict(zip(names, stage_refs))

    first_chunk = (n == 0).astype(jnp.int32)

    for p, dil in enumerate(DILATIONS):
        m_len = ATTN_CHUNK // dil
        nb = m_len // blk
        krows = blk + m_len
        qd_ref, kd_ref, vd_ref = qd_refs[p], kd_refs[p], vd_refs[p]

        f = dil // src_dil
        src_len = ATTN_CHUNK // src_dil
        keep_f32 = f > 1 and p + 1 < len(DILATIONS)

        def seg(name, r, m0, rows, f=f, src=src, src_dil=src_dil, src_len=src_len, m_len=m_len):
            if f == 1:
                return src[name][pl.ds(r * m_len + m0, rows), :]
            row0 = (r % src_dil) * src_len + r // src_dil + f * m0
            return src[name][pl.ds(row0, rows, stride=f), :]

        for r in range(dil):
            for name, dst, dst_rows in (("q", qd_ref, m_len), ("k", kd_ref, krows), ("v", vd_ref, krows)):
                cur = seg(name, r, 0, m_len)
                if keep_f32:
                    stage[name][r * m_len:(r + 1) * m_len, :] = cur
                if name == "q":
                    dst[r * m_len:(r + 1) * m_len, :] = cur.astype(BF16)
                    continue
                row0 = r * dst_rows
                dst[row0:row0 + blk, 0:HEAD_DIM] = dst[row0 + m_len:row0 + m_len + blk, 0:HEAD_DIM]
                dst[row0 + blk:row0 + dst_rows, 0:HEAD_DIM] = cur.astype(BF16)
        if keep_f32:
            src, src_dil = stage, dil

        for r in range(dil):
            for b in range(nb):
                q0 = r * m_len + b * blk
                k0 = r * krows + b * blk
                qb = qd_ref[q0:q0 + blk, :]
                kb = kd_ref[k0:k0 + 2 * blk, :]
                vb = vd_ref[k0:k0 + 2 * blk, :]
                s = lax.dot_general(qb, kb, (((1,), (1,)), ((), ())), preferred_element_type=F32)
                s = s + (bias_ref[first_chunk] if b == 0 else bias_ref[0])
                m = jnp.max(s, axis=-1, keepdims=True)
                pr = jnp.exp2(s - m)
                ext = jnp.dot(pr.astype(BF16), vb, preferred_element_type=F32)
                acc, l = ext[:, :HEAD_DIM], ext[:, HEAD_DIM:]
                rows = pl.ds(b * (blk * dil) + r, blk, stride=dil)
                o_refs[p][rows, :] = acc / l
                lse_refs[p][rows, :] = m + jnp.log2(l)

    for c in range(ATTN_CHUNK // blk):
        rows = slice(c * blk, (c + 1) * blk)
        lse = [ref[rows, :] for ref in lse_refs]
        top = functools.reduce(jnp.maximum, lse)
        w = [jnp.exp2(x - top) for x in lse]
        num = sum(wp * ref[rows, :] for wp, ref in zip(w, o_refs))
        o_ref[rows, :] = (num / sum(w)).astype(o_ref.dtype)


def _attn(h, cast_weights):
    s = h.shape[0]
    c = ATTN_CHUNK
    blk = SUB_BLOCK
    npat = len(DILATIONS)
    n_cast = len(cast_weights)
    n_chunks = s // c
    chunk = lambda col0: pl.BlockSpec((c, HEAD_DIM), lambda hh, n: (n, col0 + hh))
    cast_specs, cast_shapes = _slab_specs(cast_weights, n_chunks * ATTN_HEADS, lambda hh, n: hh * n_chunks + n)

    def body(q_ref, kc_ref, vc_ref, *rest):
        cast_in, o_ref, cast_out = rest[:n_cast], rest[n_cast], rest[n_cast + 1:2 * n_cast + 1]
        bias, scr = rest[2 * n_cast + 1], rest[2 * n_cast + 2:]
        for src, dst in zip(cast_in, cast_out):
            dst[...] = src[...].astype(BF16)
        groups = [scr[i * npat:(i + 1) * npat] for i in range(5)]
        _attn_kernel(q_ref, kc_ref, vc_ref, o_ref, bias, *groups, scr[5 * npat:])

    k_scratch = [pltpu.VMEM((dil * blk + c, HEAD_DIM), BF16) for dil in DILATIONS]
    v_scratch = [pltpu.VMEM((dil * blk + c, 2 * HEAD_DIM), BF16) for dil in DILATIONS]
    outs = pl.pallas_call(
        body,
        grid=(ATTN_HEADS, n_chunks),
        in_specs=[chunk(0), chunk(ATTN_HEADS), chunk(2 * ATTN_HEADS)] + cast_specs,
        out_specs=[pl.BlockSpec((c, HEAD_DIM), lambda hh, n: (n, hh))] + cast_specs,
        out_shape=[jax.ShapeDtypeStruct((s, ATTN_WIDTH), BF16)] + cast_shapes,
        scratch_shapes=(
            [pltpu.VMEM((2, blk, 2 * blk), F32)]
            + [pltpu.VMEM((c, HEAD_DIM), BF16)] * npat
            + k_scratch + v_scratch
            + [pltpu.VMEM((c, HEAD_DIM), F32)] * (2 * npat)
            + [pltpu.VMEM((c, HEAD_DIM), F32)] * 3),
        compiler_params=pltpu.CompilerParams(
            dimension_semantics=("arbitrary", "arbitrary"), vmem_limit_bytes=VMEM_LIMIT_BYTES),
        name="dilated_attn",
    )(h, h, h, *cast_weights)
    return outs[0], outs[1:]


def _mix_kernel(oat_ref, u_ref, uh_ref, ga_ref, gp_ref, x_ref, wpool_ref, pscale_ref,
                wba_ref, wbb_ref, wout_ref, g_ref, b_ref, out_ref, ubuf_ref, pm_ref, mg_ref, *lv_refs):
    i = pl.program_id(0)
    tm = u_ref.shape[0]
    halo = MAX_POOL_WINDOW

    pad = POOL_PAD
    top = pad + halo
    rows_all = top + tm

    @pl.when(i == 0)
    def _():
        ubuf_ref[0:top, :] = jnp.zeros((top, POOL_WIDTH), F32)
        for lv_ref in lv_refs:
            lv_ref[0:pad, :] = jnp.zeros((pad, POOL_GROUP_WIDTH), F32)

    @pl.when(i > 0)
    def _():
        ubuf_ref[pad:top, :] = uh_ref[...]

    ubuf_ref[top:rows_all, :] = u_ref[...]

    hm = tm // MIX_PARTS
    parts = [slice(part * hm, (part + 1) * hm) for part in range(MIX_PARTS)]
    n_chunks = D_MODEL // MIX_TN
    chunk_cols = [slice(c * MIX_TN, (c + 1) * MIX_TN) for c in range(n_chunks)]

    def pool_group(part, g):
        rows, w = parts[part], POOL_WINDOWS[g]
        lo, hi = pad + part * hm, top + (part + 1) * hm
        t_glob = i * tm + part * hm + lax.broadcasted_iota(jnp.int32, (hm, 1), 0)
        cols = slice(g * POOL_GROUP_WIDTH, (g + 1) * POOL_GROUP_WIDTH)
        cur, cur_cols, shift, level = ubuf_ref, cols, 1, 0
        while shift < w:
            wsum = cur[lo:hi, cur_cols] + cur[lo - shift:hi - shift, cur_cols]
            shift *= 2
            if shift < w:
                lv_refs[level % 2][lo:hi, :] = wsum
                cur, cur_cols, level = lv_refs[level % 2], slice(None), level + 1
        inv_count = 1.0 / jnp.minimum(t_glob + 1, w).astype(F32)
        pooled = wsum[halo:, :] * inv_count - u_ref[rows, cols]
        y = jnp.dot(pooled.astype(BF16), wpool_ref[g], preferred_element_type=F32)
        pm_ref[rows, cols] = (y * pscale_ref[:, cols]).astype(BF16)

    def attn_branch(part, c):
        rows, cols = parts[part], chunk_cols[c]
        y_attn = jnp.dot(oat_ref[rows, :], wba_ref[:, cols], preferred_element_type=F32)
        out_ref[rows, cols] = jax.nn.sigmoid(ga_ref[rows, cols].astype(F32)) * y_attn

    def pool_branch(part, c):
        rows, cols = parts[part], chunk_cols[c]
        y_pool = jnp.dot(pm_ref[rows, :], wbb_ref[:, cols], preferred_element_type=F32)
        merged = out_ref[rows, cols] + jax.nn.sigmoid(gp_ref[rows, cols].astype(F32)) * y_pool
        mg_ref[rows, cols] = merged.astype(BF16)

    def out_proj(part):
        rows = parts[part]
        out_ref[rows, :] = jnp.dot(mg_ref[rows, :], wout_ref[...], preferred_element_type=F32)

    def norm_rows(rows):
        out_ref[rows, :] = _layer_norm(DEEPNORM_ALPHA * x_ref[rows, :] + out_ref[rows, :], g_ref[...], b_ref[...])

    assert MIX_PARTS == 2 and n_chunks == len(POOL_WINDOWS)
    for part in range(MIX_PARTS):
        for c in range(n_chunks):
            attn_branch(part, c)
            pool_group(part, c)
    for c in range(n_chunks):
        pool_branch(0, c)
    out_proj(0)
    ln_rows = hm // n_chunks
    for c in range(n_chunks):
        pool_branch(1, c)
        norm_rows(slice(c * ln_rows, (c + 1) * ln_rows))
    out_proj(1)
    norm_rows(parts[1])


def _mix(o_attn, h_main, h_gates, x2, w_pool_b, pool_scale, w_ba_b, w_bb_b, w_out_b, ln_g, ln_b):
    s, d = x2.shape
    tm = MIX_TM
    halo = MAX_POOL_WINDOW
    u_col = (3 * ATTN_WIDTH) // POOL_WIDTH
    resident = lambda shape: pl.BlockSpec(shape, lambda i: (0,) * len(shape), pipeline_mode=pl.Buffered(1))
    return pl.pallas_call(
        _mix_kernel,
        grid=(s // tm,),
        in_specs=[
            pl.BlockSpec((tm, ATTN_WIDTH), lambda i: (i, 0)),
            pl.BlockSpec((tm, POOL_WIDTH), lambda i: (i, u_col)),
            pl.BlockSpec((halo, POOL_WIDTH), lambda i: (jnp.maximum(i * (tm // halo) - 1, 0), u_col)),
            pl.BlockSpec((tm, d), lambda i: (i, 0)),
            pl.BlockSpec((tm, d), lambda i: (i, 1)),
            pl.BlockSpec((tm, d), lambda i: (i, 0)),
            resident(w_pool_b.shape),
            resident((1, POOL_WIDTH)),
            resident(w_ba_b.shape),
            resident(w_bb_b.shape),
            resident(w_out_b.shape),
            resident((1, d)),
            resident((1, d)),
        ],
        out_specs=pl.BlockSpec((tm, d), lambda i: (i, 0)),
        out_shape=jax.ShapeDtypeStruct((s, d), F32),
        scratch_shapes=[
            pltpu.VMEM((POOL_PAD + halo + tm, POOL_WIDTH), F32),
            pltpu.VMEM((tm, POOL_WIDTH), BF16),
            pltpu.VMEM((tm, d), BF16),
            pltpu.VMEM((POOL_PAD + halo + tm, POOL_GROUP_WIDTH), F32),
            pltpu.VMEM((POOL_PAD + halo + tm, POOL_GROUP_WIDTH), F32),
        ],
        compiler_params=pltpu.CompilerParams(
            dimension_semantics=("arbitrary",), vmem_limit_bytes=VMEM_LIMIT_BYTES),
        name="mix_ln",
    )(o_attn, h_main, h_main, h_gates, h_gates, x2, w_pool_b, pool_scale, w_ba_b, w_bb_b, w_out_b, ln_g, ln_b)


def _ffn_kernel(x_ref, w1_ref, w2_ref, g_ref, b_ref, out_ref, xb_ref):
    j = pl.program_id(1)

    @pl.when(j == 0)
    def _():
        xb_ref[...] = x_ref[...].astype(BF16)
        out_ref[...] = jnp.zeros(out_ref.shape, F32)

    hid = jnp.dot(xb_ref[...], w1_ref[...], preferred_element_type=F32)
    hid = jnp.square(jnp.maximum(hid, 0.0)).astype(BF16)
    out_ref[...] += jnp.dot(hid, w2_ref[...], preferred_element_type=F32)

    @pl.when(j == pl.num_programs(1) - 1)
    def _():
        _residual_layer_norm(x_ref, out_ref, g_ref, b_ref, out_ref, FFN_LN_ROWS)


def _ffn(x1, w1_b, w2_b, ln_g, ln_b):
    s, d = x1.shape
    tm, tf = FFN_TM, FFN_TF
    return pl.pallas_call(
        _ffn_kernel,
        grid=(s // tm, D_FF // tf),
        in_specs=[
            pl.BlockSpec((tm, d), lambda i, j: (i, 0)),
            pl.BlockSpec((d, tf), lambda i, j: (0, j)),
            pl.BlockSpec((tf, d), lambda i, j: (j, 0)),
            pl.BlockSpec((1, d), lambda i, j: (0, 0)),
            pl.BlockSpec((1, d), lambda i, j: (0, 0)),
        ],
        out_specs=pl.BlockSpec((tm, d), lambda i, j: (i, 0)),
        out_shape=jax.ShapeDtypeStruct((s, d), F32),
        scratch_shapes=[pltpu.VMEM((tm, d), BF16)],
        compiler_params=pltpu.CompilerParams(
            dimension_semantics=("arbitrary", "arbitrary"), vmem_limit_bytes=VMEM_LIMIT_BYTES),
        name="ffn_ln",
    )(x1, w1_b, w2_b, ln_g, ln_b)


def kernel(x, positions, w_in, w_pool, pool_scale, w_branch_attn, w_branch_pool, w_out,
           ln_mix_g, ln_mix_b, w_ff1, w_ff2, ln_ff_g, ln_ff_b):
    b, s, d = x.shape
    assert (b, s, d) == (1, SEQ, D_MODEL) and w_in.shape[0] == DEPTH
    half = HEAD_DIM // 2
    inv_freq = ROPE_THETA ** (-jnp.arange(half, dtype=F32) / half)
    invf2 = jnp.concatenate([inv_freq, inv_freq]).reshape(1, HEAD_DIM)
    x2 = x.reshape(s, d)
    pos2 = positions.reshape(s, 1)
    for layer in range(DEPTH):
        w_pool2 = w_pool[layer].reshape(POOL_WIDTH, POOL_GROUP_WIDTH)
        h_main, h_gates, (w_ba_b, w_bb_b, w_out_b, w_pool_b) = _proj(
            x2, pos2, invf2, w_in[layer].astype(BF16),
            [w_branch_attn[layer], w_branch_pool[layer], w_out[layer], w_pool2])
        o_attn, (w1_b, w2_b) = _attn(h_main, [w_ff1[layer], w_ff2[layer]])
        x2 = _mix(o_attn, h_main, h_gates, x2, w_pool_b.reshape(w_pool[layer].shape),
                  pool_scale[layer].reshape(1, POOL_WIDTH), w_ba_b, w_bb_b, w_out_b,
                  ln_mix_g[layer].reshape(1, d), ln_mix_b[layer].reshape(1, d))
        x2 = _ffn(x2, w1_b, w2_b, ln_ff_g[layer].reshape(1, d), ln_ff_b[layer].reshape(1, d))
    return x2.reshape(b, s, d)
```

```python
import functools

import jax
import jax.numpy as jnp
from jax import lax
from jax.experimental import pallas as pl
from jax.experimental.pallas import tpu as pltpu

F32 = jnp.float32
BF16 = jnp.bfloat16

D_MODEL = 2048
SEQ = 8192
HEAD_DIM = 128
ATTN_WIDTH = D_MODEL // 2
ATTN_HEADS = ATTN_WIDTH // HEAD_DIM
POOL_WIDTH = D_MODEL // 2
POOL_WINDOWS = (2, 4, 8, 16)
POOL_GROUP_WIDTH = POOL_WIDTH // len(POOL_WINDOWS)
MAX_POOL_WINDOW = max(POOL_WINDOWS)
POOL_PAD = 8
DILATIONS = (1, 4, 16)
SUB_BLOCK = 128
D_FF = 4 * D_MODEL
IN_WIDTH = 3 * ATTN_WIDTH + POOL_WIDTH + 2 * D_MODEL
MAIN_WIDTH = 3 * ATTN_WIDTH + POOL_WIDTH
ROPE_THETA = 10000.0
LN_EPS = 1e-5
DEPTH = 1
DEEPNORM_ALPHA = (2.0 * DEPTH) ** 0.25
SM_SCALE = HEAD_DIM ** -0.5
LOG2_E = 1.4426950408889634
Q_SCALE = SM_SCALE * LOG2_E
MASK_VALUE = -1e30

VMEM_LIMIT_BYTES = 60 * 1024 * 1024

ATTN_CHUNK = max(DILATIONS) * SUB_BLOCK
PROJ_TM, PROJ_TN = 1024, 1024
MIX_TM, MIX_TN, MIX_PARTS = 512, 512, 2
FFN_TM, FFN_TF = 1024, 1024
FFN_LN_ROWS = 128


def _layer_norm(y, g, b):
    mu = jnp.mean(y, axis=-1, keepdims=True)
    yc = y - mu
    var = jnp.mean(yc * yc, axis=-1, keepdims=True)
    return yc * lax.rsqrt(var + LN_EPS) * g + b


def _residual_layer_norm(x_ref, y_ref, g_ref, b_ref, out_ref, chunk_rows):
    g = g_ref[...]
    b = b_ref[...]

    def body(c, carry):
        rows = pl.ds(pl.multiple_of(c * chunk_rows, chunk_rows), chunk_rows)
        out_ref[rows, :] = _layer_norm(DEEPNORM_ALPHA * x_ref[rows, :] + y_ref[rows, :], g, b)
        return carry

    lax.fori_loop(0, x_ref.shape[0] // chunk_rows, body, 0)


def _proj_kernel(x_ref, pos_ref, invf_ref, w_ref, *rest, n_rope_blocks, n_f32_blocks, n_cast):
    cast_in, (main_ref, gate_ref) = rest[:n_cast], rest[n_cast:n_cast + 2]
    cast_out = rest[n_cast + 2:2 * n_cast + 2]
    xb_ref, cos_ref, sin_ref = rest[2 * n_cast + 2:]
    j = pl.program_id(1)
    tm = x_ref.shape[0]
    half = HEAD_DIM // 2

    def prepare_row_block():
        xb_ref[...] = x_ref[...].astype(BF16)
        lane = lax.broadcasted_iota(jnp.int32, (1, HEAD_DIM), 1)
        low = lane < half
        pos = jnp.where(low, pos_ref[0:tm // 2, :], pos_ref[tm // 2:tm, :])
        ang = pos.astype(F32) * invf_ref[...]
        sign = jnp.where(low, -1.0, 1.0).astype(F32)
        for table_ref, tab in ((cos_ref, jnp.cos(ang)), (sin_ref, jnp.sin(ang))):
            swapped = pltpu.roll(tab, half, 1)
            scale = sign if table_ref is sin_ref else 1.0
            top = jnp.where(low, tab, swapped) * scale
            bottom = jnp.where(low, swapped, tab) * scale
            table_ref[1, 0:tm // 2, :] = top
            table_ref[1, tm // 2:tm, :] = bottom
            table_ref[0, 0:tm // 2, :] = top * Q_SCALE
            table_ref[0, tm // 2:tm, :] = bottom * Q_SCALE

    for src, dst in zip(cast_in, cast_out):
        dst[...] = src[...].astype(BF16)

    def project():
        return jnp.dot(xb_ref[...], w_ref[...], preferred_element_type=F32)

    def project_rotated(table):
        acc = project()
        cos = cos_ref[table]
        sin = sin_ref[table]
        for hh in range(acc.shape[1] // HEAD_DIM):
            cols = slice(hh * HEAD_DIM, (hh + 1) * HEAD_DIM)
            t = acc[:, cols]
            main_ref[:, cols] = t * cos + pltpu.roll(t, half, 1) * sin

    @pl.when(j == 0)
    def _():
        prepare_row_block()
        project_rotated(0)

    @pl.when(j == 1)
    def _():
        project_rotated(1)

    @pl.when(jnp.logical_and(j >= n_rope_blocks, j < n_f32_blocks))
    def _():
        main_ref[...] = project()

    @pl.when(j >= n_f32_blocks)
    def _():
        gate_ref[...] = project().astype(BF16)


def _slab_specs(weights, n_steps, step_index):
    specs, shapes = [], []
    for w in weights:
        rows, cols = w.shape
        specs.append(pl.BlockSpec((rows // n_steps, cols), lambda *g: (step_index(*g), 0)))
        shapes.append(jax.ShapeDtypeStruct(w.shape, BF16))
    return specs, shapes


def _proj(x2, pos2, invf2, w_in_b, cast_weights):
    s, d = x2.shape
    tm, tn = PROJ_TM, PROJ_TN
    n_i, n_j = s // tm, IN_WIDTH // tn
    assert tn == ATTN_WIDTH
    n_rope_blocks = 2
    n_f32_blocks = MAIN_WIDTH // tn
    cast_specs, cast_shapes = _slab_specs(cast_weights, n_i * n_j, lambda i, j: i * n_j + j)
    main_spec = pl.BlockSpec((tm, tn), lambda i, j: (i, jnp.minimum(j, n_f32_blocks - 1)))
    gate_spec = pl.BlockSpec((tm, tn), lambda i, j: (i, jnp.maximum(j - n_f32_blocks, 0)))
    outs = pl.pallas_call(
        functools.partial(_proj_kernel, n_rope_blocks=n_rope_blocks, n_f32_blocks=n_f32_blocks,
                          n_cast=len(cast_weights)),
        grid=(n_i, n_j),
        in_specs=[
            pl.BlockSpec((tm, d), lambda i, j: (i, 0)),
            pl.BlockSpec((tm, 1), lambda i, j: (i, 0)),
            pl.BlockSpec((1, HEAD_DIM), lambda i, j: (0, 0)),
            pl.BlockSpec((d, tn), lambda i, j: (0, j)),
        ] + cast_specs,
        out_specs=[main_spec, gate_spec] + cast_specs,
        out_shape=[jax.ShapeDtypeStruct((s, MAIN_WIDTH), F32),
                   jax.ShapeDtypeStruct((s, IN_WIDTH - MAIN_WIDTH), BF16)] + cast_shapes,
        scratch_shapes=[
            pltpu.VMEM((tm, d), BF16),
            pltpu.VMEM((2, tm, HEAD_DIM), F32),
            pltpu.VMEM((2, tm, HEAD_DIM), F32),
        ],
        compiler_params=pltpu.CompilerParams(
            dimension_semantics=("arbitrary", "arbitrary"), vmem_limit_bytes=VMEM_LIMIT_BYTES),
        name="proj_rope",
    )(x2, pos2, invf2, w_in_b, *cast_weights)
    return outs[0], outs[1], outs[2:]


def _attn_kernel(q_ref, kc_ref, vc_ref, o_ref, bias_ref,
                 qd_refs, kd_refs, vd_refs, o_refs, lse_refs, stage_refs):
    n = pl.program_id(1)
    blk = SUB_BLOCK

    @pl.when(jnp.logical_and(n == 0, pl.program_id(0) == 0))
    def _():
        for kd_ref, vd_ref in zip(kd_refs, vd_refs):
            kd_ref[...] = jnp.zeros(kd_ref.shape, BF16)
            vd_ref[:, :HEAD_DIM] = jnp.zeros((vd_ref.shape[0], HEAD_DIM), BF16)
            vd_ref[:, HEAD_DIM:] = jnp.ones((vd_ref.shape[0], HEAD_DIM), BF16)
        qi = lax.broadcasted_iota(jnp.int32, (blk, 2 * blk), 0)
        kj = lax.broadcasted_iota(jnp.int32, (blk, 2 * blk), 1)
        band = (kj >= qi) & (kj <= qi + blk)
        bias_ref[0] = jnp.where(band, 0.0, MASK_VALUE).astype(F32)
        bias_ref[1] = jnp.where(band & (kj >= blk), 0.0, MASK_VALUE).astype(F32)

    names = ("q", "k", "v")
    src = dict(zip(names, (q_ref, kc_ref, vc_ref)))
    src_dil = 1
    stage = dict(zip(names, stage_refs))

    first_chunk = (n == 0).astype(jnp.int32)

    for p, dil in enumerate(DILATIONS):
        m_len = ATTN_CHUNK // dil
        nb = m_len // blk
        krows = blk + m_len
        qd_ref, kd_ref, vd_ref = qd_refs[p], kd_refs[p], vd_refs[p]

        f = dil // src_dil
        src_len = ATTN_CHUNK // src_dil
        keep_f32 = f > 1 and p + 1 < len(DILATIONS)

        def seg(name, r, m0, rows, f=f, src=src, src_dil=src_dil, src_len=src_len, m_len=m_len):
            if f == 1:
                return src[name][pl.ds(r * m_len + m0, rows), :]
            row0 = (r % src_dil) * src_len + r // src_dil + f * m0
            return src[name][pl.ds(row0, rows, stride=f), :]

        for r in range(dil):
            for name, dst, dst_rows in (("q", qd_ref, m_len), ("k", kd_ref, krows), ("v", vd_ref, krows)):
                cur = seg(name, r, 0, m_len)
                if keep_f32:
                    stage[name][r * m_len:(r + 1) * m_len, :] = cur
                if name == "q":
                    dst[r * m_len:(r + 1) * m_len, :] = cur.astype(BF16)
                    continue
                row0 = r * dst_rows
                dst[row0:row0 + blk, 0:HEAD_DIM] = dst[row0 + m_len:row0 + m_len + blk, 0:HEAD_DIM]
                dst[row0 + blk:row0 + dst_rows, 0:HEAD_DIM] = cur.astype(BF16)
        if keep_f32:
            src, src_dil = stage, dil

        for r in range(dil):
            for b in range(nb):
                q0 = r * m_len + b * blk
                k0 = r * krows + b * blk
                qb = qd_ref[q0:q0 + blk, :]
                kb = kd_ref[k0:k0 + 2 * blk, :]
                vb = vd_ref[k0:k0 + 2 * blk, :]
                s = lax.dot_general(qb, kb, (((1,), (1,)), ((), ())), preferred_element_type=F32)
                s = s + (bias_ref[first_chunk] if b == 0 else bias_ref[0])
                m = jnp.max(s, axis=-1, keepdims=True)
                pr = jnp.exp2(s - m)
                ext = jnp.dot(pr.astype(BF16), vb, preferred_element_type=F32)
                acc, l = ext[:, :HEAD_DIM], ext[:, HEAD_DIM:]
                rows = pl.ds(b * (blk * dil) + r, blk, stride=dil)
                o_refs[p][rows, :] = acc / l
                lse_refs[p][rows, :] = m + jnp.log2(l)

    for c in range(ATTN_CHUNK // blk):
        rows = slice(c * blk, (c + 1) * blk)
        lse = [ref[rows, :] for ref in lse_refs]
        top = functools.reduce(jnp.maximum, lse)
        w = [jnp.exp2(x - top) for x in lse]
        num = sum(wp * ref[rows, :] for wp, ref in zip(w, o_refs))
        o_ref[rows, :] = (num / sum(w)).astype(o_ref.dtype)


def _attn(h, cast_weights):
    s = h.shape[0]
    c = ATTN_CHUNK
    blk = SUB_BLOCK
    npat = len(DILATIONS)
    n_cast = len(cast_weights)
    n_chunks = s // c
    chunk = lambda col0: pl.BlockSpec((c, HEAD_DIM), lambda hh, n: (n, col0 + hh))
    cast_specs, cast_shapes = _slab_specs(cast_weights, n_chunks * ATTN_HEADS, lambda hh, n: hh * n_chunks + n)

    def body(q_ref, kc_ref, vc_ref, *rest):
        cast_in, o_ref, cast_out = rest[:n_cast], rest[n_cast], rest[n_cast + 1:2 * n_cast + 1]
        bias, scr = rest[2 * n_cast + 1], rest[2 * n_cast + 2:]
        for src, dst in zip(cast_in, cast_out):
            dst[...] = src[...].astype(BF16)
        groups = [scr[i * npat:(i + 1) * npat] for i in range(5)]
        _attn_kernel(q_ref, kc_ref, vc_ref, o_ref, bias, *groups, scr[5 * npat:])

    k_scratch = [pltpu.VMEM((dil * blk + c, HEAD_DIM), BF16) for dil in DILATIONS]
    v_scratch = [pltpu.VMEM((dil * blk + c, 2 * HEAD_DIM), BF16) for dil in DILATIONS]
    outs = pl.pallas_call(
        body,
        grid=(ATTN_HEADS, n_chunks),
        in_specs=[chunk(0), chunk(ATTN_HEADS), chunk(2 * ATTN_HEADS)] + cast_specs,
        out_specs=[pl.BlockSpec((c, HEAD_DIM), lambda hh, n: (n, hh))] + cast_specs,
        out_shape=[jax.ShapeDtypeStruct((s, ATTN_WIDTH), BF16)] + cast_shapes,
        scratch_shapes=(
            [pltpu.VMEM((2, blk, 2 * blk), F32)]
            + [pltpu.VMEM((c, HEAD_DIM), BF16)] * npat
            + k_scratch + v_scratch
            + [pltpu.VMEM((c, HEAD_DIM), F32)] * (2 * npat)
            + [pltpu.VMEM((c, HEAD_DIM), F32)] * 3),
        compiler_params=pltpu.CompilerParams(
            dimension_semantics=("arbitrary", "arbitrary"), vmem_limit_bytes=VMEM_LIMIT_BYTES),
        name="dilated_attn",
    )(h, h, h, *cast_weights)
    return outs[0], outs[1:]


def _mix_kernel(oat_ref, u_ref, uh_ref, ga_ref, gp_ref, x_ref, wpool_ref, pscale_ref,
                wba_ref, wbb_ref, wout_ref, g_ref, b_ref, out_ref, ubuf_ref, pm_ref, mg_ref, *lv_refs):
    i = pl.program_id(0)
    tm = u_ref.shape[0]
    halo = MAX_POOL_WINDOW

    pad = POOL_PAD
    top = pad + halo
    rows_all = top + tm

    @pl.when(i == 0)
    def _():
        ubuf_ref[0:top, :] = jnp.zeros((top, POOL_WIDTH), F32)
        for lv_ref in lv_refs:
            lv_ref[0:pad, :] = jnp.zeros((pad, POOL_GROUP_WIDTH), F32)

    @pl.when(i > 0)
    def _():
        ubuf_ref[pad:top, :] = uh_ref[...]

    ubuf_ref[top:rows_all, :] = u_ref[...]

    hm = tm // MIX_PARTS
    parts = [slice(part * hm, (part + 1) * hm) for part in range(MIX_PARTS)]
    n_chunks = D_MODEL // MIX_TN
    chunk_cols = [slice(c * MIX_TN, (c + 1) * MIX_TN) for c in range(n_chunks)]

    def pool_group(part, g):
        rows, w = parts[part], POOL_WINDOWS[g]
        lo, hi = pad + part * hm, top + (part + 1) * hm
        t_glob = i * tm + part * hm + lax.broadcasted_iota(jnp.int32, (hm, 1), 0)
        cols = slice(g * POOL_GROUP_WIDTH, (g + 1) * POOL_GROUP_WIDTH)
        cur, cur_cols, shift, level = ubuf_ref, cols, 1, 0
        while shift < w:
            wsum = cur[lo:hi, cur_cols] + cur[lo - shift:hi - shift, cur_cols]
            shift *= 2
            if shift < w:
                lv_refs[level % 2][lo:hi, :] = wsum
                cur, cur_cols, level = lv_refs[level % 2], slice(None), level + 1
        inv_count = 1.0 / jnp.minimum(t_glob + 1, w).astype(F32)
        pooled = wsum[halo:, :] * inv_count - u_ref[rows, cols]
        y = jnp.dot(pooled.astype(BF16), wpool_ref[g], preferred_element_type=F32)
        pm_ref[rows, cols] = (y * pscale_ref[:, cols]).astype(BF16)

    def attn_branch(part, c):
        rows, cols = parts[part], chunk_cols[c]
        y_attn = jnp.dot(oat_ref[rows, :], wba_ref[:, cols], preferred_element_type=F32)
        out_ref[rows, cols] = jax.nn.sigmoid(ga_ref[rows, cols].astype(F32)) * y_attn

    def pool_branch(part, c):
        rows, cols = parts[part], chunk_cols[c]
        y_pool = jnp.dot(pm_ref[rows, :], wbb_ref[:, cols], preferred_element_type=F32)
        merged = out_ref[rows, cols] + jax.nn.sigmoid(gp_ref[rows, cols].astype(F32)) * y_pool
        mg_ref[rows, cols] = merged.astype(BF16)

    def out_proj(part):
        rows = parts[part]
        out_ref[rows, :] = jnp.dot(mg_ref[rows, :], wout_ref[...], preferred_element_type=F32)

    def norm_rows(rows):
        out_ref[rows, :] = _layer_norm(DEEPNORM_ALPHA * x_ref[rows, :] + out_ref[rows, :], g_ref[...], b_ref[...])

    assert MIX_PARTS == 2 and n_chunks == len(POOL_WINDOWS)
    for part in range(MIX_PARTS):
        for c in range(n_chunks):
            attn_branch(part, c)
            pool_group(part, c)
    for c in range(n_chunks):
        pool_branch(0, c)
    out_proj(0)
    ln_rows = hm // n_chunks
    for c in range(n_chunks):
        pool_branch(1, c)
        norm_rows(slice(c * ln_rows, (c + 1) * ln_rows))
    out_proj(1)
    norm_rows(parts[1])


def _mix(o_attn, h_main, h_gates, x2, w_pool_b, pool_scale, w_ba_b, w_bb_b, w_out_b, ln_g, ln_b):
    s, d = x2.shape
    tm = MIX_TM
    halo = MAX_POOL_WINDOW
    u_col = (3 * ATTN_WIDTH) // POOL_WIDTH
    resident = lambda shape: pl.BlockSpec(shape, lambda i: (0,) * len(shape), pipeline_mode=pl.Buffered(1))
    return pl.pallas_call(
        _mix_kernel,
        grid=(s // tm,),
        in_specs=[
            pl.BlockSpec((tm, ATTN_WIDTH), lambda i: (i, 0)),
            pl.BlockSpec((tm, POOL_WIDTH), lambda i: (i, u_col)),
            pl.BlockSpec((halo, POOL_WIDTH), lambda i: (jnp.maximum(i * (tm // halo) - 1, 0), u_col)),
            pl.BlockSpec((tm, d), lambda i: (i, 0)),
            pl.BlockSpec((tm, d), lambda i: (i, 1)),
            pl.BlockSpec((tm, d), lambda i: (i, 0)),
            resident(w_pool_b.shape),
            resident((1, POOL_WIDTH)),
            resident(w_ba_b.shape),
            resident(w_bb_b.shape),
            resident(w_out_b.shape),
            resident((1, d)),
            resident((1, d)),
        ],
        out_specs=pl.BlockSpec((tm, d), lambda i: (i, 0)),
        out_shape=jax.ShapeDtypeStruct((s, d), F32),
        scratch_shapes=[
            pltpu.VMEM((POOL_PAD + halo + tm, POOL_WIDTH), F32),
            pltpu.VMEM((tm, POOL_WIDTH), BF16),
            pltpu.VMEM((tm, d), BF16),
            pltpu.VMEM((POOL_PAD + halo + tm, POOL_GROUP_WIDTH), F32),
            pltpu.VMEM((POOL_PAD + halo + tm, POOL_GROUP_WIDTH), F32),
        ],
        compiler_params=pltpu.CompilerParams(
            dimension_semantics=("arbitrary",), vmem_limit_bytes=VMEM_LIMIT_BYTES),
        name="mix_ln",
    )(o_attn, h_main, h_main, h_gates, h_gates, x2, w_pool_b, pool_scale, w_ba_b, w_bb_b, w_out_b, ln_g, ln_b)


def _ffn_kernel(x_ref, w1_ref, w2_ref, g_ref, b_ref, out_ref, xb_ref):
    j = pl.program_id(1)

    @pl.when(j == 0)
    def _():
        xb_ref[...] = x_ref[...].astype(BF16)
        out_ref[...] = jnp.zeros(out_ref.shape, F32)

    hid = jnp.dot(xb_ref[...], w1_ref[...], preferred_element_type=F32)
    hid = jnp.square(jnp.maximum(hid, 0.0)).astype(BF16)
    out_ref[...] += jnp.dot(hid, w2_ref[...], preferred_element_type=F32)

    @pl.when(j == pl.num_programs(1) - 1)
    def _():
        _residual_layer_norm(x_ref, out_ref, g_ref, b_ref, out_ref, FFN_LN_ROWS)


def _ffn(x1, w1_b, w2_b, ln_g, ln_b):
    s, d = x1.shape
    tm, tf = FFN_TM, FFN_TF
    return pl.pallas_call(
        _ffn_kernel,
        grid=(s // tm, D_FF // tf),
        in_specs=[
            pl.BlockSpec((tm, d), lambda i, j: (i, 0)),
            pl.BlockSpec((d, tf), lambda i, j: (0, j)),
            pl.BlockSpec((tf, d), lambda i, j: (j, 0)),
            pl.BlockSpec((1, d), lambda i, j: (0, 0)),
            pl.BlockSpec((1, d), lambda i, j: (0, 0)),
        ],
        out_specs=pl.BlockSpec((tm, d), lambda i, j: (i, 0)),
        out_shape=jax.ShapeDtypeStruct((s, d), F32),
        scratch_shapes=[pltpu.VMEM((tm, d), BF16)],
        compiler_params=pltpu.CompilerParams(
            dimension_semantics=("arbitrary", "arbitrary"), vmem_limit_bytes=VMEM_LIMIT_BYTES),
        name="ffn_ln",
    )(x1, w1_b, w2_b, ln_g, ln_b)


def kernel(x, positions, w_in, w_pool, pool_scale, w_branch_attn, w_branch_pool, w_out,
           ln_mix_g, ln_mix_b, w_ff1, w_ff2, ln_ff_g, ln_ff_b):
    b, s, d = x.shape
    assert (b, s, d) == (1, SEQ, D_MODEL) and w_in.shape[0] == DEPTH
    half = HEAD_DIM // 2
    inv_freq = ROPE_THETA ** (-jnp.arange(half, dtype=F32) / half)
    invf2 = jnp.concatenate([inv_freq, inv_freq]).reshape(1, HEAD_DIM)
    x2 = x.reshape(s, d)
    pos2 = positions.reshape(s, 1)
    for layer in range(DEPTH):
        w_pool2 = w_pool[layer].reshape(POOL_WIDTH, POOL_GROUP_WIDTH)
        h_main, h_gates, (w_ba_b, w_bb_b, w_out_b, w_pool_b, w1_b, w2_b) = _proj(
            x2, pos2, invf2, w_in[layer].astype(BF16),
            [w_branch_attn[layer], w_branch_pool[layer], w_out[layer], w_pool2, w_ff1[layer], w_ff2[layer]])
        o_attn, _ = _attn(h_main, [])
        x2 = _mix(o_attn, h_main, h_gates, x2, w_pool_b.reshape(w_pool[layer].shape),
                  pool_scale[layer].reshape(1, POOL_WIDTH), w_ba_b, w_bb_b, w_out_b,
                  ln_mix_g[layer].reshape(1, d), ln_mix_b[layer].reshape(1, d))
        x2 = _ffn(x2, w1_b, w2_b, ln_ff_g[layer].reshape(1, d), ln_ff_b[layer].reshape(1, d))
    return x2.reshape(b, s, d)
```

```python
import functools

import jax
import jax.numpy as jnp
from jax import lax
from jax.experimental import pallas as pl
from jax.experimental.pallas import tpu as pltpu

F32 = jnp.float32
BF16 = jnp.bfloat16

D_MODEL = 2048
SEQ = 8192
HEAD_DIM = 128
ATTN_WIDTH = D_MODEL // 2
ATTN_HEADS = ATTN_WIDTH // HEAD_DIM
POOL_WIDTH = D_MODEL // 2
POOL_WINDOWS = (2, 4, 8, 16)
POOL_GROUP_WIDTH = POOL_WIDTH // len(POOL_WINDOWS)
MAX_POOL_WINDOW = max(POOL_WINDOWS)
POOL_PAD = 8
DILATIONS = (1, 4, 16)
SUB_BLOCK = 128
D_FF = 4 * D_MODEL
IN_WIDTH = 3 * ATTN_WIDTH + POOL_WIDTH + 2 * D_MODEL
MAIN_WIDTH = 3 * ATTN_WIDTH + POOL_WIDTH
ROPE_THETA = 10000.0
LN_EPS = 1e-5
DEPTH = 1
DEEPNORM_ALPHA = (2.0 * DEPTH) ** 0.25
SM_SCALE = HEAD_DIM ** -0.5
LOG2_E = 1.4426950408889634
Q_SCALE = SM_SCALE * LOG2_E
MASK_VALUE = -1e30

VMEM_LIMIT_BYTES = 60 * 1024 * 1024

ATTN_CHUNK = max(DILATIONS) * SUB_BLOCK
PROJ_TM, PROJ_TN = 1024, 1024
MIX_TM, MIX_TN, MIX_PARTS = 512, 512, 2
FFN_TM, FFN_TF = 1024, 1024
FFN_LN_ROWS = 128


def _layer_norm(y, g, b):
    mu = jnp.mean(y, axis=-1, keepdims=True)
    yc = y - mu
    var = jnp.mean(yc * yc, axis=-1, keepdims=True)
    return yc * lax.rsqrt(var + LN_EPS) * g + b


def _residual_layer_norm(x_ref, y_ref, g_ref, b_ref, out_ref, chunk_rows):
    g = g_ref[...]
    b = b_ref[...]

    def body(c, carry):
        rows = pl.ds(pl.multiple_of(c * chunk_rows, chunk_rows), chunk_rows)
        out_ref[rows, :] = _layer_norm(DEEPNORM_ALPHA * x_ref[rows, :] + y_ref[rows, :], g, b)
        return carry

    lax.fori_loop(0, x_ref.shape[0] // chunk_rows, body, 0)


def _proj_kernel(x_ref, pos_ref, invf_ref, w_ref, *rest, n_rope_blocks, n_f32_blocks, n_cast):
    cast_in, (main_ref, gate_ref) = rest[:n_cast], rest[n_cast:n_cast + 2]
    cast_out = rest[n_cast + 2:2 * n_cast + 2]
    xb_ref, cos_ref, sin_ref = rest[2 * n_cast + 2:]
    j = pl.program_id(1)
    tm = x_ref.shape[0]
    half = HEAD_DIM // 2

    def prepare_row_block():
        xb_ref[...] = x_ref[...].astype(BF16)
        lane = lax.broadcasted_iota(jnp.int32, (1, HEAD_DIM), 1)
        low = lane < half
        pos = jnp.where(low, pos_ref[0:tm // 2, :], pos_ref[tm // 2:tm, :])
        ang = pos.astype(F32) * invf_ref[...]
        sign = jnp.where(low, -1.0, 1.0).astype(F32)
        for table_ref, tab in ((cos_ref, jnp.cos(ang)), (sin_ref, jnp.sin(ang))):
            swapped = pltpu.roll(tab, half, 1)
            scale = sign if table_ref is sin_ref else 1.0
            top = jnp.where(low, tab, swapped) * scale
            bottom = jnp.where(low, swapped, tab) * scale
            table_ref[1, 0:tm // 2, :] = top
            table_ref[1, tm // 2:tm, :] = bottom
            table_ref[0, 0:tm // 2, :] = top * Q_SCALE
            table_ref[0, tm // 2:tm, :] = bottom * Q_SCALE

    for src, dst in zip(cast_in, cast_out):
        dst[...] = src[...].astype(BF16)

    def project():
        return jnp.dot(xb_ref[...], w_ref[...], preferred_element_type=F32)

    def project_rotated(table):
        acc = project()
        cos = cos_ref[table]
        sin = sin_ref[table]
        for hh in range(acc.shape[1] // HEAD_DIM):
            cols = slice(hh * HEAD_DIM, (hh + 1) * HEAD_DIM)
            t = acc[:, cols]
            main_ref[:, cols] = t * cos + pltpu.roll(t, half, 1) * sin

    @pl.when(j == 0)
    def _():
        prepare_row_block()
        project_rotated(0)

    @pl.when(j == 1)
    def _():
        project_rotated(1)

    @pl.when(jnp.logical_and(j >= n_rope_blocks, j < n_f32_blocks))
    def _():
        main_ref[...] = project()

    @pl.when(j >= n_f32_blocks)
    def _():
        gate_ref[...] = project().astype(BF16)


def _slab_specs(weights, n_steps, step_index):
    specs, shapes = [], []
    for w in weights:
        rows, cols = w.shape
        specs.append(pl.BlockSpec((rows // n_steps, cols), lambda *g: (step_index(*g), 0)))
        shapes.append(jax.ShapeDtypeStruct(w.shape, BF16))
    return specs, shapes


def _proj(x2, pos2, invf2, w_in_b, cast_weights):
    s, d = x2.shape
    tm, tn = PROJ_TM, PROJ_TN
    n_i, n_j = s // tm, IN_WIDTH // tn
    assert tn == ATTN_WIDTH
    n_rope_blocks = 2
    n_f32_blocks = MAIN_WIDTH // tn
    cast_specs, cast_shapes = _slab_specs(cast_weights, n_i * n_j, lambda i, j: i * n_j + j)
    main_spec = pl.BlockSpec((tm, tn), lambda i, j: (i, jnp.minimum(j, n_f32_blocks - 1)))
    gate_spec = pl.BlockSpec((tm, tn), lambda i, j: (i, jnp.maximum(j - n_f32_blocks, 0)))
    outs = pl.pallas_call(
        functools.partial(_proj_kernel, n_rope_blocks=n_rope_blocks, n_f32_blocks=n_f32_blocks,
                          n_cast=len(cast_weights)),
        grid=(n_i, n_j),
        in_specs=[
            pl.BlockSpec((tm, d), lambda i, j: (i, 0)),
            pl.BlockSpec((tm, 1), lambda i, j: (i, 0)),
            pl.BlockSpec((1, HEAD_DIM), lambda i, j: (0, 0)),
            pl.BlockSpec((d, tn), lambda i, j: (0, j)),
        ] + cast_specs,
        out_specs=[main_spec, gate_spec] + cast_specs,
        out_shape=[jax.ShapeDtypeStruct((s, MAIN_WIDTH), F32),
                   jax.ShapeDtypeStruct((s, IN_WIDTH - MAIN_WIDTH), BF16)] + cast_shapes,
        scratch_shapes=[
            pltpu.VMEM((tm, d), BF16),
            pltpu.VMEM((2, tm, HEAD_DIM), F32),
            pltpu.VMEM((2, tm, HEAD_DIM), F32),
        ],
        compiler_params=pltpu.CompilerParams(
            dimension_semantics=("arbitrary", "arbitrary"), vmem_limit_bytes=VMEM_LIMIT_BYTES),
        name="proj_rope",
    )(x2, pos2, invf2, w_in_b, *cast_weights)
    return outs[0], outs[1], outs[2:]


def _attn_kernel(q_ref, kc_ref, vc_ref, o_ref, bias_ref,
                 qd_refs, kd_refs, vd_refs, o_refs, lse_refs, stage_refs):
    n = pl.program_id(1)
    blk = SUB_BLOCK

    @pl.when(jnp.logical_and(n == 0, pl.program_id(0) == 0))
    def _():
        for kd_ref, vd_ref in zip(kd_refs, vd_refs):
            kd_ref[...] = jnp.zeros(kd_ref.shape, BF16)
            vd_ref[:, :HEAD_DIM] = jnp.zeros((vd_ref.shape[0], HEAD_DIM), BF16)
            vd_ref[:, HEAD_DIM:] = jnp.ones((vd_ref.shape[0], HEAD_DIM), BF16)
        qi = lax.broadcasted_iota(jnp.int32, (blk, 2 * blk), 0)
        kj = lax.broadcasted_iota(jnp.int32, (blk, 2 * blk), 1)
        band = (kj >= qi) & (kj <= qi + blk)
        bias_ref[0] = jnp.where(band, 0.0, MASK_VALUE).astype(F32)
        bias_ref[1] = jnp.where(band & (kj >= blk), 0.0, MASK_VALUE).astype(F32)

    names = ("q", "k", "v")
    src = dict(zip(names, (q_ref, kc_ref, vc_ref)))
    src_dil = 1
    stage = dict(zip(names, stage_refs))

    first_chunk = (n == 0).astype(jnp.int32)

    for p, dil in enumerate(DILATIONS):
        m_len = ATTN_CHUNK // dil
        nb = m_len // blk
        krows = blk + m_len
        qd_ref, kd_ref, vd_ref = qd_refs[p], kd_refs[p], vd_refs[p]

        f = dil // src_dil
        src_len = ATTN_CHUNK // src_dil
        keep_f32 = f > 1 and p + 1 < len(DILATIONS)

        def seg(name, r, m0, rows, f=f, src=src, src_dil=src_dil, src_len=src_len, m_len=m_len):
            if f == 1:
                return src[name][pl.ds(r * m_len + m0, rows), :]
            row0 = (r % src_dil) * src_len + r // src_dil + f * m0
            return src[name][pl.ds(row0, rows, stride=f), :]

        for r in range(dil):
            for name, dst, dst_rows in (("q", qd_ref, m_len), ("k", kd_ref, krows), ("v", vd_ref, krows)):
                cur = seg(name, r, 0, m_len)
                if keep_f32:
                    stage[name][r * m_len:(r + 1) * m_len, :] = cur
                if name == "q":
                    dst[r * m_len:(r + 1) * m_len, :] = cur.astype(BF16)
                    continue
                row0 = r * dst_rows
                dst[row0:row0 + blk, 0:HEAD_DIM] = dst[row0 + m_len:row0 + m_len + blk, 0:HEAD_DIM]
                dst[row0 + blk:row0 + dst_rows, 0:HEAD_DIM] = cur.astype(BF16)
        if keep_f32:
            src, src_dil = stage, dil

        for r in range(dil):
            for b in range(nb):
                q0 = r * m_len + b * blk
                k0 = r * krows + b * blk
                qb = qd_ref[q0:q0 + blk, :]
                kb = kd_ref[k0:k0 + 2 * blk, :]
                vb = vd_ref[k0:k0 + 2 * blk, :]
                s = lax.dot_general(qb, kb, (((1,), (1,)), ((), ())), preferred_element_type=F32)
                s = s + (bias_ref[first_chunk] if b == 0 else bias_ref[0])
                m = jnp.max(s, axis=-1, keepdims=True)
                pr = jnp.exp2(s - m)
                ext = jnp.dot(pr.astype(BF16), vb, preferred_element_type=F32)
                acc, l = ext[:, :HEAD_DIM], ext[:, HEAD_DIM:]
                rows = pl.ds(b * (blk * dil) + r, blk, stride=dil)
                o_refs[p][rows, :] = acc / l
                lse_refs[p][rows, :] = m + jnp.log2(l)

    for c in range(ATTN_CHUNK // blk):
        rows = slice(c * blk, (c + 1) * blk)
        lse = [ref[rows, :] for ref in lse_refs]
        top = functools.reduce(jnp.maximum, lse)
        w = [jnp.exp2(x - top) for x in lse]
        num = sum(wp * ref[rows, :] for wp, ref in zip(w, o_refs))
        o_ref[rows, :] = (num / sum(w)).astype(o_ref.dtype)


def _attn(h, cast_weights):
    s = h.shape[0]
    c = ATTN_CHUNK
    blk = SUB_BLOCK
    npat = len(DILATIONS)
    n_cast = len(cast_weights)
    n_chunks = s // c
    chunk = lambda col0: pl.BlockSpec((c, HEAD_DIM), lambda hh, n: (n, col0 + hh))
    cast_specs, cast_shapes = _slab_specs(cast_weights, n_chunks * ATTN_HEADS, lambda hh, n: hh * n_chunks + n)

    def body(q_ref, kc_ref, vc_ref, *rest):
        cast_in, o_ref, cast_out = rest[:n_cast], rest[n_cast], rest[n_cast + 1:2 * n_cast + 1]
        bias, scr = rest[2 * n_cast + 1], rest[2 * n_cast + 2:]
        for src, dst in zip(cast_in, cast_out):
            dst[...] = src[...].astype(BF16)
        groups = [scr[i * npat:(i + 1) * npat] for i in range(5)]
        _attn_kernel(q_ref, kc_ref, vc_ref, o_ref, bias, *groups, scr[5 * npat:])

    k_scratch = [pltpu.VMEM((dil * blk + c, HEAD_DIM), BF16) for dil in DILATIONS]
    v_scratch = [pltpu.VMEM((dil * blk + c, 2 * HEAD_DIM), BF16) for dil in DILATIONS]
    outs = pl.pallas_call(
        body,
        grid=(ATTN_HEADS, n_chunks),
        in_specs=[chunk(0), chunk(ATTN_HEADS), chunk(2 * ATTN_HEADS)] + cast_specs,
        out_specs=[pl.BlockSpec((c, HEAD_DIM), lambda hh, n: (n, hh))] + cast_specs,
        out_shape=[jax.ShapeDtypeStruct((s, ATTN_WIDTH), BF16)] + cast_shapes,
        scratch_shapes=(
            [pltpu.VMEM((2, blk, 2 * blk), F32)]
            + [pltpu.VMEM((c, HEAD_DIM), BF16)] * npat
            + k_scratch + v_scratch
            + [pltpu.VMEM((c, HEAD_DIM), F32)] * (2 * npat)
            + [pltpu.VMEM((c, HEAD_DIM), F32)] * 3),
        compiler_params=pltpu.CompilerParams(
            dimension_semantics=("arbitrary", "arbitrary"), vmem_limit_bytes=VMEM_LIMIT_BYTES),
        name="dilated_attn",
    )(h, h, h, *cast_weights)
    return outs[0], outs[1:]


def _mix_kernel(oat_ref, u_ref, uh_ref, ga_ref, gp_ref, x_ref, wpool_ref, pscale_ref,
                wba_ref, wbb_ref, wout_ref, g_ref, b_ref, out_ref, ubuf_ref, pm_ref, mg_ref, *lv_refs):
    i = pl.program_id(0)
    tm = u_ref.shape[0]
    halo = MAX_POOL_WINDOW

    pad = POOL_PAD
    top = pad + halo
    rows_all = top + tm

    @pl.when(i == 0)
    def _():
        ubuf_ref[0:top, :] = jnp.zeros((top, POOL_WIDTH), F32)
        for lv_ref in lv_refs:
            lv_ref[0:pad, :] = jnp.zeros((pad, POOL_GROUP_WIDTH), F32)

    @pl.when(i > 0)
    def _():
        ubuf_ref[pad:top, :] = uh_ref[...]

    ubuf_ref[top:rows_all, :] = u_ref[...]

    hm = tm // MIX_PARTS
    parts = [slice(part * hm, (part + 1) * hm) for part in range(MIX_PARTS)]
    n_chunks = D_MODEL // MIX_TN
    chunk_cols = [slice(c * MIX_TN, (c + 1) * MIX_TN) for c in range(n_chunks)]

    def pool_group(part, g):
        rows, w = parts[part], POOL_WINDOWS[g]
        lo, hi = pad + part * hm, top + (part + 1) * hm
        t_glob = i * tm + part * hm + lax.broadcasted_iota(jnp.int32, (hm, 1), 0)
        cols = slice(g * POOL_GROUP_WIDTH, (g + 1) * POOL_GROUP_WIDTH)
        cur, cur_cols, shift, level = ubuf_ref, cols, 1, 0
        while shift < w:
            wsum = cur[lo:hi, cur_cols] + cur[lo - shift:hi - shift, cur_cols]
            shift *= 2
            if shift < w:
                lv_refs[level % 2][lo:hi, :] = wsum
                cur, cur_cols, level = lv_refs[level % 2], slice(None), level + 1
        inv_count = 1.0 / jnp.minimum(t_glob + 1, w).astype(F32)
        pooled = wsum[halo:, :] * inv_count - u_ref[rows, cols]
        y = jnp.dot(pooled.astype(BF16), wpool_ref[g], preferred_element_type=F32)
        pm_ref[rows, cols] = (y * pscale_ref[:, cols]).astype(BF16)

    def attn_branch(part, c):
        rows, cols = parts[part], chunk_cols[c]
        y_attn = jnp.dot(oat_ref[rows, :], wba_ref[:, cols], preferred_element_type=F32)
        out_ref[rows, cols] = jax.nn.sigmoid(ga_ref[rows, cols].astype(F32)) * y_attn

    def pool_branch(part, c):
        rows, cols = parts[part], chunk_cols[c]
        y_pool = jnp.dot(pm_ref[rows, :], wbb_ref[:, cols], preferred_element_type=F32)
        merged = out_ref[rows, cols] + jax.nn.sigmoid(gp_ref[rows, cols].astype(F32)) * y_pool
        mg_ref[rows, cols] = merged.astype(BF16)

    def out_proj(part):
        rows = parts[part]
        out_ref[rows, :] = jnp.dot(mg_ref[rows, :], wout_ref[...], preferred_element_type=F32)

    def norm_rows(rows):
        out_ref[rows, :] = _layer_norm(DEEPNORM_ALPHA * x_ref[rows, :] + out_ref[rows, :], g_ref[...], b_ref[...])

    assert MIX_PARTS == 2 and n_chunks == len(POOL_WINDOWS)
    for part in range(MIX_PARTS):
        for c in range(n_chunks):
            attn_branch(part, c)
            pool_group(part, c)
    for c in range(n_chunks):
        pool_branch(0, c)
    out_proj(0)
    ln_rows = hm // n_chunks
    for c in range(n_chunks):
        pool_branch(1, c)
        norm_rows(slice(c * ln_rows, (c + 1) * ln_rows))
    out_proj(1)
    norm_rows(parts[1])


def _mix(o_attn, h_main, h_gates, x2, w_pool_b, pool_scale, w_ba_b, w_bb_b, w_out_b, ln_g, ln_b):
    s, d = x2.shape
    tm = MIX_TM
    halo = MAX_POOL_WINDOW
    u_col = (3 * ATTN_WIDTH) // POOL_WIDTH
    resident = lambda shape: pl.BlockSpec(shape, lambda i: (0,) * len(shape), pipeline_mode=pl.Buffered(1))
    return pl.pallas_call(
        _mix_kernel,
        grid=(s // tm,),
        in_specs=[
            pl.BlockSpec((tm, ATTN_WIDTH), lambda i: (i, 0)),
            pl.BlockSpec((tm, POOL_WIDTH), lambda i: (i, u_col)),
            pl.BlockSpec((halo, POOL_WIDTH), lambda i: (jnp.maximum(i * (tm // halo) - 1, 0), u_col)),
            pl.BlockSpec((tm, d), lambda i: (i, 0)),
            pl.BlockSpec((tm, d), lambda i: (i, 1)),
            pl.BlockSpec((tm, d), lambda i: (i, 0)),
            resident(w_pool_b.shape),
            resident((1, POOL_WIDTH)),
            resident(w_ba_b.shape),
            resident(w_bb_b.shape),
            resident(w_out_b.shape),
            resident((1, d)),
            resident((1, d)),
        ],
        out_specs=pl.BlockSpec((tm, d), lambda i: (i, 0)),
        out_shape=jax.ShapeDtypeStruct((s, d), F32),
        scratch_shapes=[
            pltpu.VMEM((POOL_PAD + halo + tm, POOL_WIDTH), F32),
            pltpu.VMEM((tm, POOL_WIDTH), BF16),
            pltpu.VMEM((tm, d), BF16),
            pltpu.VMEM((POOL_PAD + halo + tm, POOL_GROUP_WIDTH), F32),
            pltpu.VMEM((POOL_PAD + halo + tm, POOL_GROUP_WIDTH), F32),
        ],
        compiler_params=pltpu.CompilerParams(
            dimension_semantics=("arbitrary",), vmem_limit_bytes=VMEM_LIMIT_BYTES),
        name="mix_ln",
    )(o_attn, h_main, h_main, h_gates, h_gates, x2, w_pool_b, pool_scale, w_ba_b, w_bb_b, w_out_b, ln_g, ln_b)


def _ffn_kernel(x_ref, w1_ref, w2_ref, g_ref, b_ref, out_ref, xb_ref):
    j = pl.program_id(1)

    @pl.when(j == 0)
    def _():
        xb_ref[...] = x_ref[...].astype(BF16)
        out_ref[...] = jnp.zeros(out_ref.shape, F32)

    hid = jnp.dot(xb_ref[...], w1_ref[...], preferred_element_type=F32)
    hid = jnp.square(jnp.maximum(hid, 0.0)).astype(BF16)
    out_ref[...] += jnp.dot(hid, w2_ref[...], preferred_element_type=F32)

    @pl.when(j == pl.num_programs(1) - 1)
    def _():
        _residual_layer_norm(x_ref, out_ref, g_ref, b_ref, out_ref, FFN_LN_ROWS)


def _ffn(x1, w1_b, w2_b, ln_g, ln_b):
    s, d = x1.shape
    tm, tf = FFN_TM, FFN_TF
    return pl.pallas_call(
        _ffn_kernel,
        grid=(s // tm, D_FF // tf),
        in_specs=[
            pl.BlockSpec((tm, d), lambda i, j: (i, 0)),
            pl.BlockSpec((d, tf), lambda i, j: (0, j)),
            pl.BlockSpec((tf, d), lambda i, j: (j, 0)),
            pl.BlockSpec((1, d), lambda i, j: (0, 0)),
            pl.BlockSpec((1, d), lambda i, j: (0, 0)),
        ],
        out_specs=pl.BlockSpec((tm, d), lambda i, j: (i, 0)),
        out_shape=jax.ShapeDtypeStruct((s, d), F32),
        scratch_shapes=[pltpu.VMEM((tm, d), BF16)],
        compiler_params=pltpu.CompilerParams(
            dimension_semantics=("arbitrary", "arbitrary"), vmem_limit_bytes=VMEM_LIMIT_BYTES),
        name="ffn_ln",
    )(x1, w1_b, w2_b, ln_g, ln_b)


def kernel(x, positions, w_in, w_pool, pool_scale, w_branch_attn, w_branch_pool, w_out,
           ln_mix_g, ln_mix_b, w_ff1, w_ff2, ln_ff_g, ln_ff_b):
    b, s, d = x.shape
    assert (b, s, d) == (1, SEQ, D_MODEL) and w_in.shape[0] == DEPTH
    half = HEAD_DIM // 2
    inv_freq = ROPE_THETA ** (-jnp.arange(half, dtype=F32) / half)
    invf2 = jnp.concatenate([inv_freq, inv_freq]).reshape(1, HEAD_DIM)
    x2 = x.reshape(s, d)
    pos2 = positions.reshape(s, 1)
    for layer in range(DEPTH):
        w_pool2 = w_pool[layer].reshape(POOL_WIDTH, POOL_GROUP_WIDTH)
        h_main, h_gates, _ = _proj(x2, pos2, invf2, w_in[layer].astype(BF16), [])
        o_attn, (w_ba_b, w_bb_b, w_out_b, w_pool_b, w1_b, w2_b) = _attn(
            h_main, [w_branch_attn[layer], w_branch_pool[layer], w_out[layer], w_pool2, w_ff1[layer], w_ff2[layer]])
        x2 = _mix(o_attn, h_main, h_gates, x2, w_pool_b.reshape(w_pool[layer].shape),
                  pool_scale[layer].reshape(1, POOL_WIDTH), w_ba_b, w_bb_b, w_out_b,
                  ln_mix_g[layer].reshape(1, d), ln_mix_b[layer].reshape(1, d))
        x2 = _ffn(x2, w1_b, w2_b, ln_ff_g[layer].reshape(1, d), ln_ff_b[layer].reshape(1, d))
    return x2.reshape(b, s, d)
```

```python
import functools

import jax
import jax.numpy as jnp
from jax import lax
from jax.experimental import pallas as pl
from jax.experimental.pallas import tpu as pltpu

F32 = jnp.float32
BF16 = jnp.bfloat16

D_MODEL = 2048
SEQ = 8192
HEAD_DIM = 128
ATTN_WIDTH = D_MODEL // 2
ATTN_HEADS = ATTN_WIDTH // HEAD_DIM
POOL_WIDTH = D_MODEL // 2
POOL_WINDOWS = (2, 4, 8, 16)
POOL_GROUP_WIDTH = POOL_WIDTH // len(POOL_WINDOWS)
MAX_POOL_WINDOW = max(POOL_WINDOWS)
POOL_PAD = 8
DILATIONS = (1, 4, 16)
SUB_BLOCK = 128
D_FF = 4 * D_MODEL
IN_WIDTH = 3 * ATTN_WIDTH + POOL_WIDTH + 2 * D_MODEL
MAIN_WIDTH = 3 * ATTN_WIDTH + POOL_WIDTH
ROPE_THETA = 10000.0
LN_EPS = 1e-5
DEPTH = 1
DEEPNORM_ALPHA = (2.0 * DEPTH) ** 0.25
SM_SCALE = HEAD_DIM ** -0.5
LOG2_E = 1.4426950408889634
Q_SCALE = SM_SCALE * LOG2_E
MASK_VALUE = -1e30

VMEM_LIMIT_BYTES = 60 * 1024 * 1024

ATTN_CHUNK = max(DILATIONS) * SUB_BLOCK
PROJ_TM, PROJ_TN, PROJ_X_PARTS = 1024, 1024, 4
MIX_TM, MIX_TN, MIX_PARTS = 512, 512, 2
FFN_TM, FFN_TF = 1024, 1024
FFN_LN_ROWS = 128


def _layer_norm(y, g, b):
    mu = jnp.mean(y, axis=-1, keepdims=True)
    yc = y - mu
    var = jnp.mean(yc * yc, axis=-1, keepdims=True)
    return yc * lax.rsqrt(var + LN_EPS) * g + b


def _residual_layer_norm(x_ref, y_ref, g_ref, b_ref, out_ref, chunk_rows):
    g = g_ref[...]
    b = b_ref[...]

    def body(c, carry):
        rows = pl.ds(pl.multiple_of(c * chunk_rows, chunk_rows), chunk_rows)
        out_ref[rows, :] = _layer_norm(DEEPNORM_ALPHA * x_ref[rows, :] + y_ref[rows, :], g, b)
        return carry

    lax.fori_loop(0, x_ref.shape[0] // chunk_rows, body, 0)


def _proj_kernel(*refs, n_x, n_rope_blocks, n_f32_blocks, n_cast):
    x_parts, (pos_ref, invf_ref, w_ref), rest = refs[:n_x], refs[n_x:n_x + 3], refs[n_x + 3:]
    cast_in, (main_ref, gate_ref) = rest[:n_cast], rest[n_cast:n_cast + 2]
    cast_out = rest[n_cast + 2:2 * n_cast + 2]
    xb_ref, cos_ref, sin_ref = rest[2 * n_cast + 2:]
    j = pl.program_id(1)
    tm, kw = x_parts[0].shape
    half = HEAD_DIM // 2

    def prepare_row_block():
        for k, x_ref in enumerate(x_parts):
            xb_ref[:, k * kw:(k + 1) * kw] = x_ref[...].astype(BF16)
        lane = lax.broadcasted_iota(jnp.int32, (1, HEAD_DIM), 1)
        low = lane < half
        pos = jnp.where(low, pos_ref[0:tm // 2, :], pos_ref[tm // 2:tm, :])
        ang = pos.astype(F32) * invf_ref[...]
        sign = jnp.where(low, -1.0, 1.0).astype(F32)
        for table_ref, tab in ((cos_ref, jnp.cos(ang)), (sin_ref, jnp.sin(ang))):
            swapped = pltpu.roll(tab, half, 1)
            scale = sign if table_ref is sin_ref else 1.0
            top = jnp.where(low, tab, swapped) * scale
            bottom = jnp.where(low, swapped, tab) * scale
            table_ref[1, 0:tm // 2, :] = top
            table_ref[1, tm // 2:tm, :] = bottom
            table_ref[0, 0:tm // 2, :] = top * Q_SCALE
            table_ref[0, tm // 2:tm, :] = bottom * Q_SCALE

    for src, dst in zip(cast_in, cast_out):
        dst[...] = src[...].astype(BF16)

    def project():
        return jnp.dot(xb_ref[...], w_ref[...], preferred_element_type=F32)

    def project_rotated(table):
        acc = project()
        cos = cos_ref[table]
        sin = sin_ref[table]
        for hh in range(acc.shape[1] // HEAD_DIM):
            cols = slice(hh * HEAD_DIM, (hh + 1) * HEAD_DIM)
            t = acc[:, cols]
            main_ref[:, cols] = t * cos + pltpu.roll(t, half, 1) * sin

    @pl.when(j == 0)
    def _():
        prepare_row_block()
        project_rotated(0)

    @pl.when(j == 1)
    def _():
        project_rotated(1)

    @pl.when(jnp.logical_and(j >= n_rope_blocks, j < n_f32_blocks))
    def _():
        main_ref[...] = project()

    @pl.when(j >= n_f32_blocks)
    def _():
        gate_ref[...] = project().astype(BF16)


def _slab_specs(weights, n_steps, step_index):
    specs, shapes = [], []
    for w in weights:
        rows, cols = w.shape
        specs.append(pl.BlockSpec((rows // n_steps, cols), lambda *g: (step_index(*g), 0)))
        shapes.append(jax.ShapeDtypeStruct(w.shape, BF16))
    return specs, shapes


def _proj(x2, pos2, invf2, w_in_b, cast_weights):
    s, d = x2.shape
    tm, tn = PROJ_TM, PROJ_TN
    n_i, n_j = s // tm, IN_WIDTH // tn
    assert tn == ATTN_WIDTH
    n_rope_blocks = 2
    n_f32_blocks = MAIN_WIDTH // tn
    cast_specs, cast_shapes = _slab_specs(cast_weights, n_i * n_j, lambda i, j: i * n_j + j)
    main_spec = pl.BlockSpec((tm, tn), lambda i, j: (i, jnp.minimum(j, n_f32_blocks - 1)))
    gate_spec = pl.BlockSpec((tm, tn), lambda i, j: (i, jnp.maximum(j - n_f32_blocks, 0)))

    n_x = PROJ_X_PARTS

    def x_part_spec(k):
        def index(i, j):
            ahead = (j >= n_j - n_x + k).astype(jnp.int32)
            return jnp.minimum(i + ahead, n_i - 1), k
        return pl.BlockSpec((tm, d // n_x), index)

    outs = pl.pallas_call(
        functools.partial(_proj_kernel, n_x=n_x, n_rope_blocks=n_rope_blocks, n_f32_blocks=n_f32_blocks,
                          n_cast=len(cast_weights)),
        grid=(n_i, n_j),
        in_specs=[x_part_spec(k) for k in range(n_x)] + [
            pl.BlockSpec((tm, 1), lambda i, j: (i, 0)),
            pl.BlockSpec((1, HEAD_DIM), lambda i, j: (0, 0)),
            pl.BlockSpec((d, tn), lambda i, j: (0, j)),
        ] + cast_specs,
        out_specs=[main_spec, gate_spec] + cast_specs,
        out_shape=[jax.ShapeDtypeStruct((s, MAIN_WIDTH), F32),
                   jax.ShapeDtypeStruct((s, IN_WIDTH - MAIN_WIDTH), BF16)] + cast_shapes,
        scratch_shapes=[
            pltpu.VMEM((tm, d), BF16),
            pltpu.VMEM((2, tm, HEAD_DIM), F32),
            pltpu.VMEM((2, tm, HEAD_DIM), F32),
        ],
        compiler_params=pltpu.CompilerParams(
            dimension_semantics=("arbitrary", "arbitrary"), vmem_limit_bytes=VMEM_LIMIT_BYTES),
        name="proj_rope",
    )(*([x2] * n_x), pos2, invf2, w_in_b, *cast_weights)
    return outs[0], outs[1], outs[2:]


def _attn_kernel(q_ref, kc_ref, vc_ref, o_ref, bias_ref,
                 qd_refs, kd_refs, vd_refs, o_refs, lse_refs, stage_refs):
    n = pl.program_id(1)
    blk = SUB_BLOCK

    @pl.when(jnp.logical_and(n == 0, pl.program_id(0) == 0))
    def _():
        for kd_ref, vd_ref in zip(kd_refs, vd_refs):
            kd_ref[...] = jnp.zeros(kd_ref.shape, BF16)
            vd_ref[:, :HEAD_DIM] = jnp.zeros((vd_ref.shape[0], HEAD_DIM), BF16)
            vd_ref[:, HEAD_DIM:] = jnp.ones((vd_ref.shape[0], HEAD_DIM), BF16)
        qi = lax.broadcasted_iota(jnp.int32, (blk, 2 * blk), 0)
        kj = lax.broadcasted_iota(jnp.int32, (blk, 2 * blk), 1)
        band = (kj >= qi) & (kj <= qi + blk)
        bias_ref[0] = jnp.where(band, 0.0, MASK_VALUE).astype(F32)
        bias_ref[1] = jnp.where(band & (kj >= blk), 0.0, MASK_VALUE).astype(F32)

    names = ("q", "k", "v")
    src = dict(zip(names, (q_ref, kc_ref, vc_ref)))
    src_dil = 1
    stage = dict(zip(names, stage_refs))

    first_chunk = (n == 0).astype(jnp.int32)

    for p, dil in enumerate(DILATIONS):
        m_len = ATTN_CHUNK // dil
        nb = m_len // blk
        krows = blk + m_len
        qd_ref, kd_ref, vd_ref = qd_refs[p], kd_refs[p], vd_refs[p]

        f = dil // src_dil
        src_len = ATTN_CHUNK // src_dil
        keep_f32 = f > 1 and p + 1 < len(DILATIONS)

        def seg(name, r, m0, rows, f=f, src=src, src_dil=src_dil, src_len=src_len, m_len=m_len):
            if f == 1:
                return src[name][pl.ds(r * m_len + m0, rows), :]
            row0 = (r % src_dil) * src_len + r // src_dil + f * m0
            return src[name][pl.ds(row0, rows, stride=f), :]

        for r in range(dil):
            for name, dst, dst_rows in (("q", qd_ref, m_len), ("k", kd_ref, krows), ("v", vd_ref, krows)):
                cur = seg(name, r, 0, m_len)
                if keep_f32:
                    stage[name][r * m_len:(r + 1) * m_len, :] = cur
                if name == "q":
                    dst[r * m_len:(r + 1) * m_len, :] = cur.astype(BF16)
                    continue
                row0 = r * dst_rows
                dst[row0:row0 + blk, 0:HEAD_DIM] = dst[row0 + m_len:row0 + m_len + blk, 0:HEAD_DIM]
                dst[row0 + blk:row0 + dst_rows, 0:HEAD_DIM] = cur.astype(BF16)
        if keep_f32:
            src, src_dil = stage, dil

        for r in range(dil):
            for b in range(nb):
                q0 = r * m_len + b * blk
                k0 = r * krows + b * blk
                qb = qd_ref[q0:q0 + blk, :]
                kb = kd_ref[k0:k0 + 2 * blk, :]
                vb = vd_ref[k0:k0 + 2 * blk, :]
                s = lax.dot_general(qb, kb, (((1,), (1,)), ((), ())), preferred_element_type=F32)
                s = s + (bias_ref[first_chunk] if b == 0 else bias_ref[0])
                m = jnp.max(s, axis=-1, keepdims=True)
                pr = jnp.exp2(s - m)
                ext = jnp.dot(pr.astype(BF16), vb, preferred_element_type=F32)
                acc, l = ext[:, :HEAD_DIM], ext[:, HEAD_DIM:]
                rows = pl.ds(b * (blk * dil) + r, blk, stride=dil)
                o_refs[p][rows, :] = acc / l
                lse_refs[p][rows, :] = m + jnp.log2(l)

    for c in range(ATTN_CHUNK // blk):
        rows = slice(c * blk, (c + 1) * blk)
        lse = [ref[rows, :] for ref in lse_refs]
        top = functools.reduce(jnp.maximum, lse)
        w = [jnp.exp2(x - top) for x in lse]
        num = sum(wp * ref[rows, :] for wp, ref in zip(w, o_refs))
        o_ref[rows, :] = (num / sum(w)).astype(o_ref.dtype)


def _attn(h, cast_weights):
    s = h.shape[0]
    c = ATTN_CHUNK
    blk = SUB_BLOCK
    npat = len(DILATIONS)
    n_cast = len(cast_weights)
    n_chunks = s // c
    chunk = lambda col0: pl.BlockSpec((c, HEAD_DIM), lambda hh, n: (n, col0 + hh))
    cast_specs, cast_shapes = _slab_specs(cast_weights, n_chunks * ATTN_HEADS, lambda hh, n: hh * n_chunks + n)

    def body(q_ref, kc_ref, vc_ref, *rest):
        cast_in, o_ref, cast_out = rest[:n_cast], rest[n_cast], rest[n_cast + 1:2 * n_cast + 1]
        bias, scr = rest[2 * n_cast + 1], rest[2 * n_cast + 2:]
        for src, dst in zip(cast_in, cast_out):
            dst[...] = src[...].astype(BF16)
        groups = [scr[i * npat:(i + 1) * npat] for i in range(5)]
        _attn_kernel(q_ref, kc_ref, vc_ref, o_ref, bias, *groups, scr[5 * npat:])

    k_scratch = [pltpu.VMEM((dil * blk + c, HEAD_DIM), BF16) for dil in DILATIONS]
    v_scratch = [pltpu.VMEM((dil * blk + c, 2 * HEAD_DIM), BF16) for dil in DILATIONS]
    outs = pl.pallas_call(
        body,
        grid=(ATTN_HEADS, n_chunks),
        in_specs=[chunk(0), chunk(ATTN_HEADS), chunk(2 * ATTN_HEADS)] + cast_specs,
        out_specs=[pl.BlockSpec((c, HEAD_DIM), lambda hh, n: (n, hh))] + cast_specs,
        out_shape=[jax.ShapeDtypeStruct((s, ATTN_WIDTH), BF16)] + cast_shapes,
        scratch_shapes=(
            [pltpu.VMEM((2, blk, 2 * blk), F32)]
            + [pltpu.VMEM((c, HEAD_DIM), BF16)] * npat
            + k_scratch + v_scratch
            + [pltpu.VMEM((c, HEAD_DIM), F32)] * (2 * npat)
            + [pltpu.VMEM((c, HEAD_DIM), F32)] * 3),
        compiler_params=pltpu.CompilerParams(
            dimension_semantics=("arbitrary", "arbitrary"), vmem_limit_bytes=VMEM_LIMIT_BYTES),
        name="dilated_attn",
    )(h, h, h, *cast_weights)
    return outs[0], outs[1:]


def _mix_kernel(oat_ref, u_ref, uh_ref, ga_ref, gp_ref, x_ref, wpool_ref, pscale_ref,
                wba_ref, wbb_ref, wout_ref, g_ref, b_ref, out_ref, ubuf_ref, pm_ref, mg_ref, *lv_refs):
    i = pl.program_id(0)
    tm = u_ref.shape[0]
    halo = MAX_POOL_WINDOW

    pad = POOL_PAD
    top = pad + halo
    rows_all = top + tm

    @pl.when(i == 0)
    def _():
        ubuf_ref[0:top, :] = jnp.zeros((top, POOL_WIDTH), F32)
        for lv_ref in lv_refs:
            lv_ref[0:pad, :] = jnp.zeros((pad, POOL_GROUP_WIDTH), F32)

    @pl.when(i > 0)
    def _():
        ubuf_ref[pad:top, :] = uh_ref[...]

    ubuf_ref[top:rows_all, :] = u_ref[...]

    hm = tm // MIX_PARTS
    parts = [slice(part * hm, (part + 1) * hm) for part in range(MIX_PARTS)]
    n_chunks = D_MODEL // MIX_TN
    chunk_cols = [slice(c * MIX_TN, (c + 1) * MIX_TN) for c in range(n_chunks)]

    def pool_group(part, g):
        rows, w = parts[part], POOL_WINDOWS[g]
        lo, hi = pad + part * hm, top + (part + 1) * hm
        t_glob = i * tm + part * hm + lax.broadcasted_iota(jnp.int32, (hm, 1), 0)
        cols = slice(g * POOL_GROUP_WIDTH, (g + 1) * POOL_GROUP_WIDTH)
        cur, cur_cols, shift, level = ubuf_ref, cols, 1, 0
        while shift < w:
            wsum = cur[lo:hi, cur_cols] + cur[lo - shift:hi - shift, cur_cols]
            shift *= 2
            if shift < w:
                lv_refs[level % 2][lo:hi, :] = wsum
                cur, cur_cols, level = lv_refs[level % 2], slice(None), level + 1
        inv_count = 1.0 / jnp.minimum(t_glob + 1, w).astype(F32)
        pooled = wsum[halo:, :] * inv_count - u_ref[rows, cols]
        y = jnp.dot(pooled.astype(BF16), wpool_ref[g], preferred_element_type=F32)
        pm_ref[rows, cols] = (y * pscale_ref[:, cols]).astype(BF16)

    def attn_branch(part, c):
        rows, cols = parts[part], chunk_cols[c]
        y_attn = jnp.dot(oat_ref[rows, :], wba_ref[:, cols], preferred_element_type=F32)
        out_ref[rows, cols] = jax.nn.sigmoid(ga_ref[rows, cols].astype(F32)) * y_attn

    def pool_branch(part, c):
        rows, cols = parts[part], chunk_cols[c]
        y_pool = jnp.dot(pm_ref[rows, :], wbb_ref[:, cols], preferred_element_type=F32)
        merged = out_ref[rows, cols] + jax.nn.sigmoid(gp_ref[rows, cols].astype(F32)) * y_pool
        mg_ref[rows, cols] = merged.astype(BF16)

    def out_proj(part):
        rows = parts[part]
        out_ref[rows, :] = jnp.dot(mg_ref[rows, :], wout_ref[...], preferred_element_type=F32)

    def norm_rows(rows):
        out_ref[rows, :] = _layer_norm(DEEPNORM_ALPHA * x_ref[rows, :] + out_ref[rows, :], g_ref[...], b_ref[...])

    assert MIX_PARTS == 2 and n_chunks == len(POOL_WINDOWS)
    for part in range(MIX_PARTS):
        for c in range(n_chunks):
            attn_branch(part, c)
            pool_group(part, c)
    for c in range(n_chunks):
        pool_branch(0, c)
    out_proj(0)
    ln_rows = hm // n_chunks
    for c in range(n_chunks):
        pool_branch(1, c)
        norm_rows(slice(c * ln_rows, (c + 1) * ln_rows))
    out_proj(1)
    norm_rows(parts[1])


def _mix(o_attn, h_main, h_gates, x2, w_pool_b, pool_scale, w_ba_b, w_bb_b, w_out_b, ln_g, ln_b):
    s, d = x2.shape
    tm = MIX_TM
    halo = MAX_POOL_WINDOW
    u_col = (3 * ATTN_WIDTH) // POOL_WIDTH
    resident = lambda shape: pl.BlockSpec(shape, lambda i: (0,) * len(shape), pipeline_mode=pl.Buffered(1))
    return pl.pallas_call(
        _mix_kernel,
        grid=(s // tm,),
        in_specs=[
            pl.BlockSpec((tm, ATTN_WIDTH), lambda i: (i, 0)),
            pl.BlockSpec((tm, POOL_WIDTH), lambda i: (i, u_col)),
            pl.BlockSpec((halo, POOL_WIDTH), lambda i: (jnp.maximum(i * (tm // halo) - 1, 0), u_col)),
            pl.BlockSpec((tm, d), lambda i: (i, 0)),
            pl.BlockSpec((tm, d), lambda i: (i, 1)),
            pl.BlockSpec((tm, d), lambda i: (i, 0)),
            resident(w_pool_b.shape),
            resident((1, POOL_WIDTH)),
            resident(w_ba_b.shape),
            resident(w_bb_b.shape),
            resident(w_out_b.shape),
            resident((1, d)),
            resident((1, d)),
        ],
        out_specs=pl.BlockSpec((tm, d), lambda i: (i, 0)),
        out_shape=jax.ShapeDtypeStruct((s, d), F32),
        scratch_shapes=[
            pltpu.VMEM((POOL_PAD + halo + tm, POOL_WIDTH), F32),
            pltpu.VMEM((tm, POOL_WIDTH), BF16),
            pltpu.VMEM((tm, d), BF16),
            pltpu.VMEM((POOL_PAD + halo + tm, POOL_GROUP_WIDTH), F32),
            pltpu.VMEM((POOL_PAD + halo + tm, POOL_GROUP_WIDTH), F32),
        ],
        compiler_params=pltpu.CompilerParams(
            dimension_semantics=("arbitrary",), vmem_limit_bytes=VMEM_LIMIT_BYTES),
        name="mix_ln",
    )(o_attn, h_main, h_main, h_gates, h_gates, x2, w_pool_b, pool_scale, w_ba_b, w_bb_b, w_out_b, ln_g, ln_b)


def _ffn_kernel(x_ref, w1_ref, w2_ref, g_ref, b_ref, out_ref, xb_ref):
    j = pl.program_id(1)

    @pl.when(j == 0)
    def _():
        xb_ref[...] = x_ref[...].astype(BF16)
        out_ref[...] = jnp.zeros(out_ref.shape, F32)

    hid = jnp.dot(xb_ref[...], w1_ref[...], preferred_element_type=F32)
    hid = jnp.square(jnp.maximum(hid, 0.0)).astype(BF16)
    out_ref[...] += jnp.dot(hid, w2_ref[...], preferred_element_type=F32)

    @pl.when(j == pl.num_programs(1) - 1)
    def _():
        _residual_layer_norm(x_ref, out_ref, g_ref, b_ref, out_ref, FFN_LN_ROWS)


def _ffn(x1, w1_b, w2_b, ln_g, ln_b):
    s, d = x1.shape
    tm, tf = FFN_TM, FFN_TF
    return pl.pallas_call(
        _ffn_kernel,
        grid=(s // tm, D_FF // tf),
        in_specs=[
            pl.BlockSpec((tm, d), lambda i, j: (i, 0)),
            pl.BlockSpec((d, tf), lambda i, j: (0, j)),
            pl.BlockSpec((tf, d), lambda i, j: (j, 0)),
            pl.BlockSpec((1, d), lambda i, j: (0, 0)),
            pl.BlockSpec((1, d), lambda i, j: (0, 0)),
        ],
        out_specs=pl.BlockSpec((tm, d), lambda i, j: (i, 0)),
        out_shape=jax.ShapeDtypeStruct((s, d), F32),
        scratch_shapes=[pltpu.VMEM((tm, d), BF16)],
        compiler_params=pltpu.CompilerParams(
            dimension_semantics=("arbitrary", "arbitrary"), vmem_limit_bytes=VMEM_LIMIT_BYTES),
        name="ffn_ln",
    )(x1, w1_b, w2_b, ln_g, ln_b)


def kernel(x, positions, w_in, w_pool, pool_scale, w_branch_attn, w_branch_pool, w_out,
           ln_mix_g, ln_mix_b, w_ff1, w_ff2, ln_ff_g, ln_ff_b):
    b, s, d = x.shape
    assert (b, s, d) == (1, SEQ, D_MODEL) and w_in.shape[0] == DEPTH
    half = HEAD_DIM // 2
    inv_freq = ROPE_THETA ** (-jnp.arange(half, dtype=F32) / half)
    invf2 = jnp.concatenate([inv_freq, inv_freq]).reshape(1, HEAD_DIM)
    x2 = x.reshape(s, d)
    pos2 = positions.reshape(s, 1)
    for layer in range(DEPTH):
        w_pool2 = w_pool[layer].reshape(POOL_WIDTH, POOL_GROUP_WIDTH)
        h_main, h_gates, _ = _proj(x2, pos2, invf2, w_in[layer].astype(BF16), [])
        o_attn, (w_ba_b, w_bb_b, w_out_b, w_pool_b, w1_b, w2_b) = _attn(
            h_main, [w_branch_attn[layer], w_branch_pool[layer], w_out[layer], w_pool2, w_ff1[layer], w_ff2[layer]])
        x2 = _mix(o_attn, h_main, h_gates, x2, w_pool_b.reshape(w_pool[layer].shape),
                  pool_scale[layer].reshape(1, POOL_WIDTH), w_ba_b, w_bb_b, w_out_b,
                  ln_mix_g[layer].reshape(1, d), ln_mix_b[layer].reshape(1, d))
        x2 = _ffn(x2, w1_b, w2_b, ln_ff_g[layer].reshape(1, d), ln_ff_b[layer].reshape(1, d))
    return x2.reshape(b, s, d)
```

```python
import functools

import jax
import jax.numpy as jnp
from jax import lax
from jax.experimental import pallas as pl
from jax.experimental.pallas import tpu as pltpu

F32 = jnp.float32
BF16 = jnp.bfloat16

D_MODEL = 2048
SEQ = 8192
HEAD_DIM = 128
ATTN_WIDTH = D_MODEL // 2
ATTN_HEADS = ATTN_WIDTH // HEAD_DIM
POOL_WIDTH = D_MODEL // 2
POOL_WINDOWS = (2, 4, 8, 16)
POOL_GROUP_WIDTH = POOL_WIDTH // len(POOL_WINDOWS)
MAX_POOL_WINDOW = max(POOL_WINDOWS)
POOL_PAD = 8
DILATIONS = (1, 4, 16)
SUB_BLOCK = 128
D_FF = 4 * D_MODEL
IN_WIDTH = 3 * ATTN_WIDTH + POOL_WIDTH + 2 * D_MODEL
MAIN_WIDTH = 3 * ATTN_WIDTH + POOL_WIDTH
ROPE_THETA = 10000.0
LN_EPS = 1e-5
DEPTH = 1
DEEPNORM_ALPHA = (2.0 * DEPTH) ** 0.25
SM_SCALE = HEAD_DIM ** -0.5
LOG2_E = 1.4426950408889634
Q_SCALE = SM_SCALE * LOG2_E
MASK_VALUE = -1e30

VMEM_LIMIT_BYTES = 60 * 1024 * 1024

ATTN_CHUNK = max(DILATIONS) * SUB_BLOCK
PROJ_TM, PROJ_TN = 1024, 1024
MIX_TM, MIX_TN, MIX_PARTS = 512, 512, 2
FFN_TM, FFN_TF = 1024, 1024
FFN_LN_ROWS = 128


def _layer_norm(y, g, b):
    mu = jnp.mean(y, axis=-1, keepdims=True)
    yc = y - mu
    var = jnp.mean(yc * yc, axis=-1, keepdims=True)
    return yc * lax.rsqrt(var + LN_EPS) * g + b


def _residual_layer_norm(x_ref, y_ref, g_ref, b_ref, out_ref, chunk_rows):
    g = g_ref[...]
    b = b_ref[...]

    def body(c, carry):
        rows = pl.ds(pl.multiple_of(c * chunk_rows, chunk_rows), chunk_rows)
        out_ref[rows, :] = _layer_norm(DEEPNORM_ALPHA * x_ref[rows, :] + y_ref[rows, :], g, b)
        return carry

    lax.fori_loop(0, x_ref.shape[0] // chunk_rows, body, 0)


def _proj_kernel(x_ref, pos_ref, invf_ref, w_ref, *rest, n_rope_blocks, n_f32_blocks, n_cast):
    cast_in, (main_ref, gate_ref) = rest[:n_cast], rest[n_cast:n_cast + 2]
    cast_out = rest[n_cast + 2:2 * n_cast + 2]
    xb_ref, cos_ref, sin_ref = rest[2 * n_cast + 2:]
    j = pl.program_id(1)
    tm = x_ref.shape[0]
    half = HEAD_DIM // 2

    def prepare_row_block():
        xb_ref[...] = x_ref[...].astype(BF16)
        lane = lax.broadcasted_iota(jnp.int32, (1, HEAD_DIM), 1)
        low = lane < half
        pos = jnp.where(low, pos_ref[0:tm // 2, :], pos_ref[tm // 2:tm, :])
        ang = pos.astype(F32) * invf_ref[...]
        sign = jnp.where(low, -1.0, 1.0).astype(F32)
        for table_ref, tab in ((cos_ref, jnp.cos(ang)), (sin_ref, jnp.sin(ang))):
            swapped = pltpu.roll(tab, half, 1)
            scale = sign if table_ref is sin_ref else 1.0
            top = jnp.where(low, tab, swapped) * scale
            bottom = jnp.where(low, swapped, tab) * scale
            table_ref[1, 0:tm // 2, :] = top
            table_ref[1, tm // 2:tm, :] = bottom
            table_ref[0, 0:tm // 2, :] = top * Q_SCALE
            table_ref[0, tm // 2:tm, :] = bottom * Q_SCALE

    for src, dst in zip(cast_in, cast_out):
        dst[...] = src[...].astype(BF16)

    def project():
        return jnp.dot(xb_ref[...], w_ref[...], preferred_element_type=F32)

    def project_rotated(table):
        acc = project()
        cos = cos_ref[table]
        sin = sin_ref[table]
        for hh in range(acc.shape[1] // HEAD_DIM):
            cols = slice(hh * HEAD_DIM, (hh + 1) * HEAD_DIM)
            t = acc[:, cols]
            main_ref[:, cols] = t * cos + pltpu.roll(t, half, 1) * sin

    @pl.when(j == 0)
    def _():
        prepare_row_block()
        project_rotated(0)

    @pl.when(j == 1)
    def _():
        project_rotated(1)

    @pl.when(jnp.logical_and(j >= n_rope_blocks, j < n_f32_blocks))
    def _():
        main_ref[...] = project()

    @pl.when(j >= n_f32_blocks)
    def _():
        gate_ref[...] = project().astype(BF16)


def _slab_specs(weights, n_steps, step_index):
    specs, shapes = [], []
    for w in weights:
        rows, cols = w.shape
        specs.append(pl.BlockSpec((rows // n_steps, cols), lambda *g: (step_index(*g), 0)))
        shapes.append(jax.ShapeDtypeStruct(w.shape, BF16))
    return specs, shapes


def _proj(x2, pos2, invf2, w_in_b, cast_weights):
    s, d = x2.shape
    tm, tn = PROJ_TM, PROJ_TN
    n_i, n_j = s // tm, IN_WIDTH // tn
    assert tn == ATTN_WIDTH
    n_rope_blocks = 2
    n_f32_blocks = MAIN_WIDTH // tn
    cast_specs, cast_shapes = _slab_specs(cast_weights, n_i * n_j, lambda i, j: i * n_j + j)
    main_spec = pl.BlockSpec((tm, tn), lambda i, j: (i, jnp.minimum(j, n_f32_blocks - 1)))
    gate_spec = pl.BlockSpec((tm, tn), lambda i, j: (i, jnp.maximum(j - n_f32_blocks, 0)))
    outs = pl.pallas_call(
        functools.partial(_proj_kernel, n_rope_blocks=n_rope_blocks, n_f32_blocks=n_f32_blocks,
                          n_cast=len(cast_weights)),
        grid=(n_i, n_j),
        in_specs=[
            pl.BlockSpec((tm, d), lambda i, j: (i, 0)),
            pl.BlockSpec((tm, 1), lambda i, j: (i, 0)),
            pl.BlockSpec((1, HEAD_DIM), lambda i, j: (0, 0)),
            pl.BlockSpec((d, tn), lambda i, j: (0, j)),
        ] + cast_specs,
        out_specs=[main_spec, gate_spec] + cast_specs,
        out_shape=[jax.ShapeDtypeStruct((s, MAIN_WIDTH), F32),
                   jax.ShapeDtypeStruct((s, IN_WIDTH - MAIN_WIDTH), BF16)] + cast_shapes,
        scratch_shapes=[
            pltpu.VMEM((tm, d), BF16),
            pltpu.VMEM((2, tm, HEAD_DIM), F32),
            pltpu.VMEM((2, tm, HEAD_DIM), F32),
        ],
        compiler_params=pltpu.CompilerParams(
            dimension_semantics=("arbitrary", "arbitrary"), vmem_limit_bytes=VMEM_LIMIT_BYTES),
        name="proj_rope",
    )(x2, pos2, invf2, w_in_b, *cast_weights)
    return outs[0], outs[1], outs[2:]


def _attn_kernel(q_ref, kc_ref, vc_ref, o_ref, bias_ref,
                 qd_refs, kd_refs, vd_refs, o_refs, lse_refs, stage_refs):
    n = pl.program_id(1)
    blk = SUB_BLOCK

    @pl.when(jnp.logical_and(n == 0, pl.program_id(0) == 0))
    def _():
        for kd_ref, vd_ref in zip(kd_refs, vd_refs):
            kd_ref[...] = jnp.zeros(kd_ref.shape, BF16)
            vd_ref[:, :HEAD_DIM] = jnp.zeros((vd_ref.shape[0], HEAD_DIM), BF16)
            vd_ref[:, HEAD_DIM:] = jnp.ones((vd_ref.shape[0], HEAD_DIM), BF16)
        qi = lax.broadcasted_iota(jnp.int32, (blk, 2 * blk), 0)
        kj = lax.broadcasted_iota(jnp.int32, (blk, 2 * blk), 1)
        band = (kj >= qi) & (kj <= qi + blk)
        bias_ref[0] = jnp.where(band, 0.0, MASK_VALUE).astype(F32)
        bias_ref[1] = jnp.where(band & (kj >= blk), 0.0, MASK_VALUE).astype(F32)

    names = ("q", "k", "v")
    src = dict(zip(names, (q_ref, kc_ref, vc_ref)))
    src_dil = 1
    stage = dict(zip(names, stage_refs))
    first_chunk = (n == 0).astype(jnp.int32)

    regroup, units = [], []
    for p, dil in enumerate(DILATIONS):
        m_len = ATTN_CHUNK // dil
        nb = m_len // blk
        krows = blk + m_len
        qd_ref, kd_ref, vd_ref = qd_refs[p], kd_refs[p], vd_refs[p]

        f = dil // src_dil
        src_len = ATTN_CHUNK // src_dil
        keep_f32 = f > 1 and p + 1 < len(DILATIONS)

        def regroup_piece(name, r, dst, dst_rows, f=f, src=src, src_dil=src_dil, src_len=src_len,
                          m_len=m_len, keep_f32=keep_f32):
            if f == 1:
                cur = src[name][pl.ds(r * m_len, m_len), :]
            else:
                cur = src[name][pl.ds((r % src_dil) * src_len + r // src_dil, m_len, stride=f), :]
            if keep_f32:
                stage[name][r * m_len:(r + 1) * m_len, :] = cur
            if name == "q":
                dst[r * m_len:(r + 1) * m_len, :] = cur.astype(BF16)
                return
            row0 = r * dst_rows
            dst[row0:row0 + blk, 0:HEAD_DIM] = dst[row0 + m_len:row0 + m_len + blk, 0:HEAD_DIM]
            dst[row0 + blk:row0 + dst_rows, 0:HEAD_DIM] = cur.astype(BF16)

        regroup.append([functools.partial(regroup_piece, name, r, dst, dst_rows)
                        for r in range(dil)
                        for name, dst, dst_rows in (("q", qd_ref, m_len), ("k", kd_ref, krows), ("v", vd_ref, krows))])
        if keep_f32:
            src, src_dil = stage, dil

        def unit(r, b, p=p, dil=dil, m_len=m_len, krows=krows, qd_ref=qd_ref, kd_ref=kd_ref, vd_ref=vd_ref):
            q0 = r * m_len + b * blk
            k0 = r * krows + b * blk
            qb = qd_ref[q0:q0 + blk, :]
            kb = kd_ref[k0:k0 + 2 * blk, :]
            vb = vd_ref[k0:k0 + 2 * blk, :]
            s = lax.dot_general(qb, kb, (((1,), (1,)), ((), ())), preferred_element_type=F32)
            s = s + (bias_ref[first_chunk] if b == 0 else bias_ref[0])
            m = jnp.max(s, axis=-1, keepdims=True)
            pr = jnp.exp2(s - m)
            ext = jnp.dot(pr.astype(BF16), vb, preferred_element_type=F32)
            acc, l = ext[:, :HEAD_DIM], ext[:, HEAD_DIM:]
            rows = pl.ds(b * (blk * dil) + r, blk, stride=dil)
            o_refs[p][rows, :] = acc / l
            lse_refs[p][rows, :] = m + jnp.log2(l)

        units.append([functools.partial(unit, r, b) for r in range(dil) for b in range(nb)])

    def mix_chunk(c):
        rows = slice(c * blk, (c + 1) * blk)
        lse = [ref[rows, :] for ref in lse_refs]
        top = functools.reduce(jnp.maximum, lse)
        w = [jnp.exp2(x - top) for x in lse]
        num = sum(wp * ref[rows, :] for wp, ref in zip(w, o_refs))
        o_ref[rows, :] = (num / sum(w)).astype(o_ref.dtype)

    assert DILATIONS[0] == 1 and len(DILATIONS) == 3
    for piece in regroup[0] + regroup[1]:
        piece()
    per_unit = len(regroup[2]) // len(units[1])
    for u, run_unit in enumerate(units[1]):
        run_unit()
        for piece in regroup[2][u * per_unit:(u + 1) * per_unit]:
            piece()
    for run_unit in units[2]:
        run_unit()
    for c, run_unit in enumerate(units[0]):
        run_unit()
        mix_chunk(c)


def _attn(h, cast_weights):
    s = h.shape[0]
    c = ATTN_CHUNK
    blk = SUB_BLOCK
    npat = len(DILATIONS)
    n_cast = len(cast_weights)
    n_chunks = s // c
    chunk = lambda col0: pl.BlockSpec((c, HEAD_DIM), lambda hh, n: (n, col0 + hh))
    cast_specs, cast_shapes = _slab_specs(cast_weights, n_chunks * ATTN_HEADS, lambda hh, n: hh * n_chunks + n)

    def body(q_ref, kc_ref, vc_ref, *rest):
        cast_in, o_ref, cast_out = rest[:n_cast], rest[n_cast], rest[n_cast + 1:2 * n_cast + 1]
        bias, scr = rest[2 * n_cast + 1], rest[2 * n_cast + 2:]
        for src, dst in zip(cast_in, cast_out):
            dst[...] = src[...].astype(BF16)
        groups = [scr[i * npat:(i + 1) * npat] for i in range(5)]
        _attn_kernel(q_ref, kc_ref, vc_ref, o_ref, bias, *groups, scr[5 * npat:])

    k_scratch = [pltpu.VMEM((dil * blk + c, HEAD_DIM), BF16) for dil in DILATIONS]
    v_scratch = [pltpu.VMEM((dil * blk + c, 2 * HEAD_DIM), BF16) for dil in DILATIONS]
    outs = pl.pallas_call(
        body,
        grid=(ATTN_HEADS, n_chunks),
        in_specs=[chunk(0), chunk(ATTN_HEADS), chunk(2 * ATTN_HEADS)] + cast_specs,
        out_specs=[pl.BlockSpec((c, HEAD_DIM), lambda hh, n: (n, hh))] + cast_specs,
        out_shape=[jax.ShapeDtypeStruct((s, ATTN_WIDTH), BF16)] + cast_shapes,
        scratch_shapes=(
            [pltpu.VMEM((2, blk, 2 * blk), F32)]
            + [pltpu.VMEM((c, HEAD_DIM), BF16)] * npat
            + k_scratch + v_scratch
            + [pltpu.VMEM((c, HEAD_DIM), F32)] * (2 * npat)
            + [pltpu.VMEM((c, HEAD_DIM), F32)] * 3),
        compiler_params=pltpu.CompilerParams(
            dimension_semantics=("arbitrary", "arbitrary"), vmem_limit_bytes=VMEM_LIMIT_BYTES),
        name="dilated_attn",
    )(h, h, h, *cast_weights)
    return outs[0], outs[1:]


def _mix_kernel(oat_ref, u_ref, uh_ref, ga_ref, gp_ref, x_ref, wpool_ref, pscale_ref,
                wba_ref, wbb_ref, wout_ref, g_ref, b_ref, out_ref, ubuf_ref, pm_ref, mg_ref, *lv_refs):
    i = pl.program_id(0)
    tm = u_ref.shape[0]
    halo = MAX_POOL_WINDOW

    pad = POOL_PAD
    top = pad + halo
    rows_all = top + tm

    @pl.when(i == 0)
    def _():
        ubuf_ref[0:top, :] = jnp.zeros((top, POOL_WIDTH), F32)
        for lv_ref in lv_refs:
            lv_ref[0:pad, :] = jnp.zeros((pad, POOL_GROUP_WIDTH), F32)

    @pl.when(i > 0)
    def _():
        ubuf_ref[pad:top, :] = uh_ref[...]

    ubuf_ref[top:rows_all, :] = u_ref[...]

    hm = tm // MIX_PARTS
    parts = [slice(part * hm, (part + 1) * hm) for part in range(MIX_PARTS)]
    n_chunks = D_MODEL // MIX_TN
    chunk_cols = [slice(c * MIX_TN, (c + 1) * MIX_TN) for c in range(n_chunks)]

    def pool_group(part, g):
        rows, w = parts[part], POOL_WINDOWS[g]
        lo, hi = pad + part * hm, top + (part + 1) * hm
        t_glob = i * tm + part * hm + lax.broadcasted_iota(jnp.int32, (hm, 1), 0)
        cols = slice(g * POOL_GROUP_WIDTH, (g + 1) * POOL_GROUP_WIDTH)
        cur, cur_cols, shift, level = ubuf_ref, cols, 1, 0
        while shift < w:
            wsum = cur[lo:hi, cur_cols] + cur[lo - shift:hi - shift, cur_cols]
            shift *= 2
            if shift < w:
                lv_refs[level % 2][lo:hi, :] = wsum
                cur, cur_cols, level = lv_refs[level % 2], slice(None), level + 1
        inv_count = 1.0 / jnp.minimum(t_glob + 1, w).astype(F32)
        pooled = wsum[halo:, :] * inv_count - u_ref[rows, cols]
        y = jnp.dot(pooled.astype(BF16), wpool_ref[g], preferred_element_type=F32)
        pm_ref[rows, cols] = (y * pscale_ref[:, cols]).astype(BF16)

    def attn_branch(part, c):
        rows, cols = parts[part], chunk_cols[c]
        y_attn = jnp.dot(oat_ref[rows, :], wba_ref[:, cols], preferred_element_type=F32)
        out_ref[rows, cols] = jax.nn.sigmoid(ga_ref[rows, cols].astype(F32)) * y_attn

    def pool_branch(part, c):
        rows, cols = parts[part], chunk_cols[c]
        y_pool = jnp.dot(pm_ref[rows, :], wbb_ref[:, cols], preferred_element_type=F32)
        merged = out_ref[rows, cols] + jax.nn.sigmoid(gp_ref[rows, cols].astype(F32)) * y_pool
        mg_ref[rows, cols] = merged.astype(BF16)

    def out_proj(part):
        rows = parts[part]
        out_ref[rows, :] = jnp.dot(mg_ref[rows, :], wout_ref[...], preferred_element_type=F32)

    def norm_rows(rows):
        out_ref[rows, :] = _layer_norm(DEEPNORM_ALPHA * x_ref[rows, :] + out_ref[rows, :], g_ref[...], b_ref[...])

    assert MIX_PARTS == 2 and n_chunks == len(POOL_WINDOWS)
    for part in range(MIX_PARTS):
        for c in range(n_chunks):
            attn_branch(part, c)
            pool_group(part, c)
    for c in range(n_chunks):
        pool_branch(0, c)
    out_proj(0)
    ln_rows = hm // n_chunks
    for c in range(n_chunks):
        pool_branch(1, c)
        norm_rows(slice(c * ln_rows, (c + 1) * ln_rows))
    out_proj(1)
    norm_rows(parts[1])


def _mix(o_attn, h_main, h_gates, x2, w_pool_b, pool_scale, w_ba_b, w_bb_b, w_out_b, ln_g, ln_b):
    s, d = x2.shape
    tm = MIX_TM
    halo = MAX_POOL_WINDOW
    u_col = (3 * ATTN_WIDTH) // POOL_WIDTH
    resident = lambda shape: pl.BlockSpec(shape, lambda i: (0,) * len(shape), pipeline_mode=pl.Buffered(1))
    return pl.pallas_call(
        _mix_kernel,
        grid=(s // tm,),
        in_specs=[
            pl.BlockSpec((tm, ATTN_WIDTH), lambda i: (i, 0)),
            pl.BlockSpec((tm, POOL_WIDTH), lambda i: (i, u_col)),
            pl.BlockSpec((halo, POOL_WIDTH), lambda i: (jnp.maximum(i * (tm // halo) - 1, 0), u_col)),
            pl.BlockSpec((tm, d), lambda i: (i, 0)),
            pl.BlockSpec((tm, d), lambda i: (i, 1)),
            pl.BlockSpec((tm, d), lambda i: (i, 0)),
            resident(w_pool_b.shape),
            resident((1, POOL_WIDTH)),
            resident(w_ba_b.shape),
            resident(w_bb_b.shape),
            resident(w_out_b.shape),
            resident((1, d)),
            resident((1, d)),
        ],
        out_specs=pl.BlockSpec((tm, d), lambda i: (i, 0)),
        out_shape=jax.ShapeDtypeStruct((s, d), F32),
        scratch_shapes=[
            pltpu.VMEM((POOL_PAD + halo + tm, POOL_WIDTH), F32),
            pltpu.VMEM((tm, POOL_WIDTH), BF16),
            pltpu.VMEM((tm, d), BF16),
            pltpu.VMEM((POOL_PAD + halo + tm, POOL_GROUP_WIDTH), F32),
            pltpu.VMEM((POOL_PAD + halo + tm, POOL_GROUP_WIDTH), F32),
        ],
        compiler_params=pltpu.CompilerParams(
            dimension_semantics=("arbitrary",), vmem_limit_bytes=VMEM_LIMIT_BYTES),
        name="mix_ln",
    )(o_attn, h_main, h_main, h_gates, h_gates, x2, w_pool_b, pool_scale, w_ba_b, w_bb_b, w_out_b, ln_g, ln_b)


def _ffn_kernel(x_ref, w1_ref, w2_ref, g_ref, b_ref, out_ref, xb_ref):
    j = pl.program_id(1)

    @pl.when(j == 0)
    def _():
        xb_ref[...] = x_ref[...].astype(BF16)
        out_ref[...] = jnp.zeros(out_ref.shape, F32)

    hid = jnp.dot(xb_ref[...], w1_ref[...], preferred_element_type=F32)
    hid = jnp.square(jnp.maximum(hid, 0.0)).astype(BF16)
    out_ref[...] += jnp.dot(hid, w2_ref[...], preferred_element_type=F32)

    @pl.when(j == pl.num_programs(1) - 1)
    def _():
        _residual_layer_norm(x_ref, out_ref, g_ref, b_ref, out_ref, FFN_LN_ROWS)


def _ffn(x1, w1_b, w2_b, ln_g, ln_b):
    s, d = x1.shape
    tm, tf = FFN_TM, FFN_TF
    return pl.pallas_call(
        _ffn_kernel,
        grid=(s // tm, D_FF // tf),
        in_specs=[
            pl.BlockSpec((tm, d), lambda i, j: (i, 0)),
            pl.BlockSpec((d, tf), lambda i, j: (0, j)),
            pl.BlockSpec((tf, d), lambda i, j: (j, 0)),
            pl.BlockSpec((1, d), lambda i, j: (0, 0)),
            pl.BlockSpec((1, d), lambda i, j: (0, 0)),
        ],
        out_specs=pl.BlockSpec((tm, d), lambda i, j: (i, 0)),
        out_shape=jax.ShapeDtypeStruct((s, d), F32),
        scratch_shapes=[pltpu.VMEM((tm, d), BF16)],
        compiler_params=pltpu.CompilerParams(
            dimension_semantics=("arbitrary", "arbitrary"), vmem_limit_bytes=VMEM_LIMIT_BYTES),
        name="ffn_ln",
    )(x1, w1_b, w2_b, ln_g, ln_b)


def kernel(x, positions, w_in, w_pool, pool_scale, w_branch_attn, w_branch_pool, w_out,
           ln_mix_g, ln_mix_b, w_ff1, w_ff2, ln_ff_g, ln_ff_b):
    b, s, d = x.shape
    assert (b, s, d) == (1, SEQ, D_MODEL) and w_in.shape[0] == DEPTH
    half = HEAD_DIM // 2
    inv_freq = ROPE_THETA ** (-jnp.arange(half, dtype=F32) / half)
    invf2 = jnp.concatenate([inv_freq, inv_freq]).reshape(1, HEAD_DIM)
    x2 = x.reshape(s, d)
    pos2 = positions.reshape(s, 1)
    for layer in range(DEPTH):
        w_pool2 = w_pool[layer].reshape(POOL_WIDTH, POOL_GROUP_WIDTH)
        h_main, h_gates, _ = _proj(x2, pos2, invf2, w_in[layer].astype(BF16), [])
        o_attn, (w_ba_b, w_bb_b, w_out_b, w_pool_b, w1_b, w2_b) = _attn(
            h_main, [w_branch_attn[layer], w_branch_pool[layer], w_out[layer], w_pool2, w_ff1[layer], w_ff2[layer]])
        x2 = _mix(o_attn, h_main, h_gates, x2, w_pool_b.reshape(w_pool[layer].shape),
                  pool_scale[layer].reshape(1, POOL_WIDTH), w_ba_b, w_bb_b, w_out_b,
                  ln_mix_g[layer].reshape(1, d), ln_mix_b[layer].reshape(1, d))
        x2 = _ffn(x2, w1_b, w2_b, ln_ff_g[layer].reshape(1, d), ln_ff_b[layer].reshape(1, d))
    return x2.reshape(b, s, d)
```

```python
import functools

import jax
import jax.numpy as jnp
from jax import lax
from jax.experimental import pallas as pl
from jax.experimental.pallas import tpu as pltpu

F32 = jnp.float32
BF16 = jnp.bfloat16

D_MODEL = 2048
SEQ = 8192
HEAD_DIM = 128
ATTN_WIDTH = D_MODEL // 2
ATTN_HEADS = ATTN_WIDTH // HEAD_DIM
POOL_WIDTH = D_MODEL // 2
POOL_WINDOWS = (2, 4, 8, 16)
POOL_GROUP_WIDTH = POOL_WIDTH // len(POOL_WINDOWS)
MAX_POOL_WINDOW = max(POOL_WINDOWS)
DILATIONS = (1, 4, 16)
SUB_BLOCK = 128
D_FF = 4 * D_MODEL
IN_WIDTH = 3 * ATTN_WIDTH + POOL_WIDTH + 2 * D_MODEL
MAIN_WIDTH = 3 * ATTN_WIDTH + POOL_WIDTH
ROPE_THETA = 10000.0
LN_EPS = 1e-5
DEPTH = 1
DEEPNORM_ALPHA = (2.0 * DEPTH) ** 0.25
SM_SCALE = HEAD_DIM ** -0.5
LOG2_E = 1.4426950408889634
Q_SCALE = SM_SCALE * LOG2_E
MASK_VALUE = -1e30

VMEM_LIMIT_BYTES = 60 * 1024 * 1024

ATTN_CHUNK = max(DILATIONS) * SUB_BLOCK
PROJ_TM, PROJ_TN = 1024, 1024
MIX_TM, MIX_TN, MIX_PARTS = 512, 512, 2
FFN_TM, FFN_TF = 1024, 1024
FFN_LN_ROWS = 128


def _layer_norm(y, g, b):
    mu = jnp.mean(y, axis=-1, keepdims=True)
    yc = y - mu
    var = jnp.mean(yc * yc, axis=-1, keepdims=True)
    return yc * lax.rsqrt(var + LN_EPS) * g + b


def _residual_layer_norm(x_ref, y_ref, g_ref, b_ref, out_ref, chunk_rows):
    g = g_ref[...]
    b = b_ref[...]

    def body(c, carry):
        rows = pl.ds(pl.multiple_of(c * chunk_rows, chunk_rows), chunk_rows)
        out_ref[rows, :] = _layer_norm(DEEPNORM_ALPHA * x_ref[rows, :] + y_ref[rows, :], g, b)
        return carry

    lax.fori_loop(0, x_ref.shape[0] // chunk_rows, body, 0)


def _proj_kernel(x_ref, pos_ref, invf_ref, w_ref, *rest, n_rope_blocks, n_f32_blocks, n_cast):
    cast_in, (main_ref, gate_ref) = rest[:n_cast], rest[n_cast:n_cast + 2]
    cast_out = rest[n_cast + 2:2 * n_cast + 2]
    xb_ref, cos_ref, sin_ref = rest[2 * n_cast + 2:]
    j = pl.program_id(1)
    tm = x_ref.shape[0]
    half = HEAD_DIM // 2

    def prepare_row_block():
        xb_ref[...] = x_ref[...].astype(BF16)
        lane = lax.broadcasted_iota(jnp.int32, (1, HEAD_DIM), 1)
        low = lane < half
        pos = jnp.where(low, pos_ref[0:tm // 2, :], pos_ref[tm // 2:tm, :])
        ang = pos.astype(F32) * invf_ref[...]
        sign = jnp.where(low, -1.0, 1.0).astype(F32)
        for table_ref, tab in ((cos_ref, jnp.cos(ang)), (sin_ref, jnp.sin(ang))):
            swapped = pltpu.roll(tab, half, 1)
            scale = sign if table_ref is sin_ref else 1.0
            top = jnp.where(low, tab, swapped) * scale
            bottom = jnp.where(low, swapped, tab) * scale
            table_ref[1, 0:tm // 2, :] = top
            table_ref[1, tm // 2:tm, :] = bottom
            table_ref[0, 0:tm // 2, :] = top * Q_SCALE
            table_ref[0, tm // 2:tm, :] = bottom * Q_SCALE

    for src, dst in zip(cast_in, cast_out):
        dst[...] = src[...].astype(BF16)

    def project():
        return jnp.dot(xb_ref[...], w_ref[...], preferred_element_type=F32)

    def project_rotated(table):
        acc = project()
        cos = cos_ref[table]
        sin = sin_ref[table]
        for hh in range(acc.shape[1] // HEAD_DIM):
            cols = slice(hh * HEAD_DIM, (hh + 1) * HEAD_DIM)
            t = acc[:, cols]
            main_ref[:, cols] = t * cos + pltpu.roll(t, half, 1) * sin

    @pl.when(j == 0)
    def _():
        prepare_row_block()
        project_rotated(0)

    @pl.when(j == 1)
    def _():
        project_rotated(1)

    @pl.when(jnp.logical_and(j >= n_rope_blocks, j < n_f32_blocks))
    def _():
        main_ref[...] = project()

    @pl.when(j >= n_f32_blocks)
    def _():
        gate_ref[...] = project().astype(BF16)


def _slab_specs(weights, n_steps, step_index):
    specs, shapes = [], []
    for w in weights:
        rows, cols = w.shape
        specs.append(pl.BlockSpec((rows // n_steps, cols), lambda *g: (step_index(*g), 0)))
        shapes.append(jax.ShapeDtypeStruct(w.shape, BF16))
    return specs, shapes


def _proj(x2, pos2, invf2, w_in_b, cast_weights):
    s, d = x2.shape
    tm, tn = PROJ_TM, PROJ_TN
    n_i, n_j = s // tm, IN_WIDTH // tn
    assert tn == ATTN_WIDTH
    n_rope_blocks = 2
    n_f32_blocks = MAIN_WIDTH // tn
    cast_specs, cast_shapes = _slab_specs(cast_weights, n_i * n_j, lambda i, j: i * n_j + j)
    main_spec = pl.BlockSpec((tm, tn), lambda i, j: (i, jnp.minimum(j, n_f32_blocks - 1)))
    gate_spec = pl.BlockSpec((tm, tn), lambda i, j: (i, jnp.maximum(j - n_f32_blocks, 0)))
    outs = pl.pallas_call(
        functools.partial(_proj_kernel, n_rope_blocks=n_rope_blocks, n_f32_blocks=n_f32_blocks,
                          n_cast=len(cast_weights)),
        grid=(n_i, n_j),
        in_specs=[
            pl.BlockSpec((tm, d), lambda i, j: (i, 0)),
            pl.BlockSpec((tm, 1), lambda i, j: (i, 0)),
            pl.BlockSpec((1, HEAD_DIM), lambda i, j: (0, 0)),
            pl.BlockSpec((d, tn), lambda i, j: (0, j)),
        ] + cast_specs,
        out_specs=[main_spec, gate_spec] + cast_specs,
        out_shape=[jax.ShapeDtypeStruct((s, MAIN_WIDTH), F32),
                   jax.ShapeDtypeStruct((s, IN_WIDTH - MAIN_WIDTH), BF16)] + cast_shapes,
        scratch_shapes=[
            pltpu.VMEM((tm, d), BF16),
            pltpu.VMEM((2, tm, HEAD_DIM), F32),
            pltpu.VMEM((2, tm, HEAD_DIM), F32),
        ],
        compiler_params=pltpu.CompilerParams(
            dimension_semantics=("arbitrary", "arbitrary"), vmem_limit_bytes=VMEM_LIMIT_BYTES),
        name="proj_rope",
    )(x2, pos2, invf2, w_in_b, *cast_weights)
    return outs[0], outs[1], outs[2:]


def _attn_kernel(q_ref, kc_ref, vc_ref, o_ref, bias_ref,
                 qd_refs, kd_refs, vd_refs, o_refs, lse_refs, stage_refs):
    n = pl.program_id(1)
    blk = SUB_BLOCK

    @pl.when(jnp.logical_and(n == 0, pl.program_id(0) == 0))
    def _():
        for kd_ref, vd_ref in zip(kd_refs, vd_refs):
            kd_ref[...] = jnp.zeros(kd_ref.shape, BF16)
            vd_ref[:, :HEAD_DIM] = jnp.zeros((vd_ref.shape[0], HEAD_DIM), BF16)
            vd_ref[:, HEAD_DIM:] = jnp.ones((vd_ref.shape[0], HEAD_DIM), BF16)
        qi = lax.broadcasted_iota(jnp.int32, (blk, 2 * blk), 0)
        kj = lax.broadcasted_iota(jnp.int32, (blk, 2 * blk), 1)
        band = (kj >= qi) & (kj <= qi + blk)
        bias_ref[0] = jnp.where(band, 0.0, MASK_VALUE).astype(F32)
        bias_ref[1] = jnp.where(band & (kj >= blk), 0.0, MASK_VALUE).astype(F32)

    names = ("q", "k", "v")
    src = dict(zip(names, (q_ref, kc_ref, vc_ref)))
    src_dil = 1
    stage = dict(zip(names, stage_refs))
    first_chunk = (n == 0).astype(jnp.int32)

    regroup, units = [], []
    for p, dil in enumerate(DILATIONS):
        m_len = ATTN_CHUNK // dil
        nb = m_len // blk
        krows = blk + m_len
        qd_ref, kd_ref, vd_ref = qd_refs[p], kd_refs[p], vd_refs[p]

        f = dil // src_dil
        src_len = ATTN_CHUNK // src_dil
        keep_f32 = f > 1 and p + 1 < len(DILATIONS)

        def regroup_piece(name, r, dst, dst_rows, f=f, src=src, src_dil=src_dil, src_len=src_len,
                          m_len=m_len, keep_f32=keep_f32):
            if f == 1:
                cur = src[name][pl.ds(r * m_len, m_len), :]
            else:
                cur = src[name][pl.ds((r % src_dil) * src_len + r // src_dil, m_len, stride=f), :]
            if keep_f32:
                stage[name][r * m_len:(r + 1) * m_len, :] = cur
            if name == "q":
                dst[r * m_len:(r + 1) * m_len, :] = cur.astype(BF16)
                return
            row0 = r * dst_rows
            dst[row0:row0 + blk, 0:HEAD_DIM] = dst[row0 + m_len:row0 + m_len + blk, 0:HEAD_DIM]
            dst[row0 + blk:row0 + dst_rows, 0:HEAD_DIM] = cur.astype(BF16)

        regroup.append([functools.partial(regroup_piece, name, r, dst, dst_rows)
                        for r in range(dil)
                        for name, dst, dst_rows in (("q", qd_ref, m_len), ("k", kd_ref, krows), ("v", vd_ref, krows))])
        if keep_f32:
            src, src_dil = stage, dil

        def unit(r, b, p=p, dil=dil, m_len=m_len, krows=krows, qd_ref=qd_ref, kd_ref=kd_ref, vd_ref=vd_ref):
            q0 = r * m_len + b * blk
            k0 = r * krows + b * blk
            qb = qd_ref[q0:q0 + blk, :]
            kb = kd_ref[k0:k0 + 2 * blk, :]
            vb = vd_ref[k0:k0 + 2 * blk, :]
            s = lax.dot_general(qb, kb, (((1,), (1,)), ((), ())), preferred_element_type=F32)
            s = s + (bias_ref[first_chunk] if b == 0 else bias_ref[0])
            m = jnp.max(s, axis=-1, keepdims=True)
            pr = jnp.exp2(s - m)
            ext = jnp.dot(pr.astype(BF16), vb, preferred_element_type=F32)
            acc, l = ext[:, :HEAD_DIM], ext[:, HEAD_DIM:]
            rows = pl.ds(b * (blk * dil) + r, blk, stride=dil)
            o_refs[p][rows, :] = acc / l
            lse_refs[p][rows, :] = m + jnp.log2(l)

        units.append([functools.partial(unit, r, b) for r in range(dil) for b in range(nb)])

    def mix_chunk(c):
        rows = slice(c * blk, (c + 1) * blk)
        lse = [ref[rows, :] for ref in lse_refs]
        top = functools.reduce(jnp.maximum, lse)
        w = [jnp.exp2(x - top) for x in lse]
        num = sum(wp * ref[rows, :] for wp, ref in zip(w, o_refs))
        o_ref[rows, :] = (num / sum(w)).astype(o_ref.dtype)

    assert DILATIONS[0] == 1 and len(DILATIONS) == 3
    for piece in regroup[0] + regroup[1]:
        piece()
    per_unit = len(regroup[2]) // len(units[1])
    for u, run_unit in enumerate(units[1]):
        run_unit()
        for piece in regroup[2][u * per_unit:(u + 1) * per_unit]:
            piece()
    for run_unit in units[2]:
        run_unit()
    for c, run_unit in enumerate(units[0]):
        run_unit()
        mix_chunk(c)


def _attn(h, cast_weights):
    s = h.shape[0]
    c = ATTN_CHUNK
    blk = SUB_BLOCK
    npat = len(DILATIONS)
    n_cast = len(cast_weights)
    n_chunks = s // c
    chunk = lambda col0: pl.BlockSpec((c, HEAD_DIM), lambda hh, n: (n, col0 + hh))
    cast_specs, cast_shapes = _slab_specs(cast_weights, n_chunks * ATTN_HEADS, lambda hh, n: hh * n_chunks + n)

    def body(q_ref, kc_ref, vc_ref, *rest):
        cast_in, o_ref, cast_out = rest[:n_cast], rest[n_cast], rest[n_cast + 1:2 * n_cast + 1]
        bias, scr = rest[2 * n_cast + 1], rest[2 * n_cast + 2:]
        for src, dst in zip(cast_in, cast_out):
            dst[...] = src[...].astype(BF16)
        groups = [scr[i * npat:(i + 1) * npat] for i in range(5)]
        _attn_kernel(q_ref, kc_ref, vc_ref, o_ref, bias, *groups, scr[5 * npat:])

    k_scratch = [pltpu.VMEM((dil * blk + c, HEAD_DIM), BF16) for dil in DILATIONS]
    v_scratch = [pltpu.VMEM((dil * blk + c, 2 * HEAD_DIM), BF16) for dil in DILATIONS]
    outs = pl.pallas_call(
        body,
        grid=(ATTN_HEADS, n_chunks),
        in_specs=[chunk(0), chunk(ATTN_HEADS), chunk(2 * ATTN_HEADS)] + cast_specs,
        out_specs=[pl.BlockSpec((c, HEAD_DIM), lambda hh, n: (n, hh))] + cast_specs,
        out_shape=[jax.ShapeDtypeStruct((s, ATTN_WIDTH), BF16)] + cast_shapes,
        scratch_shapes=(
            [pltpu.VMEM((2, blk, 2 * blk), F32)]
            + [pltpu.VMEM((c, HEAD_DIM), BF16)] * npat
            + k_scratch + v_scratch
            + [pltpu.VMEM((c, HEAD_DIM), F32)] * (2 * npat)
            + [pltpu.VMEM((c, HEAD_DIM), F32)] * 3),
        compiler_params=pltpu.CompilerParams(
            dimension_semantics=("arbitrary", "arbitrary"), vmem_limit_bytes=VMEM_LIMIT_BYTES),
        name="dilated_attn",
    )(h, h, h, *cast_weights)
    return outs[0], outs[1:]


def _mix_kernel(oat_ref, u_ref, uh_ref, ga_ref, gp_ref, x_ref, wpool_ref, pscale_ref,
                wba_ref, wbb_ref, wout_ref, g_ref, b_ref, out_ref, ubuf_ref, pm_ref, mg_ref):
    i = pl.program_id(0)
    tm = u_ref.shape[0]
    halo = MAX_POOL_WINDOW

    @pl.when(i == 0)
    def _():
        ubuf_ref[0:halo, :] = jnp.zeros((halo, POOL_WIDTH), F32)

    @pl.when(i > 0)
    def _():
        ubuf_ref[0:halo, :] = uh_ref[...]

    ubuf_ref[halo:halo + tm, :] = u_ref[...]

    hm = tm // MIX_PARTS
    parts = [slice(part * hm, (part + 1) * hm) for part in range(MIX_PARTS)]
    n_chunks = D_MODEL // MIX_TN
    chunk_cols = [slice(c * MIX_TN, (c + 1) * MIX_TN) for c in range(n_chunks)]

    def pool_group(part, g):
        rows, w = parts[part], POOL_WINDOWS[g]
        lo, hi = part * hm, halo + (part + 1) * hm
        t_glob = i * tm + part * hm + lax.broadcasted_iota(jnp.int32, (hm, 1), 0)
        cols = slice(g * POOL_GROUP_WIDTH, (g + 1) * POOL_GROUP_WIDTH)
        wsum, shift = ubuf_ref[lo:hi, cols], 1
        while shift < w:
            wsum = wsum + pltpu.roll(wsum, shift, 0)
            shift *= 2
        inv_count = 1.0 / jnp.minimum(t_glob + 1, w).astype(F32)
        pooled = wsum[halo:, :] * inv_count - u_ref[rows, cols]
        y = jnp.dot(pooled.astype(BF16), wpool_ref[g], preferred_element_type=F32)
        pm_ref[rows, cols] = (y * pscale_ref[:, cols]).astype(BF16)

    def attn_branch(part, c):
        rows, cols = parts[part], chunk_cols[c]
        y_attn = jnp.dot(oat_ref[rows, :], wba_ref[:, cols], preferred_element_type=F32)
        out_ref[rows, cols] = jax.nn.sigmoid(ga_ref[rows, cols].astype(F32)) * y_attn

    def pool_branch(part, c):
        rows, cols = parts[part], chunk_cols[c]
        y_pool = jnp.dot(pm_ref[rows, :], wbb_ref[:, cols], preferred_element_type=F32)
        merged = out_ref[rows, cols] + jax.nn.sigmoid(gp_ref[rows, cols].astype(F32)) * y_pool
        mg_ref[rows, cols] = merged.astype(BF16)

    def out_proj(part):
        rows = parts[part]
        out_ref[rows, :] = jnp.dot(mg_ref[rows, :], wout_ref[...], preferred_element_type=F32)

    def norm_rows(rows):
        out_ref[rows, :] = _layer_norm(DEEPNORM_ALPHA * x_ref[rows, :] + out_ref[rows, :], g_ref[...], b_ref[...])

    assert MIX_PARTS == 2 and n_chunks == len(POOL_WINDOWS)
    for part in range(MIX_PARTS):
        for c in range(n_chunks):
            attn_branch(part, c)
            pool_group(part, c)
    for c in range(n_chunks):
        pool_branch(0, c)
    out_proj(0)
    ln_rows = hm // n_chunks
    for c in range(n_chunks):
        pool_branch(1, c)
        norm_rows(slice(c * ln_rows, (c + 1) * ln_rows))
    out_proj(1)
    norm_rows(parts[1])


def _mix(o_attn, h_main, h_gates, x2, w_pool_b, pool_scale, w_ba_b, w_bb_b, w_out_b, ln_g, ln_b):
    s, d = x2.shape
    tm = MIX_TM
    halo = MAX_POOL_WINDOW
    u_col = (3 * ATTN_WIDTH) // POOL_WIDTH
    resident = lambda shape: pl.BlockSpec(shape, lambda i: (0,) * len(shape), pipeline_mode=pl.Buffered(1))
    return pl.pallas_call(
        _mix_kernel,
        grid=(s // tm,),
        in_specs=[
            pl.BlockSpec((tm, ATTN_WIDTH), lambda i: (i, 0)),
            pl.BlockSpec((tm, POOL_WIDTH), lambda i: (i, u_col)),
            pl.BlockSpec((halo, POOL_WIDTH), lambda i: (jnp.maximum(i * (tm // halo) - 1, 0), u_col)),
            pl.BlockSpec((tm, d), lambda i: (i, 0)),
            pl.BlockSpec((tm, d), lambda i: (i, 1)),
            pl.BlockSpec((tm, d), lambda i: (i, 0)),
            resident(w_pool_b.shape),
            resident((1, POOL_WIDTH)),
            resident(w_ba_b.shape),
            resident(w_bb_b.shape),
            resident(w_out_b.shape),
            resident((1, d)),
            resident((1, d)),
        ],
        out_specs=pl.BlockSpec((tm, d), lambda i: (i, 0)),
        out_shape=jax.ShapeDtypeStruct((s, d), F32),
        scratch_shapes=[
            pltpu.VMEM((halo + tm, POOL_WIDTH), F32),
            pltpu.VMEM((tm, POOL_WIDTH), BF16),
            pltpu.VMEM((tm, d), BF16),
        ],
        compiler_params=pltpu.CompilerParams(
            dimension_semantics=("arbitrary",), vmem_limit_bytes=VMEM_LIMIT_BYTES),
        name="mix_ln",
    )(o_attn, h_main, h_main, h_gates, h_gates, x2, w_pool_b, pool_scale, w_ba_b, w_bb_b, w_out_b, ln_g, ln_b)


def _ffn_kernel(x_ref, w1_ref, w2_ref, g_ref, b_ref, out_ref, xb_ref):
    j = pl.program_id(1)

    @pl.when(j == 0)
    def _():
        xb_ref[...] = x_ref[...].astype(BF16)
        out_ref[...] = jnp.zeros(out_ref.shape, F32)

    hid = jnp.dot(xb_ref[...], w1_ref[...], preferred_element_type=F32)
    hid = jnp.square(jnp.maximum(hid, 0.0)).astype(BF16)
    out_ref[...] += jnp.dot(hid, w2_ref[...], preferred_element_type=F32)

    @pl.when(j == pl.num_programs(1) - 1)
    def _():
        _residual_layer_norm(x_ref, out_ref, g_ref, b_ref, out_ref, FFN_LN_ROWS)


def _ffn(x1, w1_b, w2_b, ln_g, ln_b):
    s, d = x1.shape
    tm, tf = FFN_TM, FFN_TF
    return pl.pallas_call(
        _ffn_kernel,
        grid=(s // tm, D_FF // tf),
        in_specs=[
            pl.BlockSpec((tm, d), lambda i, j: (i, 0)),
            pl.BlockSpec((d, tf), lambda i, j: (0, j)),
            pl.BlockSpec((tf, d), lambda i, j: (j, 0)),
            pl.BlockSpec((1, d), lambda i, j: (0, 0)),
            pl.BlockSpec((1, d), lambda i, j: (0, 0)),
        ],
        out_specs=pl.BlockSpec((tm, d), lambda i, j: (i, 0)),
        out_shape=jax.ShapeDtypeStruct((s, d), F32),
        scratch_shapes=[pltpu.VMEM((tm, d), BF16)],
        compiler_params=pltpu.CompilerParams(
            dimension_semantics=("arbitrary", "arbitrary"), vmem_limit_bytes=VMEM_LIMIT_BYTES),
        name="ffn_ln",
    )(x1, w1_b, w2_b, ln_g, ln_b)


def kernel(x, positions, w_in, w_pool, pool_scale, w_branch_attn, w_branch_pool, w_out,
           ln_mix_g, ln_mix_b, w_ff1, w_ff2, ln_ff_g, ln_ff_b):
    b, s, d = x.shape
    assert (b, s, d) == (1, SEQ, D_MODEL) and w_in.shape[0] == DEPTH
    half = HEAD_DIM // 2
    inv_freq = ROPE_THETA ** (-jnp.arange(half, dtype=F32) / half)
    invf2 = jnp.concatenate([inv_freq, inv_freq]).reshape(1, HEAD_DIM)
    x2 = x.reshape(s, d)
    pos2 = positions.reshape(s, 1)
    for layer in range(DEPTH):
        w_pool2 = w_pool[layer].reshape(POOL_WIDTH, POOL_GROUP_WIDTH)
        h_main, h_gates, _ = _proj(x2, pos2, invf2, w_in[layer].astype(BF16), [])
        o_attn, (w_ba_b, w_bb_b, w_out_b, w_pool_b, w1_b, w2_b) = _attn(
            h_main, [w_branch_attn[layer], w_branch_pool[layer], w_out[layer], w_pool2, w_ff1[layer], w_ff2[layer]])
        x2 = _mix(o_attn, h_main, h_gates, x2, w_pool_b.reshape(w_pool[layer].shape),
                  pool_scale[layer].reshape(1, POOL_WIDTH), w_ba_b, w_bb_b, w_out_b,
                  ln_mix_g[layer].reshape(1, d), ln_mix_b[layer].reshape(1, d))
        x2 = _ffn(x2, w1_b, w2_b, ln_ff_g[layer].reshape(1, d), ln_ff_b[layer].reshape(1, d))
    return x2.reshape(b, s, d)
```
